```python
import math
import jax, jax.numpy as jnp
from jax import lax
import numpy as np

D_MODEL = 1024
BATCH = 2
SEQ = 8192
DEPTH = 2

D_PLE = 256
GRID_W = 64
EPS = 1e-6

D_MIX = D_MODEL
D_SSM = D_MIX // 2
SSM_GROUP = 16
SSM_GROUPS = D_SSM // SSM_GROUP
SSM_STATE = 64
DT_MIN = 1e-3
DT_MAX = 1e-1

D_ATTN = D_MIX - D_SSM
HEAD_DIM = 64
N_Q_HEADS = D_ATTN // HEAD_DIM
N_KV_HEADS = 2
Q_PER_KV = N_Q_HEADS // N_KV_HEADS
D_KV = N_KV_HEADS * HEAD_DIM
D_IN = D_SSM + D_ATTN + 2 * D_KV
Q_BLOCK = 128
ROPE_THETA = 10000.0

PEER_HEADS = 8
PEER_KEY_DIM = 256
PEER_HALF = PEER_KEY_DIM // 2
N_KEYS = 128
N_EXPERTS = N_KEYS * N_KEYS
PEER_TOPK = 16
TOKEN_CHUNK = 128

kernel_name = "hybrid_s5_axialgqa_peer_encoder"


def rmsnorm(x, g):
    xf = x.astype(jnp.float32)
    y = xf * lax.rsqrt(jnp.mean(xf * xf, axis=-1, keepdims=True) + EPS)
    return (y * g.astype(jnp.float32)).astype(x.dtype)


def _scan_combine(e1, e2):
    a1r, a1i, b1r, b1i = e1
    a2r, a2i, b2r, b2i = e2
    return (a2r * a1r - a2i * a1i,
            a2r * a1i + a2i * a1r,
            a2r * b1r - a2i * b1i + b2r,
            a2r * b1i + a2i * b1r + b2i)


def s5_direction(u, lam_re, lam_im, log_dt, b_re, b_im, c_re, c_im, reverse):
    f32 = jnp.float32
    lam_re = lam_re.astype(f32)
    lam_im = lam_im.astype(f32)
    dt = jnp.exp(log_dt.astype(f32))[:, None]
    decay = jnp.exp(lam_re * dt)
    ab_re = decay * jnp.cos(lam_im * dt)
    ab_im = decay * jnp.sin(lam_im * dt)
    den = lam_re * lam_re + lam_im * lam_im
    nr = ab_re - 1.0
    f_re = (nr * lam_re + ab_im * lam_im) / den
    f_im = (ab_im * lam_re - nr * lam_im) / den
    b_re = b_re.astype(f32)
    b_im = b_im.astype(f32)
    bb_re = f_re[..., None] * b_re - f_im[..., None] * b_im
    bb_im = f_re[..., None] * b_im + f_im[..., None] * b_re
    bu_re = jnp.einsum('gnp,bsgp->bsgn', bb_re, u)
    bu_im = jnp.einsum('gnp,bsgp->bsgn', bb_im, u)
    a_re = jnp.broadcast_to(ab_re, bu_re.shape)
    a_im = jnp.broadcast_to(ab_im, bu_im.shape)
    _, _, h_re, h_im = lax.associative_scan(
        _scan_combine, (a_re, a_im, bu_re, bu_im), reverse=reverse, axis=1)
    return (jnp.einsum('gpn,bsgn->bsgp', c_re.astype(f32), h_re)
            - jnp.einsum('gpn,bsgn->bsgp', c_im.astype(f32), h_im))


def s5_mixer(u_in, lam_re, lam_im, log_dt, b_re, b_im, c_re, c_im, d_skip, w_glu, b_glu):
    f32 = jnp.float32
    B_, S_, _ = u_in.shape
    uf = u_in.astype(f32)
    u = uf.reshape(B_, S_, SSM_GROUPS, SSM_GROUP)
    y = (s5_direction(u, lam_re[0], lam_im[0], log_dt[0], b_re[0], b_im[0], c_re[0], c_im[0], False)
         + s5_direction(u, lam_re[1], lam_im[1], log_dt[1], b_re[1], b_im[1], c_re[1], c_im[1], True))
    y = y.reshape(B_, S_, D_SSM) + d_skip.astype(f32) * uf
    y = jax.nn.gelu(y)
    y = y * jax.nn.sigmoid(y @ w_glu.astype(f32) + b_glu.astype(f32))
    return y.astype(u_in.dtype)


def axial_rope_tables(n_tokens):
    f32 = jnp.float32
    n_rows = n_tokens // GRID_W
    row = jnp.repeat(jnp.arange(n_rows, dtype=jnp.int32), GRID_W).astype(f32)
    col = jnp.tile(jnp.arange(GRID_W, dtype=jnp.int32), n_rows).astype(f32)
    quarter = HEAD_DIM // 4
    inv = ROPE_THETA ** (-jnp.arange(quarter, dtype=f32) / quarter)
    ang_r = row[:, None] * inv[None, :]
    ang_c = col[:, None] * inv[None, :]
    return jnp.cos(ang_r), jnp.sin(ang_r), jnp.cos(ang_c), jnp.sin(ang_c)


def _rope_half(x, cos, sin):
    c = cos[None, :, None, :]
    s = sin[None, :, None, :]
    x1, x2 = x[..., :HEAD_DIM // 4], x[..., HEAD_DIM // 4:]
    return jnp.concatenate([x1 * c - x2 * s, x2 * c + x1 * s], axis=-1)


def apply_axial_rope(x, tables):
    cos_r, sin_r, cos_c, sin_c = tables
    xf = x.astype(jnp.float32)
    half = HEAD_DIM // 2
    out = jnp.concatenate([_rope_half(xf[..., :half], cos_r, sin_r),
                           _rope_half(xf[..., half:], cos_c, sin_c)], axis=-1)
    return out.astype(x.dtype)


def gqa_attention(q, k, v):
    B_, S_ = q.shape[:2]
    nb = S_ // Q_BLOCK
    qb = q.reshape(B_, nb, Q_BLOCK, N_KV_HEADS, Q_PER_KV, HEAD_DIM).transpose(1, 0, 3, 4, 2, 5)
    kt = k.transpose(0, 2, 1, 3)
    vt = v.transpose(0, 2, 1, 3)
    scale = HEAD_DIM ** -0.5

    def block(qblk):
        s = jnp.einsum('bkgqd,bksd->bkgqs', qblk, kt,
                       preferred_element_type=jnp.float32) * scale
        w = jax.nn.softmax(s, axis=-1)
        return jnp.einsum('bkgqs,bksd->bkgqd', w.astype(vt.dtype), vt)

    o = lax.map(block, qb)
    return o.transpose(1, 0, 4, 2, 3, 5).reshape(B_, S_, N_Q_HEADS * HEAD_DIM)


def peer_ffn(h, w_query, sub_keys1, sub_keys2, expert_u, expert_v):
    B_, S_, D_ = h.shape
    n_chunks = (B_ * S_) // TOKEN_CHUNK
    xt = h.reshape(n_chunks, TOKEN_CHUNK, D_)

    def chunk(xc):
        q = (xc @ w_query).reshape(TOKEN_CHUNK, PEER_HEADS, PEER_KEY_DIM)
        s1 = jnp.einsum('chd,hkd->chk', q[..., :PEER_HALF], sub_keys1).astype(jnp.float32)
        s2 = jnp.einsum('chd,hkd->chk', q[..., PEER_HALF:], sub_keys2).astype(jnp.float32)
        v1, i1 = lax.top_k(s1, PEER_TOPK)
        v2, i2 = lax.top_k(s2, PEER_TOPK)
        cand = (v1[..., :, None] + v2[..., None, :]).reshape(TOKEN_CHUNK, PEER_HEADS, PEER_TOPK * PEER_TOPK)
        cidx = (i1[..., :, None] * N_KEYS + i2[..., None, :]).reshape(TOKEN_CHUNK, PEER_HEADS, PEER_TOPK * PEER_TOPK)
        sc, pos = lax.top_k(cand, PEER_TOPK)
        idx = jnp.take_along_axis(cidx, pos, axis=-1)
        g = jax.nn.softmax(sc, axis=-1).astype(xc.dtype)
        u_sel = expert_u[idx]
        v_sel = expert_v[idx]
        a = jax.nn.gelu(jnp.einsum('chkd,cd->chk', u_sel, xc)) * g
        return jnp.einsum('chk,chkd->cd', a, v_sel)

    return lax.map(chunk, xt).reshape(B_, S_, D_)


def setup_inputs(seed: int = 0) -> dict:
    key = jax.random.key(seed)
    ks = jax.random.split(key, 32)
    f32 = jnp.float32
    L, G, N, P = DEPTH, SSM_GROUPS, SSM_STATE, SSM_GROUP

    def nrm(k, shape, scale):
        return jax.random.normal(k, shape, f32) * scale

    def gain(k, shape):
        return 1.0 + 0.05 * jax.random.normal(k, shape, f32)

    n_idx = jnp.arange(N, dtype=f32)
    lam_re = -0.5 + 0.01 * jax.random.normal(ks[4], (L, 2, G, N), f32)
    lam_im = math.pi * n_idx + 0.01 * jax.random.normal(ks[5], (L, 2, G, N), f32)
    u01 = jax.random.uniform(ks[6], (L, 2, G), f32)
    log_dt = math.log(DT_MIN) + u01 * (math.log(DT_MAX) - math.log(DT_MIN))

    return {
        "x": nrm(ks[0], (BATCH, SEQ, D_MODEL), 1.0),
        "p": nrm(ks[1], (DEPTH, BATCH, SEQ, D_PLE), 1.0),
        "norm_mix_g": gain(ks[2], (L, D_MODEL)),
        "w_in": nrm(ks[3], (L, D_MODEL, D_IN), D_MODEL ** -0.5),
        "lam_re": lam_re,
        "lam_im": lam_im,
        "log_dt": log_dt,
        "b_re": nrm(ks[7], (L, 2, G, N, P), (2.0 * P) ** -0.5),
        "b_im": nrm(ks[8], (L, 2, G, N, P), (2.0 * P) ** -0.5),
        "c_re": nrm(ks[9], (L, 2, G, P, N), (2.0 * N) ** -0.5),
        "c_im": nrm(ks[10], (L, 2, G, P, N), (2.0 * N) ** -0.5),
        "d_skip": nrm(ks[11], (L, D_SSM), 1.0),
        "w_glu": nrm(ks[12], (L, D_SSM, D_SSM), D_SSM ** -0.5),
        "b_glu": nrm(ks[13], (L, D_SSM), 0.01),
        "q_norm_g": gain(ks[14], (L, HEAD_DIM)),
        "k_norm_g": gain(ks[15], (L, HEAD_DIM)),
        "ssm_out_g": gain(ks[16], (L, D_SSM)),
        "attn_out_g": gain(ks[17], (L, D_ATTN)),
        "w_out": nrm(ks[18], (L, D_MIX, D_MODEL), D_MIX ** -0.5),
        "norm_ffn_g": gain(ks[19], (L, D_MODEL)),
        "w_query": nrm(ks[20], (L, D_MODEL, PEER_HEADS * PEER_KEY_DIM), D_MODEL ** -0.5),
        "sub_keys1": nrm(ks[21], (L, PEER_HEADS, N_KEYS, PEER_HALF), PEER_HALF ** -0.5),
        "sub_keys2": nrm(ks[22], (L, PEER_HEADS, N_KEYS, PEER_HALF), PEER_HALF ** -0.5),
        "expert_u": nrm(ks[23], (L, N_EXPERTS, D_MODEL), D_MODEL ** -0.5),
        "expert_v": nrm(ks[24], (L, N_EXPERTS, D_MODEL), 0.5),
        "norm_ple_g": gain(ks[25], (L, D_MODEL)),
        "w_ple_gate": nrm(ks[26], (L, D_MODEL, D_MODEL), D_MODEL ** -0.5),
        "w_ple_proj": nrm(ks[27], (L, D_PLE, D_MODEL), D_PLE ** -0.5),
        "final_norm_g": gain(ks[28], (D_MODEL,)),
    }


def reference(x, p, norm_mix_g, w_in, lam_re, lam_im, log_dt, b_re, b_im, c_re, c_im,
              d_skip, w_glu, b_glu, q_norm_g, k_norm_g, ssm_out_g, attn_out_g, w_out,
              norm_ffn_g, w_query, sub_keys1, sub_keys2, expert_u, expert_v,
              norm_ple_g, w_ple_gate, w_ple_proj, final_norm_g):
    B_, S_, _ = x.shape
    tables = axial_rope_tables(S_)
    for i in range(DEPTH):
        h = rmsnorm(x, norm_mix_g[i])
        z = h @ w_in[i]
        u_ssm = z[..., :D_SSM]
        q = z[..., D_SSM:D_SSM + D_ATTN].reshape(B_, S_, N_Q_HEADS, HEAD_DIM)
        k = z[..., D_SSM + D_ATTN:D_SSM + D_ATTN + D_KV].reshape(B_, S_, N_KV_HEADS, HEAD_DIM)
        v = z[..., D_SSM + D_ATTN + D_KV:].reshape(B_, S_, N_KV_HEADS, HEAD_DIM)

        y_ssm = s5_mixer(u_ssm, lam_re[i], lam_im[i], log_dt[i], b_re[i], b_im[i],
                         c_re[i], c_im[i], d_skip[i], w_glu[i], b_glu[i])

        q = apply_axial_rope(rmsnorm(q, q_norm_g[i]), tables)
        k = apply_axial_rope(rmsnorm(k, k_norm_g[i]), tables)
        y_attn = gqa_attention(q, k, v)

        y = jnp.concatenate([rmsnorm(y_ssm, ssm_out_g[i]), rmsnorm(y_attn, attn_out_g[i])], axis=-1)
        x = x + y @ w_out[i]

        x = x + peer_ffn(rmsnorm(x, norm_ffn_g[i]), w_query[i], sub_keys1[i], sub_keys2[i],
                         expert_u[i], expert_v[i])

        gate = jax.nn.sigmoid(rmsnorm(x, norm_ple_g[i]) @ w_ple_gate[i])
        x = x + gate * (p[i] @ w_ple_proj[i])
    return rmsnorm(x, final_norm_g)
```

```python
import functools
import math

import jax
import jax.numpy as jnp
from jax import lax
from jax.experimental import pallas as pl
from jax.experimental.pallas import tpu as pltpu

F32 = jnp.float32
BF16 = jnp.bfloat16

D_MODEL = 1024
D_PLE = 256
GRID_W = 64
EPS = 1e-6
D_SSM = 512
SSM_P = 16
SSM_G = D_SSM // SSM_P
SSM_N = 64
D_ATTN = 512
HEAD_DIM = 64
N_Q_HEADS = 8
N_KV_HEADS = 2
Q_PER_KV = 4
D_KV = 128
D_IN = D_SSM + D_ATTN + 2 * D_KV
ROPE_THETA = 10000.0
PEER_HEADS = 8
PEER_KEY_DIM = 256
PEER_HALF = 128
N_KEYS = 128
PEER_TOPK = 16

SSM_CHUNK = 32
SSM_LP = SSM_CHUNK * SSM_P
VMEM_LIMIT = 48 * 1024 * 1024


def _cparams(*sem):
    return pltpu.CompilerParams(dimension_semantics=sem, vmem_limit_bytes=VMEM_LIMIT)


def _rms(x, g):
    return x * lax.rsqrt(jnp.mean(x * x, axis=-1, keepdims=True) + EPS) * g


def _in_proj_kernel(x_ref, g_ref, w_ref, cq_ref, sq_ref, gq_ref, gk_ref, bd_ref,
                    u_ref, q_ref, k_ref, v_ref):
    h = _rms(x_ref[...], g_ref[...])
    z = jnp.dot(h.astype(BF16), w_ref[...], preferred_element_type=F32)
    u_ref[...] = z[:, :D_SSM]

    def head_norm_rope(t, gain, bd, c, s):
        ss = jnp.dot((t * t).astype(BF16), bd, preferred_element_type=F32)
        t = t * lax.rsqrt(ss * (1.0 / HEAD_DIM) + EPS) * gain
        w = t.shape[1]
        lane = lax.broadcasted_iota(jnp.int32, t.shape, 1)
        partner = jnp.where((lane % 32) < 16, pltpu.roll(t, w - 16, 1), pltpu.roll(t, 16, 1))
        return t * c + partner * s

    cq = cq_ref[...]
    sq = sq_ref[...]
    q = head_norm_rope(z[:, D_SSM:D_SSM + D_ATTN], gq_ref[...], bd_ref[...], cq, sq)
    q_ref[...] = (q * (HEAD_DIM ** -0.5)).astype(BF16)
    k = head_norm_rope(z[:, D_SSM + D_ATTN:D_SSM + D_ATTN + D_KV], gk_ref[...],
                       bd_ref[:D_KV, :D_KV], cq[:, :D_KV], sq[:, :D_KV])
    v = z[:, D_SSM + D_ATTN + D_KV:]
    lane = lax.broadcasted_iota(jnp.int32, (v.shape[0], HEAD_DIM), 1)
    ones_col = jnp.where(lane == 0, 1.0, 0.0).astype(F32)
    for j in range(N_KV_HEADS):
        k_ref[j] = k[:, j * HEAD_DIM:(j + 1) * HEAD_DIM].astype(BF16)
        v_ref[j] = jnp.concatenate([v[:, j * HEAD_DIM:(j + 1) * HEAD_DIM], ones_col], axis=1).astype(BF16)


def _in_proj(x, g, w, cq, sq, gq, gk, bd, seq, tm):
    t = x.shape[0]
    nseq = seq // tm
    return pl.pallas_call(
        _in_proj_kernel,
        grid=(t // tm,),
        in_specs=[
            pl.BlockSpec((tm, D_MODEL), lambda i: (i, 0)),
            pl.BlockSpec((1, D_MODEL), lambda i: (0, 0)),
            pl.BlockSpec((D_MODEL, D_IN), lambda i: (0, 0)),
            pl.BlockSpec((tm, D_ATTN), lambda i: (i % nseq, 0)),
            pl.BlockSpec((tm, D_ATTN), lambda i: (i % nseq, 0)),
            pl.BlockSpec((1, D_ATTN), lambda i: (0, 0)),
            pl.BlockSpec((1, D_KV), lambda i: (0, 0)),
            pl.BlockSpec((D_ATTN, D_ATTN), lambda i: (0, 0)),
        ],
        out_specs=[
            pl.BlockSpec((tm, D_SSM), lambda i: (i, 0)),
            pl.BlockSpec((tm, D_ATTN), lambda i: (i, 0)),
            pl.BlockSpec((N_KV_HEADS, tm, HEAD_DIM), lambda i: (0, i, 0)),
            pl.BlockSpec((N_KV_HEADS, tm, 2 * HEAD_DIM), lambda i: (0, i, 0)),
        ],
        out_shape=[
            jax.ShapeDtypeStruct((t, D_SSM), F32),
            jax.ShapeDtypeStruct((t, D_ATTN), BF16),
            jax.ShapeDtypeStruct((N_KV_HEADS, t, HEAD_DIM), BF16),
            jax.ShapeDtypeStruct((N_KV_HEADS, t, 2 * HEAD_DIM), BF16),
        ],
        compiler_params=_cparams("parallel"),
        name="in_proj",
    )(x, g, w, cq, sq, gq, gk, bd)


def _ssm_params_kernel(lr_ref, li_ref, ldt_ref, bt_re_ref, bt_im_ref, c_re_ref, c_im_ref,
                       mt_ref, et_ref, fm_ref, dec_ref):
    L = SSM_CHUNK
    hi = lax.Precision.HIGHEST
    r_parts, e_parts, f_parts, p_parts, q_parts = [], [], [], [], []
    idx = lax.broadcasted_iota(jnp.int32, (L, 1, SSM_N), 0).astype(F32)
    for d in range(2):
        lr = lr_ref[0, d]
        li = li_ref[0, d]
        dt = jnp.exp(ldt_ref[0, d])

        def apow(tau):
            mag = jnp.exp(tau * (lr * dt))
            ang = tau * (li * dt)
            return mag * jnp.cos(ang), mag * jnp.sin(ang)

        ab_re, ab_im = apow(jnp.ones((1, SSM_N), F32))
        den = lr * lr + li * li
        nr = ab_re - 1.0
        f_re = (nr * lr + ab_im * li) / den
        f_im = (ab_im * lr - nr * li) / den
        bt_re = bt_re_ref[0, d]
        bt_im = bt_im_ref[0, d]
        bb_re = f_re * bt_re - f_im * bt_im
        bb_im = f_re * bt_im + f_im * bt_re
        c_re = c_re_ref[0, d]
        c_im = c_im_ref[0, d]

        tau_k = idx if d == 0 else (L - 1.0) - idx
        pr, pi = apow(tau_k)
        lm_re = (c_re[None] * pr - c_im[None] * pi).reshape(L * SSM_P, SSM_N)
        lm_im = (c_re[None] * pi + c_im[None] * pr).reshape(L * SSM_P, SSM_N)
        nt = (((1,), (1,)), ((), ()))
        r_parts.append(lax.dot_general(bb_re, lm_re, nt, precision=hi, preferred_element_type=F32)
                       - lax.dot_general(bb_im, lm_im, nt, precision=hi, preferred_element_type=F32))

        tau_e = (L - 1.0) - idx if d == 0 else idx
        pr, pi = apow(tau_e)
        e_parts.append((pr * bb_re[None] - pi * bb_im[None]).reshape(L * SSM_P, SSM_N))
        e_parts.append((pr * bb_im[None] + pi * bb_re[None]).reshape(L * SSM_P, SSM_N))

        tau_f = idx + 1.0 if d == 0 else L - idx
        pr, pi = apow(tau_f)
        f_parts.append((c_re[None] * pr - c_im[None] * pi).reshape(L * SSM_P, SSM_N))
        f_parts.append(-(c_re[None] * pi + c_im[None] * pr).reshape(L * SSM_P, SSM_N))

        al_re, al_im = apow(jnp.full((1, SSM_N), float(L), F32))
        p_parts += [al_re, al_re]
        q_parts += [-al_im, al_im]

    zeros = jnp.zeros((SSM_P, SSM_LP), F32)
    r_bwd = jnp.concatenate([r_parts[1], zeros], axis=1)
    r_fwd = pltpu.roll(jnp.concatenate([r_parts[0], zeros], axis=1), SSM_LP - SSM_P, 1)
    r_cat = r_bwd + r_fwd
    for s in range(L):
        shift = (2 * SSM_LP - (L - 1 - s) * SSM_P) % (2 * SSM_LP)
        win = r_cat if shift == 0 else pltpu.roll(r_cat, shift, 1)
        mt_ref[0, s * SSM_P:(s + 1) * SSM_P, :] = win[:, :SSM_LP].astype(BF16)
    et_ref[0] = jnp.concatenate(e_parts, axis=1).astype(BF16)
    fm_ref[0] = jnp.concatenate(f_parts, axis=1).astype(BF16)
    dec_ref[0] = jnp.concatenate([jnp.concatenate(p_parts, axis=1),
                                  jnp.concatenate(q_parts, axis=1)], axis=0)


def _ssm_params(lr, li, ldt, bt_re, bt_im, c_re, c_im):
    g = lr.shape[0]
    vec = pl.BlockSpec((1, 2, 1, SSM_N), lambda i: (i, 0, 0, 0))
    mat = pl.BlockSpec((1, 2, SSM_P, SSM_N), lambda i: (i, 0, 0, 0))
    return pl.pallas_call(
        _ssm_params_kernel,
        grid=(g,),
        in_specs=[vec, vec, vec, mat, mat, mat, mat],
        out_specs=[
            pl.BlockSpec((1, SSM_LP, SSM_LP), lambda i: (i, 0, 0)),
            pl.BlockSpec((1, SSM_LP, 4 * SSM_N), lambda i: (i, 0, 0)),
            pl.BlockSpec((1, SSM_LP, 4 * SSM_N), lambda i: (i, 0, 0)),
            pl.BlockSpec((1, 2, 4 * SSM_N), lambda i: (i, 0, 0)),
        ],
        out_shape=[
            jax.ShapeDtypeStruct((g, SSM_LP, SSM_LP), BF16),
            jax.ShapeDtypeStruct((g, SSM_LP, 4 * SSM_N), BF16),
            jax.ShapeDtypeStruct((g, SSM_LP, 4 * SSM_N), BF16),
            jax.ShapeDtypeStruct((g, 2, 4 * SSM_N), F32),
        ],
        compiler_params=_cparams("parallel"),
        name="ssm_params",
    )(lr, li, ldt, bt_re, bt_im, c_re, c_im)


def _ssm_conv_kernel(u_ref, mt_ref, et_ref, fm_ref, dec_ref, y_ref, *, n_batch):
    u = u_ref[0]
    nc = u.shape[0]
    ncb = nc // n_batch
    w = 2 * SSM_N
    s_all = jnp.dot(u, et_ref[0], preferred_element_type=F32)
    dec = dec_ref[0]
    row = lax.broadcasted_iota(jnp.int32, (ncb, w), 0)

    def cmul(x, p, q):
        return p * x + q * pltpu.roll(x, SSM_N, 1)

    h_rows = []
    for b in range(n_batch):
        s_b = s_all[b * ncb:(b + 1) * ncb]
        xf, xb = s_b[:, :w], s_b[:, w:]
        pf, qf = dec[0:1, :w], dec[1:2, :w]
        pb, qb = dec[0:1, w:], dec[1:2, w:]
        k = 1
        while k < ncb:
            xf = xf + jnp.where(row >= k, cmul(pltpu.roll(xf, k, 0), pf, qf), 0.0)
            xb = xb + jnp.where(row < ncb - k, cmul(pltpu.roll(xb, ncb - k, 0), pb, qb), 0.0)
            pf, qf = pf * pf - qf * qf, 2.0 * pf * qf
            pb, qb = pb * pb - qb * qb, 2.0 * pb * qb
            k *= 2
        hf = jnp.where(row >= 1, pltpu.roll(xf, 1, 0), 0.0)
        hb = jnp.where(row < ncb - 1, pltpu.roll(xb, ncb - 1, 0), 0.0)
        h_rows.append(jnp.concatenate([hf, hb], axis=1))
    h = jnp.concatenate(h_rows, axis=0).astype(BF16)
    y = jnp.dot(u, mt_ref[0], preferred_element_type=F32)
    y = y + lax.dot_general(h, fm_ref[0], (((1,), (1,)), ((), ())), preferred_element_type=F32)
    y_ref[0] = y


def _ssm_conv(ug, mt, et, fm, dec, n_batch):
    g, nc, lp = ug.shape
    return pl.pallas_call(
        functools.partial(_ssm_conv_kernel, n_batch=n_batch),
        grid=(g,),
        in_specs=[
            pl.BlockSpec((1, nc, lp), lambda i: (i, 0, 0)),
            pl.BlockSpec((1, lp, lp), lambda i: (i, 0, 0)),
            pl.BlockSpec((1, lp, 4 * SSM_N), lambda i: (i, 0, 0)),
            pl.BlockSpec((1, lp, 4 * SSM_N), lambda i: (i, 0, 0)),
            pl.BlockSpec((1, 2, 4 * SSM_N), lambda i: (i, 0, 0)),
        ],
        out_specs=pl.BlockSpec((1, nc, lp), lambda i: (i, 0, 0)),
        out_shape=jax.ShapeDtypeStruct((g, nc, lp), F32),
        compiler_params=_cparams("parallel"),
        name="ssm_conv",
    )(ug, mt, et, fm, dec)


def _attn_kernel(q_ref, k_ref, v_ref, o_ref, qs_ref, m_ref, acc_ref, *, tq, tk, nk):
    q = q_ref[...]
    for j in range(Q_PER_KV):
        qs_ref[j * tq:(j + 1) * tq, :] = q[:, j * HEAD_DIM:(j + 1) * HEAD_DIM]
    m_ref[...] = jnp.full(m_ref.shape, -jnp.inf, F32)
    acc_ref[...] = jnp.zeros(acc_ref.shape, F32)

    def body(i, carry):
        start = pl.multiple_of(i * tk, tk)
        kb = k_ref[0, pl.ds(start, tk), :]
        vb = v_ref[0, pl.ds(start, tk), :]
        s = lax.dot_general(qs_ref[...], kb, (((1,), (1,)), ((), ())), preferred_element_type=F32)
        m_prev = m_ref[...]
        m_new = jnp.maximum(m_prev, jnp.max(s, axis=1, keepdims=True))
        alpha = jnp.exp(m_prev - m_new)
        p = jnp.exp(s - jnp.concatenate([m_new] * (tk // 128), axis=1))
        acc_ref[...] = acc_ref[...] * alpha + jnp.dot(p.astype(BF16), vb, preferred_element_type=F32)
        m_ref[...] = m_new
        return carry

    lax.fori_loop(0, nk, body, 0)
    acc = acc_ref[...]
    out = acc[:, :HEAD_DIM] / acc[:, HEAD_DIM:HEAD_DIM + 1]
    o_ref[...] = jnp.concatenate([out[j * tq:(j + 1) * tq] for j in range(Q_PER_KV)], axis=1)


def _attention(q, k, v, n_batch, seq, tq, tk):
    t = q.shape[0]
    nq = seq // tq
    w = Q_PER_KV * HEAD_DIM
    return pl.pallas_call(
        functools.partial(_attn_kernel, tq=tq, tk=tk, nk=seq // tk),
        grid=(n_batch, N_KV_HEADS, nq),
        in_specs=[
            pl.BlockSpec((tq, w), lambda b, g, i: (b * nq + i, g)),
            pl.BlockSpec((1, seq, HEAD_DIM), lambda b, g, i: (g, b, 0)),
            pl.BlockSpec((1, seq, 2 * HEAD_DIM), lambda b, g, i: (g, b, 0)),
        ],
        out_specs=pl.BlockSpec((tq, w), lambda b, g, i: (b * nq + i, g)),
        out_shape=jax.ShapeDtypeStruct((t, D_ATTN), F32),
        scratch_shapes=[
            pltpu.VMEM((Q_PER_KV * tq, HEAD_DIM), BF16),
            pltpu.VMEM((Q_PER_KV * tq, 128), F32),
            pltpu.VMEM((Q_PER_KV * tq, 2 * HEAD_DIM), F32),
        ],
        compiler_params=_cparams("parallel", "parallel", "parallel"),
        name="attention",
    )(q, k, v)


def _out_proj_kernel(x_ref, ys_ref, u_ref, at_ref, dsk_ref, wglu_ref, bglu_ref, gs_ref, ga_ref,
                     wo_ref, gf_ref, xo_ref, xn_ref):
    y = ys_ref[...] + dsk_ref[...] * u_ref[...]
    y = jax.nn.gelu(y)
    z = jnp.dot(y.astype(BF16), wglu_ref[...], preferred_element_type=F32) + bglu_ref[...]
    y = y * jax.nn.sigmoid(z)
    ys = _rms(y, gs_ref[...]).astype(BF16)
    ya = _rms(at_ref[...], ga_ref[...]).astype(BF16)
    xo = (x_ref[...]
          + jnp.dot(ys, wo_ref[:D_SSM, :], preferred_element_type=F32)
          + jnp.dot(ya, wo_ref[D_SSM:, :], preferred_element_type=F32))
    xo_ref[...] = xo
    xn_ref[...] = _rms(xo, gf_ref[...]).astype(BF16)


def _out_proj(x, ys, u, at, dsk, wglu, bglu, gs, ga, wo, gf, tm):
    t = x.shape[0]
    row = lambda w: pl.BlockSpec((tm, w), lambda i: (i, 0))
    full = lambda a, b: pl.BlockSpec((a, b), lambda i: (0, 0))
    return pl.pallas_call(
        _out_proj_kernel,
        grid=(t // tm,),
        in_specs=[row(D_MODEL), row(D_SSM), row(D_SSM), row(D_ATTN), full(1, D_SSM),
                  full(D_SSM, D_SSM), full(1, D_SSM), full(1, D_SSM), full(1, D_ATTN),
                  full(D_MODEL, D_MODEL), full(1, D_MODEL)],
        out_specs=[row(D_MODEL), row(D_MODEL)],
        out_shape=[jax.ShapeDtypeStruct((t, D_MODEL), F32), jax.ShapeDtypeStruct((t, D_MODEL), BF16)],
        compiler_params=_cparams("parallel"),
        name="out_proj",
    )(x, ys, u, at, dsk, wglu, bglu, gs, ga, wo, gf)


def _oe_sort_network(n):
    pairs = []
    p = 1
    while p < n:
        k = p
        while k >= 1:
            for j in range(k % p, n - k, 2 * k):
                for i in range(min(k, n - j - k)):
                    if (i + j) // (2 * p) == (i + j + k) // (2 * p):
                        pairs.append((i + j, i + j + k))
            k //= 2
        p *= 2
    return pairs


def _ce(a, b):
    if b is None:
        return a, None
    if a is None:
        return b, None
    return jnp.maximum(a, b), jnp.minimum(a, b)


def _sorted_top16(slabs):
    n = PEER_TOPK
    v = list(slabs) + [None] * (n - len(slabs))
    for i, j in _oe_sort_network(n):
        v[i], v[j] = _ce(v[i], v[j])
    for shift in (4, 2, 1):
        partner = [None if x is None else pltpu.roll(x, shift, 0) for x in v]
        top = []
        for a in range(n):
            x, y = v[a], partner[n - 1 - a]
            top.append(y if x is None else (x if y is None else jnp.maximum(x, y)))
        v = top
        d = n // 2
        while d >= 1:
            for i in range(n):
                if i & d == 0:
                    v[i], v[i + d] = _ce(v[i], v[i + d])
            d //= 2
    return v


_CAND = [(a, b) for a in range(PEER_TOPK) for b in range(PEER_TOPK) if (a + 1) * (b + 1) <= PEER_TOPK]


def _peer_route_kernel(xn_ref, wq_ref, k1_ref, k2_ref, rho_ref, e2_ref, r1_ref, c1_ref):
    nt = (((1,), (1,)), ((), ()))
    qt = lax.dot_general(wq_ref[...], xn_ref[...], nt, preferred_element_type=F32)
    s1 = jnp.dot(k1_ref[0], qt[:PEER_HALF].astype(BF16), preferred_element_type=F32)
    s2 = jnp.dot(k2_ref[0], qt[PEER_HALF:].astype(BF16), preferred_element_type=F32)
    tm = s1.shape[1]
    nslab = N_KEYS // 8
    s1 = [s1[8 * i:8 * i + 8] for i in range(nslab)]
    s2 = [s2[8 * i:8 * i + 8] for i in range(nslab)]
    v1 = _sorted_top16(s1)
    v2 = _sorted_top16(s2)

    sub = lax.broadcasted_iota(jnp.int32, (8, tm), 0)
    cand = [v1[a] + v2[b] for a, b in _CAND]
    packed = []
    for base in range(0, len(cand), 8):
        slab = None
        for s, c in enumerate(cand[base:base + 8]):
            slab = c if slab is None else jnp.where(sub == s, c, slab)
        if len(cand) - base < 8:
            slab = jnp.where(sub < len(cand) - base, slab, -jnp.inf)
        packed.append(slab)
    theta = _sorted_top16(packed)[PEER_TOPK - 1]

    top = cand[0]
    z = jnp.zeros((8, tm), F32)
    for c in cand:
        z = z + jnp.where(c >= theta, jnp.exp(c - top), 0.0)
    inv_z = 1.0 / z

    r1, c1, rho, e2 = [], [], [], []
    for i in range(nslab):
        cnt = jnp.zeros((8, tm), F32)
        rk = jnp.zeros((8, tm), F32)
        for b in range(PEER_TOPK):
            cnt = cnt + jnp.where(s1[i] + v2[b] >= theta, 1.0, 0.0)
            rk = rk + jnp.where(v2[b] > s2[i], 1.0, 0.0)
        r1.append(cnt)
        rho.append(rk)
        c1.append(jnp.exp(s1[i] - v1[0]) * inv_z)
        e2.append(jnp.exp(s2[i] - v2[0]))
    r1_ref[0] = jnp.concatenate(r1, axis=0)
    c1_ref[0] = jnp.concatenate(c1, axis=0)
    rho_ref[0] = jnp.concatenate(rho, axis=0)
    e2_ref[0] = jnp.concatenate(e2, axis=0)


def _peer_route(xn, wq_t, k1, k2, tm):
    t = xn.shape[0]
    out = pl.BlockSpec((1, N_KEYS, tm), lambda i, h: (h, 0, i))
    shp = jax.ShapeDtypeStruct((PEER_HEADS, N_KEYS, t), F32)
    return pl.pallas_call(
        _peer_route_kernel,
        grid=(t // tm, PEER_HEADS),
        in_specs=[
            pl.BlockSpec((tm, D_MODEL), lambda i, h: (i, 0)),
            pl.BlockSpec((PEER_KEY_DIM, D_MODEL), lambda i, h: (h, 0)),
            pl.BlockSpec((1, N_KEYS, PEER_HALF), lambda i, h: (h, 0, 0)),
            pl.BlockSpec((1, N_KEYS, PEER_HALF), lambda i, h: (h, 0, 0)),
        ],
        out_specs=[out, out, out, out],
        out_shape=[shp, shp, shp, shp],
        compiler_params=_cparams("parallel", "parallel"),
        name="peer_route",
    )(xn, wq_t, k1, k2)


def _peer_dense_kernel(xn_ref, u_ref, vt_ref, rho_ref, e2_ref, r1_ref, c1_ref, o_ref, *, rows):
    e = pl.program_id(1)

    @pl.when(e == 0)
    def _():
        o_ref[...] = jnp.zeros(o_ref.shape, F32)

    nt = (((1,), (1,)), ((), ()))
    act = lax.dot_general(u_ref[...], xn_ref[...], nt, preferred_element_type=F32)
    act = jax.nn.gelu(act)
    gates = []
    for i in range(rows):
        w = None
        for h in range(PEER_HEADS):
            r = r1_ref[h, i:i + 1, :]
            c = c1_ref[h, i:i + 1, :]
            term = c * jnp.where(rho_ref[h] < r, e2_ref[h], 0.0)
            w = term if w is None else w + term
        gates.append(w)
    a = (act * jnp.concatenate(gates, axis=0)).astype(BF16)
    o_ref[...] += jnp.dot(vt_ref[...], a, preferred_element_type=F32)


def _peer_dense(xn, eu, evt, rho, e2, r1, c1, tt, te):
    t = xn.shape[0]
    ne = eu.shape[0]
    rows = te // N_KEYS
    tok = pl.BlockSpec((PEER_HEADS, N_KEYS, tt), lambda j, e: (0, 0, j))
    sel = pl.BlockSpec((PEER_HEADS, rows, tt), lambda j, e: (0, e, j))
    return pl.pallas_call(
        functools.partial(_peer_dense_kernel, rows=rows),
        grid=(t // tt, ne // te),
        in_specs=[
            pl.BlockSpec((tt, D_MODEL), lambda j, e: (j, 0)),
            pl.BlockSpec((te, D_MODEL), lambda j, e: (e, 0)),
            pl.BlockSpec((D_MODEL, te), lambda j, e: (0, e)),
            tok, tok, sel, sel,
        ],
        out_specs=pl.BlockSpec((D_MODEL, tt), lambda j, e: (0, j)),
        out_shape=jax.ShapeDtypeStruct((D_MODEL, t), F32),
        compiler_params=_cparams("parallel", "arbitrary"),
        name="peer_dense",
    )(xn, eu, evt, rho, e2, r1, c1)


def _ple_kernel(x_ref, pe_ref, p_ref, g_ref, wg_ref, wp_ref, gfin_ref, o_ref, *, final):
    x = x_ref[...] + pe_ref[...]
    gate = jax.nn.sigmoid(jnp.dot(_rms(x, g_ref[...]).astype(BF16), wg_ref[...], preferred_element_type=F32))
    x = x + gate * jnp.dot(p_ref[...].astype(BF16), wp_ref[...], preferred_element_type=F32)
    if final:
        x = _rms(x, gfin_ref[...])
    o_ref[...] = x


def _ple(x, pe, p, g, wg, wp, gfin, final, tm):
    t = x.shape[0]
    row = lambda w: pl.BlockSpec((tm, w), lambda i: (i, 0))
    full = lambda a, b: pl.BlockSpec((a, b), lambda i: (0, 0))
    return pl.pallas_call(
        functools.partial(_ple_kernel, final=final),
        grid=(t // tm,),
        in_specs=[row(D_MODEL), row(D_MODEL), row(D_PLE), full(1, D_MODEL), full(D_MODEL, D_MODEL),
                  full(D_PLE, D_MODEL), full(1, D_MODEL)],
        out_specs=row(D_MODEL),
        out_shape=jax.ShapeDtypeStruct((t, D_MODEL), F32),
        compiler_params=_cparams("parallel"),
        name="ple",
    )(x, pe, p, g, wg, wp, gfin)


def _rope_tables(seq):
    n_rows = seq // GRID_W
    row = jnp.repeat(jnp.arange(n_rows, dtype=jnp.int32), GRID_W).astype(F32)
    col = jnp.tile(jnp.arange(GRID_W, dtype=jnp.int32), n_rows).astype(F32)
    quarter = HEAD_DIM // 4
    inv = ROPE_THETA ** (-jnp.arange(quarter, dtype=F32) / quarter)
    ang_r = row[:, None] * inv[None, :]
    ang_c = col[:, None] * inv[None, :]
    cos = jnp.concatenate([jnp.cos(ang_r)] * 2 + [jnp.cos(ang_c)] * 2, axis=1)
    sin = jnp.concatenate([-jnp.sin(ang_r), jnp.sin(ang_r), -jnp.sin(ang_c), jnp.sin(ang_c)], axis=1)
    return jnp.tile(cos, (1, N_Q_HEADS)), jnp.tile(sin, (1, N_Q_HEADS))


def _forward(x, p, norm_mix_g, w_in, lam_re, lam_im, log_dt, b_re, b_im, c_re, c_im, d_skip, w_glu,
             b_glu, q_norm_g, k_norm_g, ssm_out_g, attn_out_g, w_out, norm_ffn_g, w_query, sub_keys1,
             sub_keys2, expert_u, expert_v, norm_ple_g, w_ple_gate, w_ple_proj, final_norm_g):
    n_batch, seq, _ = x.shape
    depth = w_in.shape[0]
    t = n_batch * seq
    tm = min(256, seq)
    tq = min(256, seq)
    tk = min(512, seq)
    tt = min(256, t)
    te = 1024
    nc = t // SSM_CHUNK

    cq, sq = _rope_tables(seq)
    bd = jnp.kron(jnp.eye(N_Q_HEADS, dtype=F32), jnp.ones((HEAD_DIM, HEAD_DIM), F32)).astype(BF16)
    row = lambda a: a.reshape(1, -1).astype(F32)

    xf = x.reshape(t, D_MODEL)
    for i in range(depth):
        u, q, k, v = _in_proj(xf, row(norm_mix_g[i]), w_in[i].astype(BF16), cq, sq,
                              row(jnp.tile(q_norm_g[i], N_Q_HEADS)), row(jnp.tile(k_norm_g[i], N_KV_HEADS)),
                              bd, seq, tm)

        grp = lambda a: jnp.transpose(a, (1, 0, 2))[:, :, None, :]
        ldt = jnp.broadcast_to(log_dt[i][:, :, None], (2, SSM_G, SSM_N))
        mt, et, fm, dec = _ssm_params(
            grp(lam_re[i]), grp(lam_im[i]), grp(ldt),
            jnp.transpose(b_re[i], (1, 0, 3, 2)), jnp.transpose(b_im[i], (1, 0, 3, 2)),
            jnp.transpose(c_re[i], (1, 0, 2, 3)), jnp.transpose(c_im[i], (1, 0, 2, 3)))
        ug = (u.astype(BF16).reshape(nc, SSM_CHUNK, SSM_G, SSM_P)
              .transpose(2, 0, 1, 3).reshape(SSM_G, nc, SSM_LP))
        yg = _ssm_conv(ug, mt, et, fm, dec, n_batch)
        y_ssm = yg.reshape(SSM_G, nc, SSM_CHUNK, SSM_P).transpose(1, 2, 0, 3).reshape(t, D_SSM)

        attn = _attention(q, k, v, n_batch, seq, tq, tk)

        xf, xn = _out_proj(xf, y_ssm, u, attn, row(d_skip[i]), w_glu[i].astype(BF16), row(b_glu[i]),
                           row(ssm_out_g[i]), row(attn_out_g[i]), w_out[i].astype(BF16),
                           row(norm_ffn_g[i]), tm)

        rho, e2, r1, c1 = _peer_route(xn, w_query[i].T.astype(BF16), sub_keys1[i].astype(BF16),
                                      sub_keys2[i].astype(BF16), tt)
        peer_t = _peer_dense(xn, expert_u[i].astype(BF16), expert_v[i].T.astype(BF16),
                             rho, e2, r1, c1, tt, te)

        xf = _ple(xf, peer_t.T, p[i].reshape(t, D_PLE), row(norm_ple_g[i]), w_ple_gate[i].astype(BF16),
                  w_ple_proj[i].astype(BF16), row(final_norm_g), i == depth - 1, tm)
    return xf.reshape(n_batch, seq, D_MODEL)


def kernel(x, p, norm_mix_g, w_in, lam_re, lam_im, log_dt, b_re, b_im, c_re, c_im, d_skip, w_glu, b_glu, q_norm_g, k_norm_g, ssm_out_g, attn_out_g, w_out, norm_ffn_g, w_query, sub_keys1, sub_keys2, expert_u, expert_v, norm_ple_g, w_ple_gate, w_ple_proj, final_norm_g):
    return _forward(x, p, norm_mix_g, w_in, lam_re, lam_im, log_dt, b_re, b_im, c_re, c_im, d_skip, w_glu,
                    b_glu, q_norm_g, k_norm_g, ssm_out_g, attn_out_g, w_out, norm_ffn_g, w_query,
                    sub_keys1, sub_keys2, expert_u, expert_v, norm_ple_g, w_ple_gate, w_ple_proj,
                    final_norm_g)
```

```python
import functools
import math

import jax
import jax.numpy as jnp
from jax import lax
from jax.experimental import pallas as pl
from jax.experimental.pallas import tpu as pltpu

F32 = jnp.float32
BF16 = jnp.bfloat16

D_MODEL = 1024
D_PLE = 256
GRID_W = 64
EPS = 1e-6
D_SSM = 512
SSM_P = 16
SSM_G = D_SSM // SSM_P
SSM_N = 64
D_ATTN = 512
HEAD_DIM = 64
N_Q_HEADS = 8
N_KV_HEADS = 2
Q_PER_KV = 4
D_KV = 128
D_IN = D_SSM + D_ATTN + 2 * D_KV
ROPE_THETA = 10000.0
PEER_HEADS = 8
PEER_KEY_DIM = 256
PEER_HALF = 128
N_KEYS = 128
PEER_TOPK = 16

PEER_BLOCK_ROWS = 2
SSM_CHUNK = 32
SSM_LP = SSM_CHUNK * SSM_P
VMEM_LIMIT = 48 * 1024 * 1024


def _cparams(*sem):
    return pltpu.CompilerParams(dimension_semantics=sem, vmem_limit_bytes=VMEM_LIMIT)


def _rms(x, g):
    return x * lax.rsqrt(jnp.mean(x * x, axis=-1, keepdims=True) + EPS) * g


def _in_proj_kernel(x_ref, g_ref, w_ref, cq_ref, sq_ref, gq_ref, gk_ref, bd_ref,
                    u_ref, q_ref, k_ref, v_ref):
    h = _rms(x_ref[...], g_ref[...])
    z = jnp.dot(h.astype(BF16), w_ref[...], preferred_element_type=F32)
    u_ref[...] = z[:, :D_SSM]

    def head_norm_rope(t, gain, bd, c, s):
        ss = jnp.dot((t * t).astype(BF16), bd, preferred_element_type=F32)
        t = t * lax.rsqrt(ss * (1.0 / HEAD_DIM) + EPS) * gain
        w = t.shape[1]
        lane = lax.broadcasted_iota(jnp.int32, t.shape, 1)
        partner = jnp.where((lane % 32) < 16, pltpu.roll(t, w - 16, 1), pltpu.roll(t, 16, 1))
        return t * c + partner * s

    cq = cq_ref[...]
    sq = sq_ref[...]
    q = head_norm_rope(z[:, D_SSM:D_SSM + D_ATTN], gq_ref[...], bd_ref[...], cq, sq)
    q_ref[...] = (q * (HEAD_DIM ** -0.5)).astype(BF16)
    k = head_norm_rope(z[:, D_SSM + D_ATTN:D_SSM + D_ATTN + D_KV], gk_ref[...],
                       bd_ref[:D_KV, :D_KV], cq[:, :D_KV], sq[:, :D_KV])
    v = z[:, D_SSM + D_ATTN + D_KV:]
    lane = lax.broadcasted_iota(jnp.int32, (v.shape[0], HEAD_DIM), 1)
    ones_col = jnp.where(lane == 0, 1.0, 0.0).astype(F32)
    for j in range(N_KV_HEADS):
        k_ref[j] = k[:, j * HEAD_DIM:(j + 1) * HEAD_DIM].astype(BF16)
        v_ref[j] = jnp.concatenate([v[:, j * HEAD_DIM:(j + 1) * HEAD_DIM], ones_col], axis=1).astype(BF16)


def _in_proj(x, g, w, cq, sq, gq, gk, bd, seq, tm):
    t = x.shape[0]
    nseq = seq // tm
    return pl.pallas_call(
        _in_proj_kernel,
        grid=(t // tm,),
        in_specs=[
            pl.BlockSpec((tm, D_MODEL), lambda i: (i, 0)),
            pl.BlockSpec((1, D_MODEL), lambda i: (0, 0)),
            pl.BlockSpec((D_MODEL, D_IN), lambda i: (0, 0)),
            pl.BlockSpec((tm, D_ATTN), lambda i: (i % nseq, 0)),
            pl.BlockSpec((tm, D_ATTN), lambda i: (i % nseq, 0)),
            pl.BlockSpec((1, D_ATTN), lambda i: (0, 0)),
            pl.BlockSpec((1, D_KV), lambda i: (0, 0)),
            pl.BlockSpec((D_ATTN, D_ATTN), lambda i: (0, 0)),
        ],
        out_specs=[
            pl.BlockSpec((tm, D_SSM), lambda i: (i, 0)),
            pl.BlockSpec((tm, D_ATTN), lambda i: (i, 0)),
            pl.BlockSpec((N_KV_HEADS, tm, HEAD_DIM), lambda i: (0, i, 0)),
            pl.BlockSpec((N_KV_HEADS, tm, 2 * HEAD_DIM), lambda i: (0, i, 0)),
        ],
        out_shape=[
            jax.ShapeDtypeStruct((t, D_SSM), F32),
            jax.ShapeDtypeStruct((t, D_ATTN), BF16),
            jax.ShapeDtypeStruct((N_KV_HEADS, t, HEAD_DIM), BF16),
            jax.ShapeDtypeStruct((N_KV_HEADS, t, 2 * HEAD_DIM), BF16),
        ],
        compiler_params=_cparams("parallel"),
        name="in_proj",
    )(x, g, w, cq, sq, gq, gk, bd)


def _ssm_params_kernel(lr_ref, li_ref, ldt_ref, bt_re_ref, bt_im_ref, c_re_ref, c_im_ref,
                       mt_ref, et_ref, fm_ref, dec_ref):
    L = SSM_CHUNK
    hi = lax.Precision.HIGHEST
    r_parts, e_parts, f_parts, p_parts, q_parts = [], [], [], [], []
    idx = lax.broadcasted_iota(jnp.int32, (L, 1, SSM_N), 0).astype(F32)
    for d in range(2):
        lr = lr_ref[0, d]
        li = li_ref[0, d]
        dt = jnp.exp(ldt_ref[0, d])

        def apow(tau):
            mag = jnp.exp(tau * (lr * dt))
            ang = tau * (li * dt)
            return mag * jnp.cos(ang), mag * jnp.sin(ang)

        ab_re, ab_im = apow(jnp.ones((1, SSM_N), F32))
        den = lr * lr + li * li
        nr = ab_re - 1.0
        f_re = (nr * lr + ab_im * li) / den
        f_im = (ab_im * lr - nr * li) / den
        bt_re = bt_re_ref[0, d]
        bt_im = bt_im_ref[0, d]
        bb_re = f_re * bt_re - f_im * bt_im
        bb_im = f_re * bt_im + f_im * bt_re
        c_re = c_re_ref[0, d]
        c_im = c_im_ref[0, d]

        tau_k = idx if d == 0 else (L - 1.0) - idx
        pr, pi = apow(tau_k)
        lm_re = (c_re[None] * pr - c_im[None] * pi).reshape(L * SSM_P, SSM_N)
        lm_im = (c_re[None] * pi + c_im[None] * pr).reshape(L * SSM_P, SSM_N)
        nt = (((1,), (1,)), ((), ()))
        r_parts.append(lax.dot_general(bb_re, lm_re, nt, precision=hi, preferred_element_type=F32)
                       - lax.dot_general(bb_im, lm_im, nt, precision=hi, preferred_element_type=F32))

        tau_e = (L - 1.0) - idx if d == 0 else idx
        pr, pi = apow(tau_e)
        e_parts.append((pr * bb_re[None] - pi * bb_im[None]).reshape(L * SSM_P, SSM_N))
        e_parts.append((pr * bb_im[None] + pi * bb_re[None]).reshape(L * SSM_P, SSM_N))

        tau_f = idx + 1.0 if d == 0 else L - idx
        pr, pi = apow(tau_f)
        f_parts.append((c_re[None] * pr - c_im[None] * pi).reshape(L * SSM_P, SSM_N))
        f_parts.append(-(c_re[None] * pi + c_im[None] * pr).reshape(L * SSM_P, SSM_N))

        al_re, al_im = apow(jnp.full((1, SSM_N), float(L), F32))
        p_parts += [al_re, al_re]
        q_parts += [-al_im, al_im]

    zeros = jnp.zeros((SSM_P, SSM_LP), F32)
    r_bwd = jnp.concatenate([r_parts[1], zeros], axis=1)
    r_fwd = pltpu.roll(jnp.concatenate([r_parts[0], zeros], axis=1), SSM_LP - SSM_P, 1)
    r_cat = r_bwd + r_fwd
    for s in range(L):
        shift = (2 * SSM_LP - (L - 1 - s) * SSM_P) % (2 * SSM_LP)
        win = r_cat if shift == 0 else pltpu.roll(r_cat, shift, 1)
        mt_ref[0, s * SSM_P:(s + 1) * SSM_P, :] = win[:, :SSM_LP].astype(BF16)
    et_ref[0] = jnp.concatenate(e_parts, axis=1).astype(BF16)
    fm_ref[0] = jnp.concatenate(f_parts, axis=1).astype(BF16)
    dec_ref[0] = jnp.concatenate([jnp.concatenate(p_parts, axis=1),
                                  jnp.concatenate(q_parts, axis=1)], axis=0)


def _ssm_params(lr, li, ldt, bt_re, bt_im, c_re, c_im):
    g = lr.shape[0]
    vec = pl.BlockSpec((1, 2, 1, SSM_N), lambda i: (i, 0, 0, 0))
    mat = pl.BlockSpec((1, 2, SSM_P, SSM_N), lambda i: (i, 0, 0, 0))
    return pl.pallas_call(
        _ssm_params_kernel,
        grid=(g,),
        in_specs=[vec, vec, vec, mat, mat, mat, mat],
        out_specs=[
            pl.BlockSpec((1, SSM_LP, SSM_LP), lambda i: (i, 0, 0)),
            pl.BlockSpec((1, SSM_LP, 4 * SSM_N), lambda i: (i, 0, 0)),
            pl.BlockSpec((1, SSM_LP, 4 * SSM_N), lambda i: (i, 0, 0)),
            pl.BlockSpec((1, 2, 4 * SSM_N), lambda i: (i, 0, 0)),
        ],
        out_shape=[
            jax.ShapeDtypeStruct((g, SSM_LP, SSM_LP), BF16),
            jax.ShapeDtypeStruct((g, SSM_LP, 4 * SSM_N), BF16),
            jax.ShapeDtypeStruct((g, SSM_LP, 4 * SSM_N), BF16),
            jax.ShapeDtypeStruct((g, 2, 4 * SSM_N), F32),
        ],
        compiler_params=_cparams("parallel"),
        name="ssm_params",
    )(lr, li, ldt, bt_re, bt_im, c_re, c_im)


def _ssm_conv_kernel(u_ref, mt_ref, et_ref, fm_ref, dec_ref, y_ref, *, n_batch):
    u = u_ref[0]
    nc = u.shape[0]
    ncb = nc // n_batch
    w = 2 * SSM_N
    s_all = jnp.dot(u, et_ref[0], preferred_element_type=F32)
    dec = dec_ref[0]
    row = lax.broadcasted_iota(jnp.int32, (ncb, w), 0)

    def cmul(x, p, q):
        return p * x + q * pltpu.roll(x, SSM_N, 1)

    h_rows = []
    for b in range(n_batch):
        s_b = s_all[b * ncb:(b + 1) * ncb]
        xf, xb = s_b[:, :w], s_b[:, w:]
        pf, qf = dec[0:1, :w], dec[1:2, :w]
        pb, qb = dec[0:1, w:], dec[1:2, w:]
        k = 1
        while k < ncb:
            xf = xf + jnp.where(row >= k, cmul(pltpu.roll(xf, k, 0), pf, qf), 0.0)
            xb = xb + jnp.where(row < ncb - k, cmul(pltpu.roll(xb, ncb - k, 0), pb, qb), 0.0)
            pf, qf = pf * pf - qf * qf, 2.0 * pf * qf
            pb, qb = pb * pb - qb * qb, 2.0 * pb * qb
            k *= 2
        hf = jnp.where(row >= 1, pltpu.roll(xf, 1, 0), 0.0)
        hb = jnp.where(row < ncb - 1, pltpu.roll(xb, ncb - 1, 0), 0.0)
        h_rows.append(jnp.concatenate([hf, hb], axis=1))
    h = jnp.concatenate(h_rows, axis=0).astype(BF16)
    y = jnp.dot(u, mt_ref[0], preferred_element_type=F32)
    y = y + lax.dot_general(h, fm_ref[0], (((1,), (1,)), ((), ())), preferred_element_type=F32)
    y_ref[0] = y


def _ssm_conv(ug, mt, et, fm, dec, n_batch):
    g, nc, lp = ug.shape
    return pl.pallas_call(
        functools.partial(_ssm_conv_kernel, n_batch=n_batch),
        grid=(g,),
        in_specs=[
            pl.BlockSpec((1, nc, lp), lambda i: (i, 0, 0)),
            pl.BlockSpec((1, lp, lp), lambda i: (i, 0, 0)),
            pl.BlockSpec((1, lp, 4 * SSM_N), lambda i: (i, 0, 0)),
            pl.BlockSpec((1, lp, 4 * SSM_N), lambda i: (i, 0, 0)),
            pl.BlockSpec((1, 2, 4 * SSM_N), lambda i: (i, 0, 0)),
        ],
        out_specs=pl.BlockSpec((1, nc, lp), lambda i: (i, 0, 0)),
        out_shape=jax.ShapeDtypeStruct((g, nc, lp), F32),
        compiler_params=_cparams("parallel"),
        name="ssm_conv",
    )(ug, mt, et, fm, dec)


def _attn_kernel(q_ref, k_ref, v_ref, o_ref, qs_ref, m_ref, acc_ref, *, tq, tk, nk):
    q = q_ref[...]
    for j in range(Q_PER_KV):
        qs_ref[j * tq:(j + 1) * tq, :] = q[:, j * HEAD_DIM:(j + 1) * HEAD_DIM]
    m_ref[...] = jnp.full(m_ref.shape, -jnp.inf, F32)
    acc_ref[...] = jnp.zeros(acc_ref.shape, F32)

    def body(i, carry):
        start = pl.multiple_of(i * tk, tk)
        kb = k_ref[0, pl.ds(start, tk), :]
        vb = v_ref[0, pl.ds(start, tk), :]
        s = lax.dot_general(qs_ref[...], kb, (((1,), (1,)), ((), ())), preferred_element_type=F32)
        m_prev = m_ref[...]
        m_new = jnp.maximum(m_prev, jnp.max(s, axis=1, keepdims=True))
        alpha = jnp.exp(m_prev - m_new)
        p = jnp.exp(s - jnp.concatenate([m_new] * (tk // 128), axis=1))
        acc_ref[...] = acc_ref[...] * alpha + jnp.dot(p.astype(BF16), vb, preferred_element_type=F32)
        m_ref[...] = m_new
        return carry

    lax.fori_loop(0, nk, body, 0)
    acc = acc_ref[...]
    out = acc[:, :HEAD_DIM] / acc[:, HEAD_DIM:HEAD_DIM + 1]
    o_ref[...] = jnp.concatenate([out[j * tq:(j + 1) * tq] for j in range(Q_PER_KV)], axis=1)


def _attention(q, k, v, n_batch, seq, tq, tk):
    t = q.shape[0]
    nq = seq // tq
    w = Q_PER_KV * HEAD_DIM
    return pl.pallas_call(
        functools.partial(_attn_kernel, tq=tq, tk=tk, nk=seq // tk),
        grid=(n_batch, N_KV_HEADS, nq),
        in_specs=[
            pl.BlockSpec((tq, w), lambda b, g, i: (b * nq + i, g)),
            pl.BlockSpec((1, seq, HEAD_DIM), lambda b, g, i: (g, b, 0)),
            pl.BlockSpec((1, seq, 2 * HEAD_DIM), lambda b, g, i: (g, b, 0)),
        ],
        out_specs=pl.BlockSpec((tq, w), lambda b, g, i: (b * nq + i, g)),
        out_shape=jax.ShapeDtypeStruct((t, D_ATTN), F32),
        scratch_shapes=[
            pltpu.VMEM((Q_PER_KV * tq, HEAD_DIM), BF16),
            pltpu.VMEM((Q_PER_KV * tq, 128), F32),
            pltpu.VMEM((Q_PER_KV * tq, 2 * HEAD_DIM), F32),
        ],
        compiler_params=_cparams("parallel", "parallel", "parallel"),
        name="attention",
    )(q, k, v)


def _out_proj_kernel(x_ref, ys_ref, u_ref, at_ref, dsk_ref, wglu_ref, bglu_ref, gs_ref, ga_ref,
                     wo_ref, gf_ref, xo_ref, xn_ref):
    y = ys_ref[...] + dsk_ref[...] * u_ref[...]
    y = jax.nn.gelu(y)
    z = jnp.dot(y.astype(BF16), wglu_ref[...], preferred_element_type=F32) + bglu_ref[...]
    y = y * jax.nn.sigmoid(z)
    ys = _rms(y, gs_ref[...]).astype(BF16)
    ya = _rms(at_ref[...], ga_ref[...]).astype(BF16)
    xo = (x_ref[...]
          + jnp.dot(ys, wo_ref[:D_SSM, :], preferred_element_type=F32)
          + jnp.dot(ya, wo_ref[D_SSM:, :], preferred_element_type=F32))
    xo_ref[...] = xo
    xn_ref[...] = _rms(xo, gf_ref[...]).astype(BF16)


def _out_proj(x, ys, u, at, dsk, wglu, bglu, gs, ga, wo, gf, tm):
    t = x.shape[0]
    row = lambda w: pl.BlockSpec((tm, w), lambda i: (i, 0))
    full = lambda a, b: pl.BlockSpec((a, b), lambda i: (0, 0))
    return pl.pallas_call(
        _out_proj_kernel,
        grid=(t // tm,),
        in_specs=[row(D_MODEL), row(D_SSM), row(D_SSM), row(D_ATTN), full(1, D_SSM),
                  full(D_SSM, D_SSM), full(1, D_SSM), full(1, D_SSM), full(1, D_ATTN),
                  full(D_MODEL, D_MODEL), full(1, D_MODEL)],
        out_specs=[row(D_MODEL), row(D_MODEL)],
        out_shape=[jax.ShapeDtypeStruct((t, D_MODEL), F32), jax.ShapeDtypeStruct((t, D_MODEL), BF16)],
        compiler_params=_cparams("parallel"),
        name="out_proj",
    )(x, ys, u, at, dsk, wglu, bglu, gs, ga, wo, gf)


def _oe_sort_network(n):
    pairs = []
    p = 1
    while p < n:
        k = p
        while k >= 1:
            for j in range(k % p, n - k, 2 * k):
                for i in range(min(k, n - j - k)):
                    if (i + j) // (2 * p) == (i + j + k) // (2 * p):
                        pairs.append((i + j, i + j + k))
            k //= 2
        p *= 2
    return pairs


def _ce(a, b):
    if b is None:
        return a, None
    if a is None:
        return b, None
    return jnp.maximum(a, b), jnp.minimum(a, b)


def _sorted_top16(slabs):
    n = PEER_TOPK
    v = list(slabs) + [None] * (n - len(slabs))
    for i, j in _oe_sort_network(n):
        v[i], v[j] = _ce(v[i], v[j])
    for shift in (4, 2, 1):
        partner = [None if x is None else pltpu.roll(x, shift, 0) for x in v]
        top = []
        for a in range(n):
            x, y = v[a], partner[n - 1 - a]
            top.append(y if x is None else (x if y is None else jnp.maximum(x, y)))
        v = top
        d = n // 2
        while d >= 1:
            for i in range(n):
                if i & d == 0:
                    v[i], v[i + d] = _ce(v[i], v[i + d])
            d //= 2
    return v


_CAND = [(a, b) for a in range(PEER_TOPK) for b in range(PEER_TOPK) if (a + 1) * (b + 1) <= PEER_TOPK]


def _bf16_pair_words(x):
    b = pltpu.bitcast(x, jnp.uint32)
    b = (b + jnp.uint32(0x7FFF) + ((b >> 16) & jnp.uint32(1))) >> 16
    return b | (b << 16)


def _peer_route_kernel(xn_ref, wq_ref, k1_ref, k2_ref, rho_ref, e2_ref, r1_ref, c1_ref):
    nt = (((1,), (1,)), ((), ()))
    qt = lax.dot_general(wq_ref[...], xn_ref[...], nt, preferred_element_type=F32)
    s1 = jnp.dot(k1_ref[0], qt[:PEER_HALF].astype(BF16), preferred_element_type=F32)
    s2 = jnp.dot(k2_ref[0], qt[PEER_HALF:].astype(BF16), preferred_element_type=F32)
    tm = s1.shape[1]
    nslab = N_KEYS // 8
    s1 = [s1[8 * i:8 * i + 8] for i in range(nslab)]
    s2 = [s2[8 * i:8 * i + 8] for i in range(nslab)]
    v1 = _sorted_top16(s1)
    v2 = _sorted_top16(s2)

    sub = lax.broadcasted_iota(jnp.int32, (8, tm), 0)
    cand = [v1[a] + v2[b] for a, b in _CAND]
    packed = []
    for base in range(0, len(cand), 8):
        slab = None
        for s, c in enumerate(cand[base:base + 8]):
            slab = c if slab is None else jnp.where(sub == s, c, slab)
        if len(cand) - base < 8:
            slab = jnp.where(sub < len(cand) - base, slab, -jnp.inf)
        packed.append(slab)
    theta = _sorted_top16(packed)[PEER_TOPK - 1]

    top = cand[0]
    z = jnp.zeros((8, tm), F32)
    for c in cand:
        z = z + jnp.where(c >= theta, jnp.exp(c - top), 0.0)
    inv_z = 1.0 / z

    r1, c1, rho, e2 = [], [], [], []
    for i in range(nslab):
        cnt = jnp.zeros((8, tm), F32)
        rk = jnp.zeros((8, tm), F32)
        for b in range(PEER_TOPK):
            cnt = cnt + jnp.where(s1[i] + v2[b] >= theta, 1.0, 0.0)
            rk = rk + jnp.where(v2[b] > s2[i], 1.0, 0.0)
        r1.append(cnt)
        rho.append(rk)
        c1.append(jnp.exp(s1[i] - v1[0]) * inv_z)
        e2.append(jnp.exp(s2[i] - v2[0]))
    r1 = _bf16_pair_words(jnp.concatenate(r1, axis=0))
    c1 = _bf16_pair_words(jnp.concatenate(c1, axis=0))
    rho = jnp.concatenate(rho, axis=0).astype(BF16)
    e2 = jnp.concatenate(e2, axis=0).astype(BF16)
    for l in range(tm // 128):
        lanes = slice(l * 128, (l + 1) * 128)
        r1_ref[0, l] = r1[:, lanes]
        c1_ref[0, l] = c1[:, lanes]
    for l in range(tm // 256):
        lanes = slice(l * 256, (l + 1) * 256)
        rho_ref[0, l * N_KEYS:(l + 1) * N_KEYS, :] = rho[:, lanes]
        e2_ref[0, l * N_KEYS:(l + 1) * N_KEYS, :] = e2[:, lanes]


def _peer_route(xn, wq_t, k1, k2, tm):
    t = xn.shape[0]
    sel = pl.BlockSpec((1, tm // 128, N_KEYS, 128), lambda i, h: (h, i, 0, 0))
    shp = jax.ShapeDtypeStruct((PEER_HEADS, t // 128, N_KEYS, 128), jnp.uint32)
    tok = pl.BlockSpec((1, tm // 2, 256), lambda i, h: (h, i, 0))
    shp16 = jax.ShapeDtypeStruct((PEER_HEADS, t // 2, 256), BF16)
    return pl.pallas_call(
        _peer_route_kernel,
        grid=(t // tm, PEER_HEADS),
        in_specs=[
            pl.BlockSpec((tm, D_MODEL), lambda i, h: (i, 0)),
            pl.BlockSpec((PEER_KEY_DIM, D_MODEL), lambda i, h: (h, 0)),
            pl.BlockSpec((1, N_KEYS, PEER_HALF), lambda i, h: (h, 0, 0)),
            pl.BlockSpec((1, N_KEYS, PEER_HALF), lambda i, h: (h, 0, 0)),
        ],
        out_specs=[tok, tok, sel, sel],
        out_shape=[shp16, shp16, shp, shp],
        compiler_params=_cparams("parallel", "parallel"),
        name="peer_route",
    )(xn, wq_t, k1, k2)


def _load_row_repeated(ref, h, l, i, n):
    return ref[h, l, pl.ds(i, n, stride=0), :]


def _peer_dense_kernel(xn_ref, u_ref, vt_ref, rho_ref, e2_ref, r1_ref, c1_ref, o_ref, a_ref, *, rows):
    e = pl.program_id(1)

    @pl.when(e == 0)
    def _():
        o_ref[...] = jnp.zeros(o_ref.shape, F32)

    tt = xn_ref.shape[1]
    sub = 16
    blk = PEER_BLOCK_ROWS * N_KEYS
    nblk = rows // PEER_BLOCK_ROWS
    strip = 256
    zero = jnp.zeros((sub, strip), BF16)

    def row_tile(ref, h, s, i):
        return jnp.concatenate(
            [pltpu.bitcast(_load_row_repeated(ref, h, s * (strip // 128) + l, i, sub // 2), BF16)
             for l in range(strip // 128)], axis=1)

    def first_matmul(k):
        return jnp.dot(u_ref[k * blk:(k + 1) * blk, :], xn_ref[...], preferred_element_type=F32)

    act = first_matmul(0)
    for k in range(nblk):
        act_next = first_matmul(k + 1) if k + 1 < nblk else None
        a_ref[k * blk:(k + 1) * blk, :] = jax.nn.gelu(act).astype(BF16)
        for s in range(tt // strip):
            lanes = slice(s * strip, (s + 1) * strip)
            for ii in range(PEER_BLOCK_ROWS):
                i = k * PEER_BLOCK_ROWS + ii
                rc = [(row_tile(r1_ref, h, s, i), row_tile(c1_ref, h, s, i)) for h in range(PEER_HEADS)]
                for jb in range(N_KEYS // sub):
                    keys = slice(s * N_KEYS + jb * sub, s * N_KEYS + (jb + 1) * sub)
                    w = None
                    for h in range(PEER_HEADS):
                        r, c = rc[h]
                        term = c * jnp.where(rho_ref[h, keys, :] < r, e2_ref[h, keys, :], zero)
                        w = term if w is None else w + term
                    lo = k * blk + ii * N_KEYS + jb * sub
                    a_ref[lo:lo + sub, lanes] = a_ref[lo:lo + sub, lanes] * w
        o_ref[...] += jnp.dot(vt_ref[:, k * blk:(k + 1) * blk], a_ref[k * blk:(k + 1) * blk, :],
                              preferred_element_type=F32)
        act = act_next


def _peer_dense(xn_t, eu, evt, rho, e2, r1, c1, tt, te):
    t = xn_t.shape[1]
    ne = eu.shape[0]
    rows = te // N_KEYS
    tok = pl.BlockSpec((PEER_HEADS, tt // 2, 256), lambda j, e: (0, j, 0))
    sel = pl.BlockSpec((PEER_HEADS, tt // 128, rows, 128), lambda j, e: (0, j, e, 0))
    return pl.pallas_call(
        functools.partial(_peer_dense_kernel, rows=rows),
        grid=(t // tt, ne // te),
        in_specs=[
            pl.BlockSpec((D_MODEL, tt), lambda j, e: (0, j)),
            pl.BlockSpec((te, D_MODEL), lambda j, e: (e, 0)),
            pl.BlockSpec((D_MODEL, te), lambda j, e: (0, e)),
            tok, tok, sel, sel,
        ],
        out_specs=pl.BlockSpec((D_MODEL, tt), lambda j, e: (0, j)),
        out_shape=jax.ShapeDtypeStruct((D_MODEL, t), F32),
        scratch_shapes=[pltpu.VMEM((te, tt), BF16)],
        compiler_params=_cparams("parallel", "arbitrary"),
        name="peer_dense",
    )(xn_t, eu, evt, rho, e2, r1, c1)


def _ple_kernel(x_ref, pe_ref, p_ref, g_ref, wg_ref, wp_ref, gfin_ref, o_ref, *, final):
    x = x_ref[...] + pe_ref[...]
    gate = jax.nn.sigmoid(jnp.dot(_rms(x, g_ref[...]).astype(BF16), wg_ref[...], preferred_element_type=F32))
    x = x + gate * jnp.dot(p_ref[...].astype(BF16), wp_ref[...], preferred_element_type=F32)
    if final:
        x = _rms(x, gfin_ref[...])
    o_ref[...] = x


def _ple(x, pe, p, g, wg, wp, gfin, final, tm):
    t = x.shape[0]
    row = lambda w: pl.BlockSpec((tm, w), lambda i: (i, 0))
    full = lambda a, b: pl.BlockSpec((a, b), lambda i: (0, 0))
    return pl.pallas_call(
        functools.partial(_ple_kernel, final=final),
        grid=(t // tm,),
        in_specs=[row(D_MODEL), row(D_MODEL), row(D_PLE), full(1, D_MODEL), full(D_MODEL, D_MODEL),
                  full(D_PLE, D_MODEL), full(1, D_MODEL)],
        out_specs=row(D_MODEL),
        out_shape=jax.ShapeDtypeStruct((t, D_MODEL), F32),
        compiler_params=_cparams("parallel"),
        name="ple",
    )(x, pe, p, g, wg, wp, gfin)


def _rope_tables(seq):
    n_rows = seq // GRID_W
    row = jnp.repeat(jnp.arange(n_rows, dtype=jnp.int32), GRID_W).astype(F32)
    col = jnp.tile(jnp.arange(GRID_W, dtype=jnp.int32), n_rows).astype(F32)
    quarter = HEAD_DIM // 4
    inv = ROPE_THETA ** (-jnp.arange(quarter, dtype=F32) / quarter)
    ang_r = row[:, None] * inv[None, :]
    ang_c = col[:, None] * inv[None, :]
    cos = jnp.concatenate([jnp.cos(ang_r)] * 2 + [jnp.cos(ang_c)] * 2, axis=1)
    sin = jnp.concatenate([-jnp.sin(ang_r), jnp.sin(ang_r), -jnp.sin(ang_c), jnp.sin(ang_c)], axis=1)
    return jnp.tile(cos, (1, N_Q_HEADS)), jnp.tile(sin, (1, N_Q_HEADS))


def _forward(x, p, norm_mix_g, w_in, lam_re, lam_im, log_dt, b_re, b_im, c_re, c_im, d_skip, w_glu,
             b_glu, q_norm_g, k_norm_g, ssm_out_g, attn_out_g, w_out, norm_ffn_g, w_query, sub_keys1,
             sub_keys2, expert_u, expert_v, norm_ple_g, w_ple_gate, w_ple_proj, final_norm_g):
    n_batch, seq, _ = x.shape
    depth = w_in.shape[0]
    t = n_batch * seq
    tm = min(256, seq)
    tq = min(256, seq)
    tk = min(512, seq)
    tt = min(512, t)
    te = 1024
    nc = t // SSM_CHUNK

    cq, sq = _rope_tables(seq)
    bd = jnp.kron(jnp.eye(N_Q_HEADS, dtype=F32), jnp.ones((HEAD_DIM, HEAD_DIM), F32)).astype(BF16)
    row = lambda a: a.reshape(1, -1).astype(F32)

    xf = x.reshape(t, D_MODEL)
    for i in range(depth):
        u, q, k, v = _in_proj(xf, row(norm_mix_g[i]), w_in[i].astype(BF16), cq, sq,
                              row(jnp.tile(q_norm_g[i], N_Q_HEADS)), row(jnp.tile(k_norm_g[i], N_KV_HEADS)),
                              bd, seq, tm)

        grp = lambda a: jnp.transpose(a, (1, 0, 2))[:, :, None, :]
        ldt = jnp.broadcast_to(log_dt[i][:, :, None], (2, SSM_G, SSM_N))
        mt, et, fm, dec = _ssm_params(
            grp(lam_re[i]), grp(lam_im[i]), grp(ldt),
            jnp.transpose(b_re[i], (1, 0, 3, 2)), jnp.transpose(b_im[i], (1, 0, 3, 2)),
            jnp.transpose(c_re[i], (1, 0, 2, 3)), jnp.transpose(c_im[i], (1, 0, 2, 3)))
        ug = (u.astype(BF16).reshape(nc, SSM_CHUNK, SSM_G, SSM_P)
              .transpose(2, 0, 1, 3).reshape(SSM_G, nc, SSM_LP))
        yg = _ssm_conv(ug, mt, et, fm, dec, n_batch)
        y_ssm = yg.reshape(SSM_G, nc, SSM_CHUNK, SSM_P).transpose(1, 2, 0, 3).reshape(t, D_SSM)

        attn = _attention(q, k, v, n_batch, seq, tq, tk)

        xf, xn = _out_proj(xf, y_ssm, u, attn, row(d_skip[i]), w_glu[i].astype(BF16), row(b_glu[i]),
                           row(ssm_out_g[i]), row(attn_out_g[i]), w_out[i].astype(BF16),
                           row(norm_ffn_g[i]), tm)

        rho, e2, r1, c1 = _peer_route(xn, w_query[i].T.astype(BF16), sub_keys1[i].astype(BF16),
                                      sub_keys2[i].astype(BF16), min(256, t))
        peer_t = _peer_dense(xn.T, expert_u[i].astype(BF16), expert_v[i].T.astype(BF16),
                             rho, e2, r1, c1, tt, te)

        xf = _ple(xf, peer_t.T, p[i].reshape(t, D_PLE), row(norm_ple_g[i]), w_ple_gate[i].astype(BF16),
                  w_ple_proj[i].astype(BF16), row(final_norm_g), i == depth - 1, tm)
    return xf.reshape(n_batch, seq, D_MODEL)


def kernel(x, p, norm_mix_g, w_in, lam_re, lam_im, log_dt, b_re, b_im, c_re, c_im, d_skip, w_glu, b_glu, q_norm_g, k_norm_g, ssm_out_g, attn_out_g, w_out, norm_ffn_g, w_query, sub_keys1, sub_keys2, expert_u, expert_v, norm_ple_g, w_ple_gate, w_ple_proj, final_norm_g):
    return _forward(x, p, norm_mix_g, w_in, lam_re, lam_im, log_dt, b_re, b_im, c_re, c_im, d_skip, w_glu,
                    b_glu, q_norm_g, k_norm_g, ssm_out_g, attn_out_g, w_out, norm_ffn_g, w_query,
                    sub_keys1, sub_keys2, expert_u, expert_v, norm_ple_g, w_ple_gate, w_ple_proj,
                    final_norm_g)
```

```python
import functools
import math

import jax
import jax.numpy as jnp
from jax import lax
from jax.experimental import pallas as pl
from jax.experimental.pallas import tpu as pltpu

F32 = jnp.float32
BF16 = jnp.bfloat16

D_MODEL = 1024
D_PLE = 256
GRID_W = 64
EPS = 1e-6
LOG2_E = 1.4426950408889634
D_SSM = 512
SSM_P = 16
SSM_G = D_SSM // SSM_P
SSM_N = 64
D_ATTN = 512
HEAD_DIM = 64
V_ROWS = HEAD_DIM + 16
N_Q_HEADS = 8
N_KV_HEADS = 2
Q_PER_KV = 4
D_KV = 128
D_IN = D_SSM + D_ATTN + 2 * D_KV
ROPE_THETA = 10000.0
PEER_HEADS = 8
PEER_KEY_DIM = 256
PEER_HALF = 128
N_KEYS = 128
PEER_TOPK = 16

PEER_BLOCK_ROWS = 2
SSM_CHUNK = 32
SSM_LP = SSM_CHUNK * SSM_P
VMEM_LIMIT = 48 * 1024 * 1024


def _cparams(*sem):
    return pltpu.CompilerParams(dimension_semantics=sem, vmem_limit_bytes=VMEM_LIMIT)


def _rms(x, g):
    return x * lax.rsqrt(jnp.mean(x * x, axis=-1, keepdims=True) + EPS) * g


def _in_proj_kernel(x_ref, g_ref, w_ref, cq_ref, sq_ref, gq_ref, gk_ref, bd_ref,
                    u_ref, qt_ref, k_ref, vt_ref):
    h = _rms(x_ref[...], g_ref[...])
    z = jnp.dot(h.astype(BF16), w_ref[...], preferred_element_type=F32)
    u_ref[...] = z[:, :D_SSM]

    def head_norm_rope(t, gain, bd, c, s):
        ss = jnp.dot((t * t).astype(BF16), bd, preferred_element_type=F32)
        t = t * lax.rsqrt(ss * (1.0 / HEAD_DIM) + EPS) * gain
        w = t.shape[1]
        lane = lax.broadcasted_iota(jnp.int32, t.shape, 1)
        partner = jnp.where((lane % 32) < 16, pltpu.roll(t, w - 16, 1), pltpu.roll(t, 16, 1))
        return t * c + partner * s

    cq = cq_ref[...]
    sq = sq_ref[...]
    q = head_norm_rope(z[:, D_SSM:D_SSM + D_ATTN], gq_ref[...], bd_ref[...], cq, sq)
    qt_ref[...] = (q * (HEAD_DIM ** -0.5 * LOG2_E)).T.astype(BF16)
    k = head_norm_rope(z[:, D_SSM + D_ATTN:D_SSM + D_ATTN + D_KV], gk_ref[...],
                       bd_ref[:D_KV, :D_KV], cq[:, :D_KV], sq[:, :D_KV])
    vt = z[:, D_SSM + D_ATTN + D_KV:].T
    row = lax.broadcasted_iota(jnp.int32, (V_ROWS - HEAD_DIM, vt.shape[1]), 0)
    ones_row = jnp.where(row == 0, 1.0, 0.0).astype(F32)
    for j in range(N_KV_HEADS):
        k_ref[j] = k[:, j * HEAD_DIM:(j + 1) * HEAD_DIM].astype(BF16)
        vt_ref[j, 0] = jnp.concatenate([vt[j * HEAD_DIM:(j + 1) * HEAD_DIM], ones_row], axis=0).astype(BF16)


def _in_proj(x, g, w, cq, sq, gq, gk, bd, seq, tm):
    t = x.shape[0]
    nseq = seq // tm
    return pl.pallas_call(
        _in_proj_kernel,
        grid=(t // tm,),
        in_specs=[
            pl.BlockSpec((tm, D_MODEL), lambda i: (i, 0)),
            pl.BlockSpec((1, D_MODEL), lambda i: (0, 0)),
            pl.BlockSpec((D_MODEL, D_IN), lambda i: (0, 0)),
            pl.BlockSpec((tm, D_ATTN), lambda i: (i % nseq, 0)),
            pl.BlockSpec((tm, D_ATTN), lambda i: (i % nseq, 0)),
            pl.BlockSpec((1, D_ATTN), lambda i: (0, 0)),
            pl.BlockSpec((1, D_KV), lambda i: (0, 0)),
            pl.BlockSpec((D_ATTN, D_ATTN), lambda i: (0, 0)),
        ],
        out_specs=[
            pl.BlockSpec((tm, D_SSM), lambda i: (i, 0)),
            pl.BlockSpec((D_ATTN, tm), lambda i: (0, i)),
            pl.BlockSpec((N_KV_HEADS, tm, HEAD_DIM), lambda i: (0, i, 0)),
            pl.BlockSpec((N_KV_HEADS, 1, V_ROWS, tm), lambda i: (0, i, 0, 0)),
        ],
        out_shape=[
            jax.ShapeDtypeStruct((t, D_SSM), F32),
            jax.ShapeDtypeStruct((D_ATTN, t), BF16),
            jax.ShapeDtypeStruct((N_KV_HEADS, t, HEAD_DIM), BF16),
            jax.ShapeDtypeStruct((N_KV_HEADS, t // tm, V_ROWS, tm), BF16),
        ],
        compiler_params=_cparams("parallel"),
        name="in_proj",
    )(x, g, w, cq, sq, gq, gk, bd)


def _ssm_params_kernel(lr_ref, li_ref, ldt_ref, bt_re_ref, bt_im_ref, c_re_ref, c_im_ref,
                       mt_ref, et_ref, fm_ref, dec_ref):
    L = SSM_CHUNK
    hi = lax.Precision.HIGHEST
    r_parts, e_parts, f_parts, p_parts, q_parts = [], [], [], [], []
    idx = lax.broadcasted_iota(jnp.int32, (L, 1, SSM_N), 0).astype(F32)
    for d in range(2):
        lr = lr_ref[0, d]
        li = li_ref[0, d]
        dt = jnp.exp(ldt_ref[0, d])

        def apow(tau):
            mag = jnp.exp(tau * (lr * dt))
            ang = tau * (li * dt)
            return mag * jnp.cos(ang), mag * jnp.sin(ang)

        ab_re, ab_im = apow(jnp.ones((1, SSM_N), F32))
        den = lr * lr + li * li
        nr = ab_re - 1.0
        f_re = (nr * lr + ab_im * li) / den
        f_im = (ab_im * lr - nr * li) / den
        bt_re = bt_re_ref[0, d]
        bt_im = bt_im_ref[0, d]
        bb_re = f_re * bt_re - f_im * bt_im
        bb_im = f_re * bt_im + f_im * bt_re
        c_re = c_re_ref[0, d]
        c_im = c_im_ref[0, d]

        tau_k = idx if d == 0 else (L - 1.0) - idx
        pr, pi = apow(tau_k)
        lm_re = (c_re[None] * pr - c_im[None] * pi).reshape(L * SSM_P, SSM_N)
        lm_im = (c_re[None] * pi + c_im[None] * pr).reshape(L * SSM_P, SSM_N)
        nt = (((1,), (1,)), ((), ()))
        r_parts.append(lax.dot_general(bb_re, lm_re, nt, precision=hi, preferred_element_type=F32)
                       - lax.dot_general(bb_im, lm_im, nt, precision=hi, preferred_element_type=F32))

        tau_e = (L - 1.0) - idx if d == 0 else idx
        pr, pi = apow(tau_e)
        e_parts.append((pr * bb_re[None] - pi * bb_im[None]).reshape(L * SSM_P, SSM_N))
        e_parts.append((pr * bb_im[None] + pi * bb_re[None]).reshape(L * SSM_P, SSM_N))

        tau_f = idx + 1.0 if d == 0 else L - idx
        pr, pi = apow(tau_f)
        f_parts.append((c_re[None] * pr - c_im[None] * pi).reshape(L * SSM_P, SSM_N))
        f_parts.append(-(c_re[None] * pi + c_im[None] * pr).reshape(L * SSM_P, SSM_N))

        al_re, al_im = apow(jnp.full((1, SSM_N), float(L), F32))
        p_parts += [al_re, al_re]
        q_parts += [-al_im, al_im]

    zeros = jnp.zeros((SSM_P, SSM_LP), F32)
    r_bwd = jnp.concatenate([r_parts[1], zeros], axis=1)
    r_fwd = pltpu.roll(jnp.concatenate([r_parts[0], zeros], axis=1), SSM_LP - SSM_P, 1)
    r_cat = r_bwd + r_fwd
    for s in range(L):
        shift = (2 * SSM_LP - (L - 1 - s) * SSM_P) % (2 * SSM_LP)
        win = r_cat if shift == 0 else pltpu.roll(r_cat, shift, 1)
        mt_ref[0, s * SSM_P:(s + 1) * SSM_P, :] = win[:, :SSM_LP].astype(BF16)
    et_ref[0] = jnp.concatenate(e_parts, axis=1).astype(BF16)
    fm_ref[0] = jnp.concatenate(f_parts, axis=1).astype(BF16)
    dec_ref[0] = jnp.concatenate([jnp.concatenate(p_parts, axis=1),
                                  jnp.concatenate(q_parts, axis=1)], axis=0)


def _ssm_params(lr, li, ldt, bt_re, bt_im, c_re, c_im):
    g = lr.shape[0]
    vec = pl.BlockSpec((1, 2, 1, SSM_N), lambda i: (i, 0, 0, 0))
    mat = pl.BlockSpec((1, 2, SSM_P, SSM_N), lambda i: (i, 0, 0, 0))
    return pl.pallas_call(
        _ssm_params_kernel,
        grid=(g,),
        in_specs=[vec, vec, vec, mat, mat, mat, mat],
        out_specs=[
            pl.BlockSpec((1, SSM_LP, SSM_LP), lambda i: (i, 0, 0)),
            pl.BlockSpec((1, SSM_LP, 4 * SSM_N), lambda i: (i, 0, 0)),
            pl.BlockSpec((1, SSM_LP, 4 * SSM_N), lambda i: (i, 0, 0)),
            pl.BlockSpec((1, 2, 4 * SSM_N), lambda i: (i, 0, 0)),
        ],
        out_shape=[
            jax.ShapeDtypeStruct((g, SSM_LP, SSM_LP), BF16),
            jax.ShapeDtypeStruct((g, SSM_LP, 4 * SSM_N), BF16),
            jax.ShapeDtypeStruct((g, SSM_LP, 4 * SSM_N), BF16),
            jax.ShapeDtypeStruct((g, 2, 4 * SSM_N), F32),
        ],
        compiler_params=_cparams("parallel"),
        name="ssm_params",
    )(lr, li, ldt, bt_re, bt_im, c_re, c_im)


def _ssm_conv_kernel(u_ref, mt_ref, et_ref, fm_ref, dec_ref, y_ref, *, n_batch):
    u = u_ref[0]
    nc = u.shape[0]
    ncb = nc // n_batch
    w = 2 * SSM_N
    s_all = jnp.dot(u, et_ref[0], preferred_element_type=F32)
    dec = dec_ref[0]
    row = lax.broadcasted_iota(jnp.int32, (ncb, w), 0)

    def cmul(x, p, q):
        return p * x + q * pltpu.roll(x, SSM_N, 1)

    h_rows = []
    for b in range(n_batch):
        s_b = s_all[b * ncb:(b + 1) * ncb]
        xf, xb = s_b[:, :w], s_b[:, w:]
        pf, qf = dec[0:1, :w], dec[1:2, :w]
        pb, qb = dec[0:1, w:], dec[1:2, w:]
        k = 1
        while k < ncb:
            xf = xf + jnp.where(row >= k, cmul(pltpu.roll(xf, k, 0), pf, qf), 0.0)
            xb = xb + jnp.where(row < ncb - k, cmul(pltpu.roll(xb, ncb - k, 0), pb, qb), 0.0)
            pf, qf = pf * pf - qf * qf, 2.0 * pf * qf
            pb, qb = pb * pb - qb * qb, 2.0 * pb * qb
            k *= 2
        hf = jnp.where(row >= 1, pltpu.roll(xf, 1, 0), 0.0)
        hb = jnp.where(row < ncb - 1, pltpu.roll(xb, ncb - 1, 0), 0.0)
        h_rows.append(jnp.concatenate([hf, hb], axis=1))
    h = jnp.concatenate(h_rows, axis=0).astype(BF16)
    y = jnp.dot(u, mt_ref[0], preferred_element_type=F32)
    y = y + lax.dot_general(h, fm_ref[0], (((1,), (1,)), ((), ())), preferred_element_type=F32)
    y_ref[0] = y


def _ssm_conv(ug, mt, et, fm, dec, n_batch):
    g, nc, lp = ug.shape
    return pl.pallas_call(
        functools.partial(_ssm_conv_kernel, n_batch=n_batch),
        grid=(g,),
        in_specs=[
            pl.BlockSpec((1, nc, lp), lambda i: (i, 0, 0)),
            pl.BlockSpec((1, lp, lp), lambda i: (i, 0, 0)),
            pl.BlockSpec((1, lp, 4 * SSM_N), lambda i: (i, 0, 0)),
            pl.BlockSpec((1, lp, 4 * SSM_N), lambda i: (i, 0, 0)),
            pl.BlockSpec((1, 2, 4 * SSM_N), lambda i: (i, 0, 0)),
        ],
        out_specs=pl.BlockSpec((1, nc, lp), lambda i: (i, 0, 0)),
        out_shape=jax.ShapeDtypeStruct((g, nc, lp), F32),
        compiler_params=_cparams("parallel"),
        name="ssm_conv",
    )(ug, mt, et, fm, dec)


def _attn_kernel(qt_ref, k_ref, vt_ref, o_ref, q_sc, m_ref, acc_ref, sa_ref, sb_ref, *, tq, tk, nk):
    qt = qt_ref[...]
    for j in range(Q_PER_KV):
        q_sc[:, j * tq:(j + 1) * tq] = qt[j * HEAD_DIM:(j + 1) * HEAD_DIM, :]
    m_ref[...] = jnp.full(m_ref.shape, -jnp.inf, F32)
    acc_ref[...] = jnp.zeros(acc_ref.shape, F32)

    def scores(i):
        kb = k_ref[0, pl.ds(pl.multiple_of(i * tk, tk), tk), :]
        return jnp.dot(kb, q_sc[...], preferred_element_type=F32)

    def accumulate(s_ref, i):
        s = s_ref[...]
        m_prev = m_ref[...]
        m_new = jnp.maximum(m_prev, jnp.max(s, axis=0, keepdims=True))
        alpha = jnp.exp2(m_prev - m_new)
        p = jnp.exp2(s - m_new[0:1, :])
        acc_ref[:V_ROWS, :] = (acc_ref[:V_ROWS, :] * alpha[0:1, :]
                               + jnp.dot(vt_ref[0, i], p.astype(BF16), preferred_element_type=F32))
        m_ref[...] = m_new

    sa_ref[...] = scores(0)

    def body(j, carry):
        sb_ref[...] = scores(2 * j + 1)
        accumulate(sa_ref, 2 * j)
        sa_ref[...] = scores(2 * j + 2)
        accumulate(sb_ref, 2 * j + 1)
        return carry

    lax.fori_loop(0, nk // 2 - 1, body, 0)
    sb_ref[...] = scores(nk - 1)
    accumulate(sa_ref, nk - 2)
    accumulate(sb_ref, nk - 1)
    acc = acc_ref[...]
    out = (acc / acc[HEAD_DIM:HEAD_DIM + 1, :]).T
    o_ref[...] = jnp.concatenate([out[j * tq:(j + 1) * tq, :HEAD_DIM] for j in range(Q_PER_KV)], axis=1)


def _attention(qt, k, vt, n_batch, seq, tq):
    t = qt.shape[1]
    tk = vt.shape[3]
    nq = seq // tq
    nk = seq // tk
    assert nk % 2 == 0, "the key loop handles chunks in pairs"
    w = Q_PER_KV * HEAD_DIM
    return pl.pallas_call(
        functools.partial(_attn_kernel, tq=tq, tk=tk, nk=nk),
        grid=(n_batch, N_KV_HEADS, nq),
        in_specs=[
            pl.BlockSpec((w, tq), lambda b, g, i: (g, b * nq + i)),
            pl.BlockSpec((1, seq, HEAD_DIM), lambda b, g, i: (g, b, 0)),
            pl.BlockSpec((1, nk, V_ROWS, tk), lambda b, g, i: (g, b, 0, 0)),
        ],
        out_specs=pl.BlockSpec((tq, w), lambda b, g, i: (b * nq + i, g)),
        out_shape=jax.ShapeDtypeStruct((t, D_ATTN), F32),
        scratch_shapes=[
            pltpu.VMEM((HEAD_DIM, Q_PER_KV * tq), BF16),
            pltpu.VMEM((8, Q_PER_KV * tq), F32),
            pltpu.VMEM((2 * HEAD_DIM, Q_PER_KV * tq), F32),
            pltpu.VMEM((tk, Q_PER_KV * tq), F32),
            pltpu.VMEM((tk, Q_PER_KV * tq), F32),
        ],
        compiler_params=_cparams("parallel", "parallel", "parallel"),
        name="attention",
    )(qt, k, vt)


def _out_proj_kernel(x_ref, ys_ref, u_ref, at_ref, dsk_ref, wglu_ref, bglu_ref, gs_ref, ga_ref,
                     wo_ref, gf_ref, xo_ref, xn_ref):
    y = ys_ref[...] + dsk_ref[...] * u_ref[...]
    y = jax.nn.gelu(y)
    z = jnp.dot(y.astype(BF16), wglu_ref[...], preferred_element_type=F32) + bglu_ref[...]
    y = y * jax.nn.sigmoid(z)
    ys = _rms(y, gs_ref[...]).astype(BF16)
    ya = _rms(at_ref[...], ga_ref[...]).astype(BF16)
    xo = (x_ref[...]
          + jnp.dot(ys, wo_ref[:D_SSM, :], preferred_element_type=F32)
          + jnp.dot(ya, wo_ref[D_SSM:, :], preferred_element_type=F32))
    xo_ref[...] = xo
    xn_ref[...] = _rms(xo, gf_ref[...]).astype(BF16)


def _out_proj(x, ys, u, at, dsk, wglu, bglu, gs, ga, wo, gf, tm):
    t = x.shape[0]
    row = lambda w: pl.BlockSpec((tm, w), lambda i: (i, 0))
    full = lambda a, b: pl.BlockSpec((a, b), lambda i: (0, 0))
    return pl.pallas_call(
        _out_proj_kernel,
        grid=(t // tm,),
        in_specs=[row(D_MODEL), row(D_SSM), row(D_SSM), row(D_ATTN), full(1, D_SSM),
                  full(D_SSM, D_SSM), full(1, D_SSM), full(1, D_SSM), full(1, D_ATTN),
                  full(D_MODEL, D_MODEL), full(1, D_MODEL)],
        out_specs=[row(D_MODEL), row(D_MODEL)],
        out_shape=[jax.ShapeDtypeStruct((t, D_MODEL), F32), jax.ShapeDtypeStruct((t, D_MODEL), BF16)],
        compiler_params=_cparams("parallel"),
        name="out_proj",
    )(x, ys, u, at, dsk, wglu, bglu, gs, ga, wo, gf)


def _oe_sort_network(n):
    pairs = []
    p = 1
    while p < n:
        k = p
        while k >= 1:
            for j in range(k % p, n - k, 2 * k):
                for i in range(min(k, n - j - k)):
                    if (i + j) // (2 * p) == (i + j + k) // (2 * p):
                        pairs.append((i + j, i + j + k))
            k //= 2
        p *= 2
    return pairs


def _ce(a, b):
    if b is None:
        return a, None
    if a is None:
        return b, None
    return jnp.maximum(a, b), jnp.minimum(a, b)


def _sorted_top16(slabs):
    n = PEER_TOPK
    v = list(slabs) + [None] * (n - len(slabs))
    for i, j in _oe_sort_network(n):
        v[i], v[j] = _ce(v[i], v[j])
    for shift in (4, 2, 1):
        partner = [None if x is None else pltpu.roll(x, shift, 0) for x in v]
        top = []
        for a in range(n):
            x, y = v[a], partner[n - 1 - a]
            top.append(y if x is None else (x if y is None else jnp.maximum(x, y)))
        v = top
        d = n // 2
        while d >= 1:
            for i in range(n):
                if i & d == 0:
                    v[i], v[i + d] = _ce(v[i], v[i + d])
            d //= 2
    return v


_CAND = [(a, b) for a in range(PEER_TOPK) for b in range(PEER_TOPK) if (a + 1) * (b + 1) <= PEER_TOPK]


def _bf16_pair_words(x):
    b = pltpu.bitcast(x, jnp.uint32)
    b = (b + jnp.uint32(0x7FFF) + ((b >> 16) & jnp.uint32(1))) >> 16
    return b | (b << 16)


def _peer_route_kernel(xn_ref, wq_ref, k1_ref, k2_ref, rho_ref, e2_ref, r1_ref, c1_ref):
    nt = (((1,), (1,)), ((), ()))
    qt = lax.dot_general(wq_ref[...], xn_ref[...], nt, preferred_element_type=F32)
    s1 = jnp.dot(k1_ref[0], qt[:PEER_HALF].astype(BF16), preferred_element_type=F32)
    s2 = jnp.dot(k2_ref[0], qt[PEER_HALF:].astype(BF16), preferred_element_type=F32)
    tm = s1.shape[1]
    nslab = N_KEYS // 8
    s1 = [s1[8 * i:8 * i + 8] for i in range(nslab)]
    s2 = [s2[8 * i:8 * i + 8] for i in range(nslab)]
    v1 = _sorted_top16(s1)
    v2 = _sorted_top16(s2)

    sub = lax.broadcasted_iota(jnp.int32, (8, tm), 0)
    cand = [v1[a] + v2[b] for a, b in _CAND]
    packed = []
    for base in range(0, len(cand), 8):
        slab = None
        for s, c in enumerate(cand[base:base + 8]):
            slab = c if slab is None else jnp.where(sub == s, c, slab)
        if len(cand) - base < 8:
            slab = jnp.where(sub < len(cand) - base, slab, -jnp.inf)
        packed.append(slab)
    theta = _sorted_top16(packed)[PEER_TOPK - 1]

    top = cand[0]
    z = jnp.zeros((8, tm), F32)
    for c in cand:
        z = z + jnp.where(c >= theta, jnp.exp(c - top), 0.0)
    inv_z = 1.0 / z

    r1, c1, rho, e2 = [], [], [], []
    for i in range(nslab):
        cnt = jnp.zeros((8, tm), F32)
        rk = jnp.zeros((8, tm), F32)
        for b in range(PEER_TOPK):
            cnt = cnt + jnp.where(s1[i] + v2[b] >= theta, 1.0, 0.0)
            rk = rk + jnp.where(v2[b] > s2[i], 1.0, 0.0)
        r1.append(cnt)
        rho.append(rk)
        c1.append(jnp.exp(s1[i] - v1[0]) * inv_z)
        e2.append(jnp.exp(s2[i] - v2[0]))
    r1 = _bf16_pair_words(jnp.concatenate(r1, axis=0))
    c1 = _bf16_pair_words(jnp.concatenate(c1, axis=0))
    rho = jnp.concatenate(rho, axis=0).astype(BF16)
    e2 = jnp.concatenate(e2, axis=0).astype(BF16)
    for l in range(tm // 128):
        lanes = slice(l * 128, (l + 1) * 128)
        r1_ref[0, l] = r1[:, lanes]
        c1_ref[0, l] = c1[:, lanes]
    for l in range(tm // 256):
        lanes = slice(l * 256, (l + 1) * 256)
        rho_ref[0, l * N_KEYS:(l + 1) * N_KEYS, :] = rho[:, lanes]
        e2_ref[0, l * N_KEYS:(l + 1) * N_KEYS, :] = e2[:, lanes]


def _peer_route(xn, wq_t, k1, k2, tm):
    t = xn.shape[0]
    sel = pl.BlockSpec((1, tm // 128, N_KEYS, 128), lambda i, h: (h, i, 0, 0))
    shp = jax.ShapeDtypeStruct((PEER_HEADS, t // 128, N_KEYS, 128), jnp.uint32)
    tok = pl.BlockSpec((1, tm // 2, 256), lambda i, h: (h, i, 0))
    shp16 = jax.ShapeDtypeStruct((PEER_HEADS, t // 2, 256), BF16)
    return pl.pallas_call(
        _peer_route_kernel,
        grid=(t // tm, PEER_HEADS),
        in_specs=[
            pl.BlockSpec((tm, D_MODEL), lambda i, h: (i, 0)),
            pl.BlockSpec((PEER_KEY_DIM, D_MODEL), lambda i, h: (h, 0)),
            pl.BlockSpec((1, N_KEYS, PEER_HALF), lambda i, h: (h, 0, 0)),
            pl.BlockSpec((1, N_KEYS, PEER_HALF), lambda i, h: (h, 0, 0)),
        ],
        out_specs=[tok, tok, sel, sel],
        out_shape=[shp16, shp16, shp, shp],
        compiler_params=_cparams("parallel", "parallel"),
        name="peer_route",
    )(xn, wq_t, k1, k2)


def _load_row_repeated(ref, h, l, i, n):
    return ref[h, l, pl.ds(i, n, stride=0), :]


def _peer_dense_kernel(xn_ref, u_ref, vt_ref, rho_ref, e2_ref, r1_ref, c1_ref, o_ref, a_ref, *, rows):
    e = pl.program_id(1)

    @pl.when(e == 0)
    def _():
        o_ref[...] = jnp.zeros(o_ref.shape, F32)

    tt = xn_ref.shape[1]
    sub = 16
    blk = PEER_BLOCK_ROWS * N_KEYS
    nblk = rows // PEER_BLOCK_ROWS
    strip = 256
    zero = jnp.zeros((sub, strip), BF16)

    def row_tile(ref, h, s, i):
        return jnp.concatenate(
            [pltpu.bitcast(_load_row_repeated(ref, h, s * (strip // 128) + l, i, sub // 2), BF16)
             for l in range(strip // 128)], axis=1)

    def first_matmul(k):
        return jnp.dot(u_ref[k * blk:(k + 1) * blk, :], xn_ref[...], preferred_element_type=F32)

    act = first_matmul(0)
    for k in range(nblk):
        act_next = first_matmul(k + 1) if k + 1 < nblk else None
        a_ref[k * blk:(k + 1) * blk, :] = jax.nn.gelu(act).astype(BF16)
        for s in range(tt // strip):
            lanes = slice(s * strip, (s + 1) * strip)
            for ii in range(PEER_BLOCK_ROWS):
                i = k * PEER_BLOCK_ROWS + ii
                rc = [(row_tile(r1_ref, h, s, i), row_tile(c1_ref, h, s, i)) for h in range(PEER_HEADS)]
                for jb in range(N_KEYS // sub):
                    keys = slice(s * N_KEYS + jb * sub, s * N_KEYS + (jb + 1) * sub)
                    w = None
                    for h in range(PEER_HEADS):
                        r, c = rc[h]
                        term = c * jnp.where(rho_ref[h, keys, :] < r, e2_ref[h, keys, :], zero)
                        w = term if w is None else w + term
                    lo = k * blk + ii * N_KEYS + jb * sub
                    a_ref[lo:lo + sub, lanes] = a_ref[lo:lo + sub, lanes] * w
        o_ref[...] += jnp.dot(vt_ref[:, k * blk:(k + 1) * blk], a_ref[k * blk:(k + 1) * blk, :],
                              preferred_element_type=F32)
        act = act_next


def _peer_dense(xn_t, eu, evt, rho, e2, r1, c1, tt, te):
    t = xn_t.shape[1]
    ne = eu.shape[0]
    rows = te // N_KEYS
    tok = pl.BlockSpec((PEER_HEADS, tt // 2, 256), lambda j, e: (0, j, 0))
    sel = pl.BlockSpec((PEER_HEADS, tt // 128, rows, 128), lambda j, e: (0, j, e, 0))
    return pl.pallas_call(
        functools.partial(_peer_dense_kernel, rows=rows),
        grid=(t // tt, ne // te),
        in_specs=[
            pl.BlockSpec((D_MODEL, tt), lambda j, e: (0, j)),
            pl.BlockSpec((te, D_MODEL), lambda j, e: (e, 0)),
            pl.BlockSpec((D_MODEL, te), lambda j, e: (0, e)),
            tok, tok, sel, sel,
        ],
        out_specs=pl.BlockSpec((D_MODEL, tt), lambda j, e: (0, j)),
        out_shape=jax.ShapeDtypeStruct((D_MODEL, t), F32),
        scratch_shapes=[pltpu.VMEM((te, tt), BF16)],
        compiler_params=_cparams("parallel", "arbitrary"),
        name="peer_dense",
    )(xn_t, eu, evt, rho, e2, r1, c1)


def _ple_kernel(x_ref, pe_ref, p_ref, g_ref, wg_ref, wp_ref, gfin_ref, o_ref, *, final):
    x = x_ref[...] + pe_ref[...]
    gate = jax.nn.sigmoid(jnp.dot(_rms(x, g_ref[...]).astype(BF16), wg_ref[...], preferred_element_type=F32))
    x = x + gate * jnp.dot(p_ref[...].astype(BF16), wp_ref[...], preferred_element_type=F32)
    if final:
        x = _rms(x, gfin_ref[...])
    o_ref[...] = x


def _ple(x, pe, p, g, wg, wp, gfin, final, tm):
    t = x.shape[0]
    row = lambda w: pl.BlockSpec((tm, w), lambda i: (i, 0))
    full = lambda a, b: pl.BlockSpec((a, b), lambda i: (0, 0))
    return pl.pallas_call(
        functools.partial(_ple_kernel, final=final),
        grid=(t // tm,),
        in_specs=[row(D_MODEL), row(D_MODEL), row(D_PLE), full(1, D_MODEL), full(D_MODEL, D_MODEL),
                  full(D_PLE, D_MODEL), full(1, D_MODEL)],
        out_specs=row(D_MODEL),
        out_shape=jax.ShapeDtypeStruct((t, D_MODEL), F32),
        compiler_params=_cparams("parallel"),
        name="ple",
    )(x, pe, p, g, wg, wp, gfin)


def _rope_tables(seq):
    n_rows = seq // GRID_W
    row = jnp.repeat(jnp.arange(n_rows, dtype=jnp.int32), GRID_W).astype(F32)
    col = jnp.tile(jnp.arange(GRID_W, dtype=jnp.int32), n_rows).astype(F32)
    quarter = HEAD_DIM // 4
    inv = ROPE_THETA ** (-jnp.arange(quarter, dtype=F32) / quarter)
    ang_r = row[:, None] * inv[None, :]
    ang_c = col[:, None] * inv[None, :]
    cos = jnp.concatenate([jnp.cos(ang_r)] * 2 + [jnp.cos(ang_c)] * 2, axis=1)
    sin = jnp.concatenate([-jnp.sin(ang_r), jnp.sin(ang_r), -jnp.sin(ang_c), jnp.sin(ang_c)], axis=1)
    return jnp.tile(cos, (1, N_Q_HEADS)), jnp.tile(sin, (1, N_Q_HEADS))


def _forward(x, p, norm_mix_g, w_in, lam_re, lam_im, log_dt, b_re, b_im, c_re, c_im, d_skip, w_glu,
             b_glu, q_norm_g, k_norm_g, ssm_out_g, attn_out_g, w_out, norm_ffn_g, w_query, sub_keys1,
             sub_keys2, expert_u, expert_v, norm_ple_g, w_ple_gate, w_ple_proj, final_norm_g):
    n_batch, seq, _ = x.shape
    depth = w_in.shape[0]
    t = n_batch * seq
    tm = min(256, seq)
    tq = min(256, seq)
    tk = min(512, seq // 2)
    tt = min(512, t)
    te = 1024
    nc = t // SSM_CHUNK

    cq, sq = _rope_tables(seq)
    bd = jnp.kron(jnp.eye(N_Q_HEADS, dtype=F32), jnp.ones((HEAD_DIM, HEAD_DIM), F32)).astype(BF16)
    row = lambda a: a.reshape(1, -1).astype(F32)

    xf = x.reshape(t, D_MODEL)
    for i in range(depth):
        u, qt, k, vt = _in_proj(xf, row(norm_mix_g[i]), w_in[i].astype(BF16), cq, sq,
                                row(jnp.tile(q_norm_g[i], N_Q_HEADS)), row(jnp.tile(k_norm_g[i], N_KV_HEADS)),
                                bd, seq, tk)

        grp = lambda a: jnp.transpose(a, (1, 0, 2))[:, :, None, :]
        ldt = jnp.broadcast_to(log_dt[i][:, :, None], (2, SSM_G, SSM_N))
        mt, et, fm, dec = _ssm_params(
            grp(lam_re[i]), grp(lam_im[i]), grp(ldt),
            jnp.transpose(b_re[i], (1, 0, 3, 2)), jnp.transpose(b_im[i], (1, 0, 3, 2)),
            jnp.transpose(c_re[i], (1, 0, 2, 3)), jnp.transpose(c_im[i], (1, 0, 2, 3)))
        ug = (u.astype(BF16).reshape(nc, SSM_CHUNK, SSM_G, SSM_P)
              .transpose(2, 0, 1, 3).reshape(SSM_G, nc, SSM_LP))
        yg = _ssm_conv(ug, mt, et, fm, dec, n_batch)
        y_ssm = yg.reshape(SSM_G, nc, SSM_CHUNK, SSM_P).transpose(1, 2, 0, 3).reshape(t, D_SSM)

        attn = _attention(qt, k, vt, n_batch, seq, tq)

        xf, xn = _out_proj(xf, y_ssm, u, attn, row(d_skip[i]), w_glu[i].astype(BF16), row(b_glu[i]),
                           row(ssm_out_g[i]), row(attn_out_g[i]), w_out[i].astype(BF16),
                           row(norm_ffn_g[i]), tm)

        rho, e2, r1, c1 = _peer_route(xn, w_query[i].T.astype(BF16), sub_keys1[i].astype(BF16),
                                      sub_keys2[i].astype(BF16), min(256, t))
        peer_t = _peer_dense(xn.T, expert_u[i].astype(BF16), expert_v[i].T.astype(BF16),
                             rho, e2, r1, c1, tt, te)

        xf = _ple(xf, peer_t.T, p[i].reshape(t, D_PLE), row(norm_ple_g[i]), w_ple_gate[i].astype(BF16),
                  w_ple_proj[i].astype(BF16), row(final_norm_g), i == depth - 1, tm)
    return xf.reshape(n_batch, seq, D_MODEL)


def kernel(x, p, norm_mix_g, w_in, lam_re, lam_im, log_dt, b_re, b_im, c_re, c_im, d_skip, w_glu, b_glu, q_norm_g, k_norm_g, ssm_out_g, attn_out_g, w_out, norm_ffn_g, w_query, sub_keys1, sub_keys2, expert_u, expert_v, norm_ple_g, w_ple_gate, w_ple_proj, final_norm_g):
    return _forward(x, p, norm_mix_g, w_in, lam_re, lam_im, log_dt, b_re, b_im, c_re, c_im, d_skip, w_glu,
                    b_glu, q_norm_g, k_norm_g, ssm_out_g, attn_out_g, w_out, norm_ffn_g, w_query,
                    sub_keys1, sub_keys2, expert_u, expert_v, norm_ple_g, w_ple_gate, w_ple_proj,
                    final_norm_g)
```

```python
import functools
import math

import jax
import jax.numpy as jnp
from jax import lax
from jax.experimental import pallas as pl
from jax.experimental.pallas import tpu as pltpu

F32 = jnp.float32
BF16 = jnp.bfloat16

D_MODEL = 1024
D_PLE = 256
GRID_W = 64
EPS = 1e-6
LOG2_E = 1.4426950408889634
D_SSM = 512
SSM_P = 16
SSM_G = D_SSM // SSM_P
SSM_N = 64
D_ATTN = 512
HEAD_DIM = 64
V_ROWS = HEAD_DIM + 16
N_Q_HEADS = 8
N_KV_HEADS = 2
Q_PER_KV = 4
D_KV = 128
D_IN = D_SSM + D_ATTN + 2 * D_KV
ROPE_THETA = 10000.0
PEER_HEADS = 8
PEER_KEY_DIM = 256
PEER_HALF = 128
N_KEYS = 128
PEER_TOPK = 16

PEER_BLOCK_ROWS = 2
SSM_CHUNK = 32
SSM_LP = SSM_CHUNK * SSM_P
VMEM_LIMIT = 48 * 1024 * 1024

def _cparams(*sem):
    return pltpu.CompilerParams(dimension_semantics=sem, vmem_limit_bytes=VMEM_LIMIT)


def _rms(x, g):
    return x * lax.rsqrt(jnp.mean(x * x, axis=-1, keepdims=True) + EPS) * g


def _in_proj_kernel(x_ref, g_ref, w_ref, cq_ref, sq_ref, gq_ref, gk_ref, bd_ref,
                    u_ref, qt_ref, k_ref, vt_ref):
    h = _rms(x_ref[...], g_ref[...])
    z = jnp.dot(h.astype(BF16), w_ref[...], preferred_element_type=F32)
    u_ref[...] = z[:, :D_SSM]

    def head_norm_rope(t, gain, bd, c, s):
        ss = jnp.dot((t * t).astype(BF16), bd, preferred_element_type=F32)
        t = t * lax.rsqrt(ss * (1.0 / HEAD_DIM) + EPS) * gain
        w = t.shape[1]
        lane = lax.broadcasted_iota(jnp.int32, t.shape, 1)
        partner = jnp.where((lane % 32) < 16, pltpu.roll(t, w - 16, 1), pltpu.roll(t, 16, 1))
        return t * c + partner * s

    cq = cq_ref[...]
    sq = sq_ref[...]
    q = head_norm_rope(z[:, D_SSM:D_SSM + D_ATTN], gq_ref[...], bd_ref[...], cq, sq)
    qt_ref[...] = (q * (HEAD_DIM ** -0.5 * LOG2_E)).T.astype(BF16)
    k = head_norm_rope(z[:, D_SSM + D_ATTN:D_SSM + D_ATTN + D_KV], gk_ref[...],
                       bd_ref[:D_KV, :D_KV], cq[:, :D_KV], sq[:, :D_KV])
    vt = z[:, D_SSM + D_ATTN + D_KV:].T
    row = lax.broadcasted_iota(jnp.int32, (V_ROWS - HEAD_DIM, vt.shape[1]), 0)
    ones_row = jnp.where(row == 0, 1.0, 0.0).astype(F32)
    for j in range(N_KV_HEADS):
        k_ref[j] = k[:, j * HEAD_DIM:(j + 1) * HEAD_DIM].astype(BF16)
        vt_ref[j, 0] = jnp.concatenate([vt[j * HEAD_DIM:(j + 1) * HEAD_DIM], ones_row], axis=0).astype(BF16)


def _in_proj(x, g, w, cq, sq, gq, gk, bd, seq, tm):
    t = x.shape[0]
    nseq = seq // tm
    return pl.pallas_call(
        _in_proj_kernel,
        grid=(t // tm,),
        in_specs=[
            pl.BlockSpec((tm, D_MODEL), lambda i: (i, 0)),
            pl.BlockSpec((1, D_MODEL), lambda i: (0, 0)),
            pl.BlockSpec((D_MODEL, D_IN), lambda i: (0, 0)),
            pl.BlockSpec((tm, D_ATTN), lambda i: (i % nseq, 0)),
            pl.BlockSpec((tm, D_ATTN), lambda i: (i % nseq, 0)),
            pl.BlockSpec((1, D_ATTN), lambda i: (0, 0)),
            pl.BlockSpec((1, D_KV), lambda i: (0, 0)),
            pl.BlockSpec((D_ATTN, D_ATTN), lambda i: (0, 0)),
        ],
        out_specs=[
            pl.BlockSpec((tm, D_SSM), lambda i: (i, 0)),
            pl.BlockSpec((D_ATTN, tm), lambda i: (0, i)),
            pl.BlockSpec((N_KV_HEADS, tm, HEAD_DIM), lambda i: (0, i, 0)),
            pl.BlockSpec((N_KV_HEADS, 1, V_ROWS, tm), lambda i: (0, i, 0, 0)),
        ],
        out_shape=[
            jax.ShapeDtypeStruct((t, D_SSM), F32),
            jax.ShapeDtypeStruct((D_ATTN, t), BF16),
            jax.ShapeDtypeStruct((N_KV_HEADS, t, HEAD_DIM), BF16),
            jax.ShapeDtypeStruct((N_KV_HEADS, t // tm, V_ROWS, tm), BF16),
        ],
        compiler_params=_cparams("parallel"),
        name="in_proj",
    )(x, g, w, cq, sq, gq, gk, bd)


def _ssm_params_kernel(lr_ref, li_ref, ldt_ref, bt_re_ref, bt_im_ref, c_re_ref, c_im_ref,
                       mt_ref, et_ref, fm_ref, dec_ref):
    L = SSM_CHUNK
    hi = lax.Precision.HIGHEST
    r_parts, e_parts, f_parts, p_parts, q_parts = [], [], [], [], []
    idx = lax.broadcasted_iota(jnp.int32, (L, 1, SSM_N), 0).astype(F32)
    for d in range(2):
        lr = lr_ref[0, d]
        li = li_ref[0, d]
        dt = jnp.exp(ldt_ref[0, d])

        def apow(tau):
            mag = jnp.exp(tau * (lr * dt))
            ang = tau * (li * dt)
            return mag * jnp.cos(ang), mag * jnp.sin(ang)

        ab_re, ab_im = apow(jnp.ones((1, SSM_N), F32))
        den = lr * lr + li * li
        nr = ab_re - 1.0
        f_re = (nr * lr + ab_im * li) / den
        f_im = (ab_im * lr - nr * li) / den
        bt_re = bt_re_ref[0, d]
        bt_im = bt_im_ref[0, d]
        bb_re = f_re * bt_re - f_im * bt_im
        bb_im = f_re * bt_im + f_im * bt_re
        c_re = c_re_ref[0, d]
        c_im = c_im_ref[0, d]

        tau_k = idx if d == 0 else (L - 1.0) - idx
        pr, pi = apow(tau_k)
        lm_re = (c_re[None] * pr - c_im[None] * pi).reshape(L * SSM_P, SSM_N)
        lm_im = (c_re[None] * pi + c_im[None] * pr).reshape(L * SSM_P, SSM_N)
        nt = (((1,), (1,)), ((), ()))
        r_parts.append(lax.dot_general(bb_re, lm_re, nt, precision=hi, preferred_element_type=F32)
                       - lax.dot_general(bb_im, lm_im, nt, precision=hi, preferred_element_type=F32))

        tau_e = (L - 1.0) - idx if d == 0 else idx
        pr, pi = apow(tau_e)
        e_parts.append((pr * bb_re[None] - pi * bb_im[None]).reshape(L * SSM_P, SSM_N))
        e_parts.append((pr * bb_im[None] + pi * bb_re[None]).reshape(L * SSM_P, SSM_N))

        tau_f = idx + 1.0 if d == 0 else L - idx
        pr, pi = apow(tau_f)
        f_parts.append((c_re[None] * pr - c_im[None] * pi).reshape(L * SSM_P, SSM_N))
        f_parts.append(-(c_re[None] * pi + c_im[None] * pr).reshape(L * SSM_P, SSM_N))

        al_re, al_im = apow(jnp.full((1, SSM_N), float(L), F32))
        p_parts += [al_re, al_re]
        q_parts += [-al_im, al_im]

    zeros = jnp.zeros((SSM_P, SSM_LP), F32)
    r_bwd = jnp.concatenate([r_parts[1], zeros], axis=1)
    r_fwd = pltpu.roll(jnp.concatenate([r_parts[0], zeros], axis=1), SSM_LP - SSM_P, 1)
    r_cat = r_bwd + r_fwd
    for s in range(L):
        shift = (2 * SSM_LP - (L - 1 - s) * SSM_P) % (2 * SSM_LP)
        win = r_cat if shift == 0 else pltpu.roll(r_cat, shift, 1)
        mt_ref[0, s * SSM_P:(s + 1) * SSM_P, :] = win[:, :SSM_LP].astype(BF16)
    et_ref[0] = jnp.concatenate(e_parts, axis=1).astype(BF16)
    fm_ref[0] = jnp.concatenate(f_parts, axis=1).astype(BF16)
    dec_ref[0] = jnp.concatenate([jnp.concatenate(p_parts, axis=1),
                                  jnp.concatenate(q_parts, axis=1)], axis=0)


def _ssm_params(lr, li, ldt, bt_re, bt_im, c_re, c_im):
    g = lr.shape[0]
    vec = pl.BlockSpec((1, 2, 1, SSM_N), lambda i: (i, 0, 0, 0))
    mat = pl.BlockSpec((1, 2, SSM_P, SSM_N), lambda i: (i, 0, 0, 0))
    return pl.pallas_call(
        _ssm_params_kernel,
        grid=(g,),
        in_specs=[vec, vec, vec, mat, mat, mat, mat],
        out_specs=[
            pl.BlockSpec((1, SSM_LP, SSM_LP), lambda i: (i, 0, 0)),
            pl.BlockSpec((1, SSM_LP, 4 * SSM_N), lambda i: (i, 0, 0)),
            pl.BlockSpec((1, SSM_LP, 4 * SSM_N), lambda i: (i, 0, 0)),
            pl.BlockSpec((1, 2, 4 * SSM_N), lambda i: (i, 0, 0)),
        ],
        out_shape=[
            jax.ShapeDtypeStruct((g, SSM_LP, SSM_LP), BF16),
            jax.ShapeDtypeStruct((g, SSM_LP, 4 * SSM_N), BF16),
            jax.ShapeDtypeStruct((g, SSM_LP, 4 * SSM_N), BF16),
            jax.ShapeDtypeStruct((g, 2, 4 * SSM_N), F32),
        ],
        compiler_params=_cparams("parallel"),
        name="ssm_params",
    )(lr, li, ldt, bt_re, bt_im, c_re, c_im)


def _ssm_conv_kernel(u_ref, mt_ref, et_ref, fm_ref, dec_ref, y_ref, *, n_batch):
    u = u_ref[0]
    nc = u.shape[0]
    ncb = nc // n_batch
    w = 2 * SSM_N
    s_all = jnp.dot(u, et_ref[0], preferred_element_type=F32)
    dec = dec_ref[0]
    row = lax.broadcasted_iota(jnp.int32, (ncb, w), 0)

    def cmul(x, p, q):
        return p * x + q * pltpu.roll(x, SSM_N, 1)

    h_rows = []
    for b in range(n_batch):
        s_b = s_all[b * ncb:(b + 1) * ncb]
        xf, xb = s_b[:, :w], s_b[:, w:]
        pf, qf = dec[0:1, :w], dec[1:2, :w]
        pb, qb = dec[0:1, w:], dec[1:2, w:]
        k = 1
        while k < ncb:
            xf = xf + jnp.where(row >= k, cmul(pltpu.roll(xf, k, 0), pf, qf), 0.0)
            xb = xb + jnp.where(row < ncb - k, cmul(pltpu.roll(xb, ncb - k, 0), pb, qb), 0.0)
            pf, qf = pf * pf - qf * qf, 2.0 * pf * qf
            pb, qb = pb * pb - qb * qb, 2.0 * pb * qb
            k *= 2
        hf = jnp.where(row >= 1, pltpu.roll(xf, 1, 0), 0.0)
        hb = jnp.where(row < ncb - 1, pltpu.roll(xb, ncb - 1, 0), 0.0)
        h_rows.append(jnp.concatenate([hf, hb], axis=1))
    h = jnp.concatenate(h_rows, axis=0).astype(BF16)
    y = jnp.dot(u, mt_ref[0], preferred_element_type=F32)
    y = y + lax.dot_general(h, fm_ref[0], (((1,), (1,)), ((), ())), preferred_element_type=F32)
    y_ref[0] = y


def _ssm_conv(ug, mt, et, fm, dec, n_batch):
    g, nc, lp = ug.shape
    return pl.pallas_call(
        functools.partial(_ssm_conv_kernel, n_batch=n_batch),
        grid=(g,),
        in_specs=[
            pl.BlockSpec((1, nc, lp), lambda i: (i, 0, 0)),
            pl.BlockSpec((1, lp, lp), lambda i: (i, 0, 0)),
            pl.BlockSpec((1, lp, 4 * SSM_N), lambda i: (i, 0, 0)),
            pl.BlockSpec((1, lp, 4 * SSM_N), lambda i: (i, 0, 0)),
            pl.BlockSpec((1, 2, 4 * SSM_N), lambda i: (i, 0, 0)),
        ],
        out_specs=pl.BlockSpec((1, nc, lp), lambda i: (i, 0, 0)),
        out_shape=jax.ShapeDtypeStruct((g, nc, lp), F32),
        compiler_params=_cparams("parallel"),
        name="ssm_conv",
    )(ug, mt, et, fm, dec)


def _attn_kernel(qt_ref, k_ref, vt_ref, o_ref, q_sc, m_ref, acc_ref, sa_ref, sb_ref, *, tq, tk, nk):
    qt = qt_ref[...]
    for j in range(Q_PER_KV):
        q_sc[:, j * tq:(j + 1) * tq] = qt[j * HEAD_DIM:(j + 1) * HEAD_DIM, :]
    m_ref[...] = jnp.full(m_ref.shape, -jnp.inf, F32)
    acc_ref[...] = jnp.zeros(acc_ref.shape, F32)

    def scores(i):
        kb = k_ref[0, pl.ds(pl.multiple_of(i * tk, tk), tk), :]
        return jnp.dot(kb, q_sc[...], preferred_element_type=F32)

    def accumulate(s_ref, i):
        s = s_ref[...]
        m_prev = m_ref[...]
        m_new = jnp.maximum(m_prev, jnp.max(s, axis=0, keepdims=True))
        alpha = jnp.exp2(m_prev - m_new)
        p = jnp.exp2(s - m_new[0:1, :])
        acc_ref[:V_ROWS, :] = (acc_ref[:V_ROWS, :] * alpha[0:1, :]
                               + jnp.dot(vt_ref[0, i], p.astype(BF16), preferred_element_type=F32))
        m_ref[...] = m_new

    sa_ref[...] = scores(0)

    def body(j, carry):
        sb_ref[...] = scores(2 * j + 1)
        accumulate(sa_ref, 2 * j)
        sa_ref[...] = scores(2 * j + 2)
        accumulate(sb_ref, 2 * j + 1)
        return carry

    lax.fori_loop(0, nk // 2 - 1, body, 0)
    sb_ref[...] = scores(nk - 1)
    accumulate(sa_ref, nk - 2)
    accumulate(sb_ref, nk - 1)
    acc = acc_ref[...]
    out = (acc / acc[HEAD_DIM:HEAD_DIM + 1, :]).T
    o_ref[...] = jnp.concatenate([out[j * tq:(j + 1) * tq, :HEAD_DIM] for j in range(Q_PER_KV)], axis=1)


def _attention(qt, k, vt, n_batch, seq, tq):
    t = qt.shape[1]
    tk = vt.shape[3]
    nq = seq // tq
    nk = seq // tk
    assert nk % 2 == 0, "the key loop handles chunks in pairs"
    w = Q_PER_KV * HEAD_DIM
    return pl.pallas_call(
        functools.partial(_attn_kernel, tq=tq, tk=tk, nk=nk),
        grid=(n_batch, N_KV_HEADS, nq),
        in_specs=[
            pl.BlockSpec((w, tq), lambda b, g, i: (g, b * nq + i)),
            pl.BlockSpec((1, seq, HEAD_DIM), lambda b, g, i: (g, b, 0)),
            pl.BlockSpec((1, nk, V_ROWS, tk), lambda b, g, i: (g, b, 0, 0)),
        ],
        out_specs=pl.BlockSpec((tq, w), lambda b, g, i: (b * nq + i, g)),
        out_shape=jax.ShapeDtypeStruct((t, D_ATTN), F32),
        scratch_shapes=[
            pltpu.VMEM((HEAD_DIM, Q_PER_KV * tq), BF16),
            pltpu.VMEM((8, Q_PER_KV * tq), F32),
            pltpu.VMEM((2 * HEAD_DIM, Q_PER_KV * tq), F32),
            pltpu.VMEM((tk, Q_PER_KV * tq), F32),
            pltpu.VMEM((tk, Q_PER_KV * tq), F32),
        ],
        compiler_params=_cparams("parallel", "parallel", "parallel"),
        name="attention",
    )(qt, k, vt)


def _out_proj_kernel(x_ref, ys_ref, u_ref, at_ref, dsk_ref, wglu_ref, bglu_ref, gs_ref, ga_ref,
                     wo_ref, gf_ref, xo_ref, xn_ref):
    y = ys_ref[...] + dsk_ref[...] * u_ref[...]
    y = jax.nn.gelu(y)
    z = jnp.dot(y.astype(BF16), wglu_ref[...], preferred_element_type=F32) + bglu_ref[...]
    y = y * jax.nn.sigmoid(z)
    ys = _rms(y, gs_ref[...]).astype(BF16)
    ya = _rms(at_ref[...], ga_ref[...]).astype(BF16)
    xo = (x_ref[...]
          + jnp.dot(ys, wo_ref[:D_SSM, :], preferred_element_type=F32)
          + jnp.dot(ya, wo_ref[D_SSM:, :], preferred_element_type=F32))
    xo_ref[...] = xo
    xn_ref[...] = _rms(xo, gf_ref[...]).astype(BF16)


def _out_proj(x, ys, u, at, dsk, wglu, bglu, gs, ga, wo, gf, tm):
    t = x.shape[0]
    row = lambda w: pl.BlockSpec((tm, w), lambda i: (i, 0))
    full = lambda a, b: pl.BlockSpec((a, b), lambda i: (0, 0))
    return pl.pallas_call(
        _out_proj_kernel,
        grid=(t // tm,),
        in_specs=[row(D_MODEL), row(D_SSM), row(D_SSM), row(D_ATTN), full(1, D_SSM),
                  full(D_SSM, D_SSM), full(1, D_SSM), full(1, D_SSM), full(1, D_ATTN),
                  full(D_MODEL, D_MODEL), full(1, D_MODEL)],
        out_specs=[row(D_MODEL), row(D_MODEL)],
        out_shape=[jax.ShapeDtypeStruct((t, D_MODEL), F32), jax.ShapeDtypeStruct((t, D_MODEL), BF16)],
        compiler_params=_cparams("parallel"),
        name="out_proj",
    )(x, ys, u, at, dsk, wglu, bglu, gs, ga, wo, gf)


def _oe_sort_network(n):
    pairs = []
    p = 1
    while p < n:
        k = p
        while k >= 1:
            for j in range(k % p, n - k, 2 * k):
                for i in range(min(k, n - j - k)):
                    if (i + j) // (2 * p) == (i + j + k) // (2 * p):
                        pairs.append((i + j, i + j + k))
            k //= 2
        p *= 2
    return pairs


def _ce(a, b):
    if b is None:
        return a, None
    if a is None:
        return b, None
    return jnp.maximum(a, b), jnp.minimum(a, b)


def _sorted_top16(slabs):
    n = PEER_TOPK
    v = list(slabs) + [None] * (n - len(slabs))
    for i, j in _oe_sort_network(n):
        v[i], v[j] = _ce(v[i], v[j])
    for shift in (4, 2, 1):
        partner = [None if x is None else pltpu.roll(x, shift, 0) for x in v]
        top = []
        for a in range(n):
            x, y = v[a], partner[n - 1 - a]
            top.append(y if x is None else (x if y is None else jnp.maximum(x, y)))
        v = top
        d = n // 2
        while d >= 1:
            for i in range(n):
                if i & d == 0:
                    v[i], v[i + d] = _ce(v[i], v[i + d])
            d //= 2
    return v


_CAND = [(a, b) for a in range(PEER_TOPK) for b in range(PEER_TOPK) if (a + 1) * (b + 1) <= PEER_TOPK]


def _count_prefix(pred, vals):
    pick = lambda m, a, b: jnp.where(m, a, b)
    m8 = pred(vals[7])
    m4 = pred(pick(m8, vals[11], vals[3]))
    m2 = pred(pick(m8, pick(m4, vals[13], vals[9]), pick(m4, vals[5], vals[1])))
    m1 = pred(pick(m8, pick(m4, pick(m2, vals[14], vals[12]), pick(m2, vals[10], vals[8])),
                   pick(m4, pick(m2, vals[6], vals[4]), pick(m2, vals[2], vals[0]))))
    cnt = pick(m8, 8.0, 0.0) + pick(m4, 4.0, 0.0) + pick(m2, 2.0, 0.0) + pick(m1, 1.0, 0.0)
    return cnt + pick(pred(vals[15]), 1.0, 0.0)


def _bf16_pair_words(x):
    b = pltpu.bitcast(x, jnp.uint32)
    b = (b + jnp.uint32(0x7FFF) + ((b >> 16) & jnp.uint32(1))) >> 16
    return b | (b << 16)


def _peer_route_kernel(xn_ref, wq_ref, k1_ref, k2_ref, rho_ref, e2_ref, r1_ref, c1_ref):
    nt = (((1,), (1,)), ((), ()))

    def scores(h):
        qt = lax.dot_general(wq_ref[h * PEER_KEY_DIM:(h + 1) * PEER_KEY_DIM, :], xn_ref[...], nt,
                             preferred_element_type=F32)
        return (jnp.dot(k1_ref[h], qt[:PEER_HALF].astype(BF16), preferred_element_type=F32),
                jnp.dot(k2_ref[h], qt[PEER_HALF:].astype(BF16), preferred_element_type=F32))

    s = scores(0)
    for h in range(PEER_HEADS):
        s_next = scores(h + 1) if h + 1 < PEER_HEADS else None
        _peer_route_head(h, s[0], s[1], rho_ref, e2_ref, r1_ref, c1_ref)
        s = s_next


def _peer_route_head(h, s1, s2, rho_ref, e2_ref, r1_ref, c1_ref):
    tm = s1.shape[1]
    nslab = N_KEYS // 8
    s1 = [s1[8 * i:8 * i + 8] for i in range(nslab)]
    s2 = [s2[8 * i:8 * i + 8] for i in range(nslab)]
    v1 = _sorted_top16(s1)
    v2 = _sorted_top16(s2)

    sub = lax.broadcasted_iota(jnp.int32, (8, tm), 0)
    cand = [v1[a] + v2[b] for a, b in _CAND]
    packed = []
    for base in range(0, len(cand), 8):
        slab = None
        for s, c in enumerate(cand[base:base + 8]):
            slab = c if slab is None else jnp.where(sub == s, c, slab)
        if len(cand) - base < 8:
            slab = jnp.where(sub < len(cand) - base, slab, -jnp.inf)
        packed.append(slab)
    theta = _sorted_top16(packed)[PEER_TOPK - 1]

    top = cand[0]
    z = jnp.zeros((8, tm), F32)
    for c in cand:
        z = z + jnp.where(c >= theta, jnp.exp(c - top), 0.0)
    inv_z = 1.0 / z

    r1, c1, rho, e2 = [], [], [], []
    for i in range(nslab):
        r1.append(_count_prefix(lambda v, x=s1[i]: x + v >= theta, v2))
        rho.append(_count_prefix(lambda v, x=s2[i]: v > x, v2))
        c1.append(jnp.exp(s1[i] - v1[0]) * inv_z)
        e2.append(jnp.exp(s2[i] - v2[0]))
    r1 = _bf16_pair_words(jnp.concatenate(r1, axis=0))
    c1 = _bf16_pair_words(jnp.concatenate(c1, axis=0))
    rho = jnp.concatenate(rho, axis=0).astype(BF16)
    e2 = jnp.concatenate(e2, axis=0).astype(BF16)
    for l in range(tm // 128):
        lanes = slice(l * 128, (l + 1) * 128)
        r1_ref[h, l] = r1[:, lanes]
        c1_ref[h, l] = c1[:, lanes]
    for l in range(tm // 256):
        lanes = slice(l * 256, (l + 1) * 256)
        rho_ref[h, l * N_KEYS:(l + 1) * N_KEYS, :] = rho[:, lanes]
        e2_ref[h, l * N_KEYS:(l + 1) * N_KEYS, :] = e2[:, lanes]


def _peer_route(xn, wq_t, k1, k2, tm):
    t = xn.shape[0]
    sel = pl.BlockSpec((PEER_HEADS, tm // 128, N_KEYS, 128), lambda i: (0, i, 0, 0))
    shp = jax.ShapeDtypeStruct((PEER_HEADS, t // 128, N_KEYS, 128), jnp.uint32)
    tok = pl.BlockSpec((PEER_HEADS, tm // 2, 256), lambda i: (0, i, 0))
    shp16 = jax.ShapeDtypeStruct((PEER_HEADS, t // 2, 256), BF16)
    return pl.pallas_call(
        _peer_route_kernel,
        grid=(t // tm,),
        in_specs=[
            pl.BlockSpec((tm, D_MODEL), lambda i: (i, 0)),
            pl.BlockSpec((PEER_HEADS * PEER_KEY_DIM, D_MODEL), lambda i: (0, 0)),
            pl.BlockSpec((PEER_HEADS, N_KEYS, PEER_HALF), lambda i: (0, 0, 0)),
            pl.BlockSpec((PEER_HEADS, N_KEYS, PEER_HALF), lambda i: (0, 0, 0)),
        ],
        out_specs=[tok, tok, sel, sel],
        out_shape=[shp16, shp16, shp, shp],
        compiler_params=_cparams("parallel"),
        name="peer_route",
    )(xn, wq_t, k1, k2)


def _gelu_tanh(x):
    k1 = -2.0 * math.sqrt(2.0 / math.pi) * LOG2_E
    t = x * (k1 + (k1 * 0.044715) * (x * x))
    return x / (1.0 + jnp.exp2(t))


def _load_row_repeated(ref, h, l, i, n):
    return ref[h, l, pl.ds(i, n, stride=0), :]


def _peer_dense_kernel(xn_ref, u_ref, vt_ref, rho_ref, e2_ref, r1_ref, c1_ref, o_ref, a_ref, *, rows):
    e = pl.program_id(1)

    @pl.when(e == 0)
    def _():
        o_ref[...] = jnp.zeros(o_ref.shape, F32)

    tt = xn_ref.shape[1]
    sub = 16
    blk = PEER_BLOCK_ROWS * N_KEYS
    nblk = rows // PEER_BLOCK_ROWS
    strip = 256
    zero = jnp.zeros((sub, strip), BF16)

    def row_tile(ref, h, s, i):
        return jnp.concatenate(
            [pltpu.bitcast(_load_row_repeated(ref, h, s * (strip // 128) + l, i, sub // 2), BF16)
             for l in range(strip // 128)], axis=1)

    def first_matmul(k):
        return jnp.dot(u_ref[k * blk:(k + 1) * blk, :], xn_ref[...], preferred_element_type=F32)

    act = first_matmul(0)
    for k in range(nblk):
        act_next = first_matmul(k + 1) if k + 1 < nblk else None
        a_ref[k * blk:(k + 1) * blk, :] = _gelu_tanh(act).astype(BF16)
        for s in range(tt // strip):
            lanes = slice(s * strip, (s + 1) * strip)
            for ii in range(PEER_BLOCK_ROWS):
                i = k * PEER_BLOCK_ROWS + ii
                rc = [(row_tile(r1_ref, h, s, i), row_tile(c1_ref, h, s, i)) for h in range(PEER_HEADS)]
                for jb in range(N_KEYS // sub):
                    keys = slice(s * N_KEYS + jb * sub, s * N_KEYS + (jb + 1) * sub)
                    w = None
                    for h in range(PEER_HEADS):
                        r, c = rc[h]
                        term = c * jnp.where(rho_ref[h, keys, :] < r, e2_ref[h, keys, :], zero)
                        w = term if w is None else w + term
                    lo = k * blk + ii * N_KEYS + jb * sub
                    a_ref[lo:lo + sub, lanes] = a_ref[lo:lo + sub, lanes] * w
        o_ref[...] += jnp.dot(vt_ref[:, k * blk:(k + 1) * blk], a_ref[k * blk:(k + 1) * blk, :],
                              preferred_element_type=F32)
        act = act_next


def _peer_dense(xn_t, eu, evt, rho, e2, r1, c1, tt, te):
    t = xn_t.shape[1]
    ne = eu.shape[0]
    rows = te // N_KEYS
    tok = pl.BlockSpec((PEER_HEADS, tt // 2, 256), lambda j, e: (0, j, 0))
    sel = pl.BlockSpec((PEER_HEADS, tt // 128, rows, 128), lambda j, e: (0, j, e, 0))
    return pl.pallas_call(
        functools.partial(_peer_dense_kernel, rows=rows),
        grid=(t // tt, ne // te),
        in_specs=[
            pl.BlockSpec((D_MODEL, tt), lambda j, e: (0, j)),
            pl.BlockSpec((te, D_MODEL), lambda j, e: (e, 0)),
            pl.BlockSpec((D_MODEL, te), lambda j, e: (0, e)),
            tok, tok, sel, sel,
        ],
        out_specs=pl.BlockSpec((D_MODEL, tt), lambda j, e: (0, j)),
        out_shape=jax.ShapeDtypeStruct((D_MODEL, t), F32),
        scratch_shapes=[pltpu.VMEM((te, tt), BF16)],
        compiler_params=_cparams("parallel", "arbitrary"),
        name="peer_dense",
    )(xn_t, eu, evt, rho, e2, r1, c1)


def _ple_kernel(x_ref, pe_ref, p_ref, g_ref, wg_ref, wp_ref, gfin_ref, o_ref, *, final):
    x = x_ref[...] + pe_ref[...]
    gate = jax.nn.sigmoid(jnp.dot(_rms(x, g_ref[...]).astype(BF16), wg_ref[...], preferred_element_type=F32))
    x = x + gate * jnp.dot(p_ref[...].astype(BF16), wp_ref[...], preferred_element_type=F32)
    if final:
        x = _rms(x, gfin_ref[...])
    o_ref[...] = x


def _ple(x, pe, p, g, wg, wp, gfin, final, tm):
    t = x.shape[0]
    row = lambda w: pl.BlockSpec((tm, w), lambda i: (i, 0))
    full = lambda a, b: pl.BlockSpec((a, b), lambda i: (0, 0))
    return pl.pallas_call(
        functools.partial(_ple_kernel, final=final),
        grid=(t // tm,),
        in_specs=[row(D_MODEL), row(D_MODEL), row(D_PLE), full(1, D_MODEL), full(D_MODEL, D_MODEL),
                  full(D_PLE, D_MODEL), full(1, D_MODEL)],
        out_specs=row(D_MODEL),
        out_shape=jax.ShapeDtypeStruct((t, D_MODEL), F32),
        compiler_params=_cparams("parallel"),
        name="ple",
    )(x, pe, p, g, wg, wp, gfin)


def _rope_tables(seq):
    n_rows = seq // GRID_W
    row = jnp.repeat(jnp.arange(n_rows, dtype=jnp.int32), GRID_W).astype(F32)
    col = jnp.tile(jnp.arange(GRID_W, dtype=jnp.int32), n_rows).astype(F32)
    quarter = HEAD_DIM // 4
    inv = ROPE_THETA ** (-jnp.arange(quarter, dtype=F32) / quarter)
    ang_r = row[:, None] * inv[None, :]
    ang_c = col[:, None] * inv[None, :]
    cos = jnp.concatenate([jnp.cos(ang_r)] * 2 + [jnp.cos(ang_c)] * 2, axis=1)
    sin = jnp.concatenate([-jnp.sin(ang_r), jnp.sin(ang_r), -jnp.sin(ang_c), jnp.sin(ang_c)], axis=1)
    return jnp.tile(cos, (1, N_Q_HEADS)), jnp.tile(sin, (1, N_Q_HEADS))


def _forward(x, p, norm_mix_g, w_in, lam_re, lam_im, log_dt, b_re, b_im, c_re, c_im, d_skip, w_glu,
             b_glu, q_norm_g, k_norm_g, ssm_out_g, attn_out_g, w_out, norm_ffn_g, w_query, sub_keys1,
             sub_keys2, expert_u, expert_v, norm_ple_g, w_ple_gate, w_ple_proj, final_norm_g):
    n_batch, seq, _ = x.shape
    depth = w_in.shape[0]
    t = n_batch * seq
    tm = min(256, seq)
    tq = min(256, seq)
    tk = min(512, seq // 2)
    tt = min(512, t)
    te = 1024
    nc = t // SSM_CHUNK

    cq, sq = _rope_tables(seq)
    bd = jnp.kron(jnp.eye(N_Q_HEADS, dtype=F32), jnp.ones((HEAD_DIM, HEAD_DIM), F32)).astype(BF16)
    row = lambda a: a.reshape(1, -1).astype(F32)

    xf = x.reshape(t, D_MODEL)
    for i in range(depth):
        u, qt, k, vt = _in_proj(xf, row(norm_mix_g[i]), w_in[i].astype(BF16), cq, sq,
                                row(jnp.tile(q_norm_g[i], N_Q_HEADS)), row(jnp.tile(k_norm_g[i], N_KV_HEADS)),
                                bd, seq, tk)

        grp = lambda a: jnp.transpose(a, (1, 0, 2))[:, :, None, :]
        ldt = jnp.broadcast_to(log_dt[i][:, :, None], (2, SSM_G, SSM_N))
        mt, et, fm, dec = _ssm_params(
            grp(lam_re[i]), grp(lam_im[i]), grp(ldt),
            jnp.transpose(b_re[i], (1, 0, 3, 2)), jnp.transpose(b_im[i], (1, 0, 3, 2)),
            jnp.transpose(c_re[i], (1, 0, 2, 3)), jnp.transpose(c_im[i], (1, 0, 2, 3)))
        ug = (u.astype(BF16).reshape(nc, SSM_CHUNK, SSM_G, SSM_P)
              .transpose(2, 0, 1, 3).reshape(SSM_G, nc, SSM_LP))
        yg = _ssm_conv(ug, mt, et, fm, dec, n_batch)
        y_ssm = yg.reshape(SSM_G, nc, SSM_CHUNK, SSM_P).transpose(1, 2, 0, 3).reshape(t, D_SSM)

        attn = _attention(qt, k, vt, n_batch, seq, tq)

        xf, xn = _out_proj(xf, y_ssm, u, attn, row(d_skip[i]), w_glu[i].astype(BF16), row(b_glu[i]),
                           row(ssm_out_g[i]), row(attn_out_g[i]), w_out[i].astype(BF16),
                           row(norm_ffn_g[i]), tm)

        rho, e2, r1, c1 = _peer_route(xn, w_query[i].T.astype(BF16), sub_keys1[i].astype(BF16),
                                      sub_keys2[i].astype(BF16), min(256, t))
        peer_t = _peer_dense(xn.T, expert_u[i].astype(BF16), expert_v[i].T.astype(BF16),
                             rho, e2, r1, c1, tt, te)

        xf = _ple(xf, peer_t.T, p[i].reshape(t, D_PLE), row(norm_ple_g[i]), w_ple_gate[i].astype(BF16),
                  w_ple_proj[i].astype(BF16), row(final_norm_g), i == depth - 1, tm)
    return xf.reshape(n_batch, seq, D_MODEL)


def kernel(x, p, norm_mix_g, w_in, lam_re, lam_im, log_dt, b_re, b_im, c_re, c_im, d_skip, w_glu, b_glu, q_norm_g, k_norm_g, ssm_out_g, attn_out_g, w_out, norm_ffn_g, w_query, sub_keys1, sub_keys2, expert_u, expert_v, norm_ple_g, w_ple_gate, w_ple_proj, final_norm_g):
    return _forward(x, p, norm_mix_g, w_in, lam_re, lam_im, log_dt, b_re, b_im, c_re, c_im, d_skip, w_glu,
                    b_glu, q_norm_g, k_norm_g, ssm_out_g, attn_out_g, w_out, norm_ffn_g, w_query,
                    sub_keys1, sub_keys2, expert_u, expert_v, norm_ple_g, w_ple_gate, w_ple_proj,
                    final_norm_g)
```

```python
import functools
import math

import jax
import jax.numpy as jnp
from jax import lax
from jax.experimental import pallas as pl
from jax.experimental.pallas import tpu as pltpu

F32 = jnp.float32
BF16 = jnp.bfloat16

D_MODEL = 1024
D_PLE = 256
GRID_W = 64
EPS = 1e-6
LOG2_E = 1.4426950408889634
D_SSM = 512
SSM_P = 16
SSM_G = D_SSM // SSM_P
SSM_N = 64
D_ATTN = 512
HEAD_DIM = 64
V_ROWS = HEAD_DIM + 16
N_Q_HEADS = 8
N_KV_HEADS = 2
Q_PER_KV = 4
D_KV = 128
D_IN = D_SSM + D_ATTN + 2 * D_KV
ROPE_THETA = 10000.0
PEER_HEADS = 8
PEER_KEY_DIM = 256
PEER_HALF = 128
N_KEYS = 128
PEER_TOPK = 16

PEER_BLOCK_ROWS = 2
PEER_STRIP = 256
SSM_CHUNK = 32
SSM_LP = SSM_CHUNK * SSM_P
VMEM_LIMIT = 48 * 1024 * 1024

def _cparams(*sem):
    return pltpu.CompilerParams(dimension_semantics=sem, vmem_limit_bytes=VMEM_LIMIT)


def _rms(x, g):
    return x * lax.rsqrt(jnp.mean(x * x, axis=-1, keepdims=True) + EPS) * g


def _in_proj_kernel(x_ref, g_ref, w_ref, cq_ref, sq_ref, gq_ref, gk_ref, bd_ref,
                    u_ref, qt_ref, k_ref, vt_ref):
    h = _rms(x_ref[...], g_ref[...])
    z = jnp.dot(h.astype(BF16), w_ref[...], preferred_element_type=F32)
    u_ref[...] = z[:, :D_SSM]

    def head_norm_rope(t, gain, bd, c, s):
        ss = jnp.dot((t * t).astype(BF16), bd, preferred_element_type=F32)
        t = t * lax.rsqrt(ss * (1.0 / HEAD_DIM) + EPS) * gain
        w = t.shape[1]
        lane = lax.broadcasted_iota(jnp.int32, t.shape, 1)
        partner = jnp.where((lane % 32) < 16, pltpu.roll(t, w - 16, 1), pltpu.roll(t, 16, 1))
        return t * c + partner * s

    cq = cq_ref[...]
    sq = sq_ref[...]
    q = head_norm_rope(z[:, D_SSM:D_SSM + D_ATTN], gq_ref[...], bd_ref[...], cq, sq)
    qt_ref[...] = (q * (HEAD_DIM ** -0.5 * LOG2_E)).T.astype(BF16)
    k = head_norm_rope(z[:, D_SSM + D_ATTN:D_SSM + D_ATTN + D_KV], gk_ref[...],
                       bd_ref[:D_KV, :D_KV], cq[:, :D_KV], sq[:, :D_KV])
    vt = z[:, D_SSM + D_ATTN + D_KV:].T
    row = lax.broadcasted_iota(jnp.int32, (V_ROWS - HEAD_DIM, vt.shape[1]), 0)
    ones_row = jnp.where(row == 0, 1.0, 0.0).astype(F32)
    for j in range(N_KV_HEADS):
        k_ref[j] = k[:, j * HEAD_DIM:(j + 1) * HEAD_DIM].astype(BF16)
        vt_ref[j, 0] = jnp.concatenate([vt[j * HEAD_DIM:(j + 1) * HEAD_DIM], ones_row], axis=0).astype(BF16)


def _in_proj(x, g, w, cq, sq, gq, gk, bd, seq, tm):
    t = x.shape[0]
    nseq = seq // tm
    return pl.pallas_call(
        _in_proj_kernel,
        grid=(t // tm,),
        in_specs=[
            pl.BlockSpec((tm, D_MODEL), lambda i: (i, 0)),
            pl.BlockSpec((1, D_MODEL), lambda i: (0, 0)),
            pl.BlockSpec((D_MODEL, D_IN), lambda i: (0, 0)),
            pl.BlockSpec((tm, D_ATTN), lambda i: (i % nseq, 0)),
            pl.BlockSpec((tm, D_ATTN), lambda i: (i % nseq, 0)),
            pl.BlockSpec((1, D_ATTN), lambda i: (0, 0)),
            pl.BlockSpec((1, D_KV), lambda i: (0, 0)),
            pl.BlockSpec((D_ATTN, D_ATTN), lambda i: (0, 0)),
        ],
        out_specs=[
            pl.BlockSpec((tm, D_SSM), lambda i: (i, 0)),
            pl.BlockSpec((D_ATTN, tm), lambda i: (0, i)),
            pl.BlockSpec((N_KV_HEADS, tm, HEAD_DIM), lambda i: (0, i, 0)),
            pl.BlockSpec((N_KV_HEADS, 1, V_ROWS, tm), lambda i: (0, i, 0, 0)),
        ],
        out_shape=[
            jax.ShapeDtypeStruct((t, D_SSM), F32),
            jax.ShapeDtypeStruct((D_ATTN, t), BF16),
            jax.ShapeDtypeStruct((N_KV_HEADS, t, HEAD_DIM), BF16),
            jax.ShapeDtypeStruct((N_KV_HEADS, t // tm, V_ROWS, tm), BF16),
        ],
        compiler_params=_cparams("parallel"),
        name="in_proj",
    )(x, g, w, cq, sq, gq, gk, bd)


def _ssm_params_kernel(lr_ref, li_ref, ldt_ref, bt_re_ref, bt_im_ref, c_re_ref, c_im_ref,
                       mt_ref, et_ref, fm_ref, dec_ref):
    L = SSM_CHUNK
    hi = lax.Precision.HIGHEST
    r_parts, e_parts, f_parts, p_parts, q_parts = [], [], [], [], []
    idx = lax.broadcasted_iota(jnp.int32, (L, 1, SSM_N), 0).astype(F32)
    for d in range(2):
        lr = lr_ref[0, d]
        li = li_ref[0, d]
        dt = jnp.exp(ldt_ref[0, d])

        def apow(tau):
            mag = jnp.exp(tau * (lr * dt))
            ang = tau * (li * dt)
            return mag * jnp.cos(ang), mag * jnp.sin(ang)

        ab_re, ab_im = apow(jnp.ones((1, SSM_N), F32))
        den = lr * lr + li * li
        nr = ab_re - 1.0
        f_re = (nr * lr + ab_im * li) / den
        f_im = (ab_im * lr - nr * li) / den
        bt_re = bt_re_ref[0, d]
        bt_im = bt_im_ref[0, d]
        bb_re = f_re * bt_re - f_im * bt_im
        bb_im = f_re * bt_im + f_im * bt_re
        c_re = c_re_ref[0, d]
        c_im = c_im_ref[0, d]

        tau_k = idx if d == 0 else (L - 1.0) - idx
        pr, pi = apow(tau_k)
        lm_re = (c_re[None] * pr - c_im[None] * pi).reshape(L * SSM_P, SSM_N)
        lm_im = (c_re[None] * pi + c_im[None] * pr).reshape(L * SSM_P, SSM_N)
        nt = (((1,), (1,)), ((), ()))
        r_parts.append(lax.dot_general(bb_re, lm_re, nt, precision=hi, preferred_element_type=F32)
                       - lax.dot_general(bb_im, lm_im, nt, precision=hi, preferred_element_type=F32))

        tau_e = (L - 1.0) - idx if d == 0 else idx
        pr, pi = apow(tau_e)
        e_parts.append((pr * bb_re[None] - pi * bb_im[None]).reshape(L * SSM_P, SSM_N))
        e_parts.append((pr * bb_im[None] + pi * bb_re[None]).reshape(L * SSM_P, SSM_N))

        tau_f = idx + 1.0 if d == 0 else L - idx
        pr, pi = apow(tau_f)
        f_parts.append((c_re[None] * pr - c_im[None] * pi).reshape(L * SSM_P, SSM_N))
        f_parts.append(-(c_re[None] * pi + c_im[None] * pr).reshape(L * SSM_P, SSM_N))

        al_re, al_im = apow(jnp.full((1, SSM_N), float(L), F32))
        p_parts += [al_re, al_re]
        q_parts += [-al_im, al_im]

    zeros = jnp.zeros((SSM_P, SSM_LP), F32)
    r_bwd = jnp.concatenate([r_parts[1], zeros], axis=1)
    r_fwd = pltpu.roll(jnp.concatenate([r_parts[0], zeros], axis=1), SSM_LP - SSM_P, 1)
    r_cat = r_bwd + r_fwd
    for s in range(L):
        shift = (2 * SSM_LP - (L - 1 - s) * SSM_P) % (2 * SSM_LP)
        win = r_cat if shift == 0 else pltpu.roll(r_cat, shift, 1)
        mt_ref[0, s * SSM_P:(s + 1) * SSM_P, :] = win[:, :SSM_LP].astype(BF16)
    et_ref[0] = jnp.concatenate(e_parts, axis=1).astype(BF16)
    fm_ref[0] = jnp.concatenate(f_parts, axis=1).astype(BF16)
    dec_ref[0] = jnp.concatenate([jnp.concatenate(p_parts, axis=1),
                                  jnp.concatenate(q_parts, axis=1)], axis=0)


def _ssm_params(lr, li, ldt, bt_re, bt_im, c_re, c_im):
    g = lr.shape[0]
    vec = pl.BlockSpec((1, 2, 1, SSM_N), lambda i: (i, 0, 0, 0))
    mat = pl.BlockSpec((1, 2, SSM_P, SSM_N), lambda i: (i, 0, 0, 0))
    return pl.pallas_call(
        _ssm_params_kernel,
        grid=(g,),
        in_specs=[vec, vec, vec, mat, mat, mat, mat],
        out_specs=[
            pl.BlockSpec((1, SSM_LP, SSM_LP), lambda i: (i, 0, 0)),
            pl.BlockSpec((1, SSM_LP, 4 * SSM_N), lambda i: (i, 0, 0)),
            pl.BlockSpec((1, SSM_LP, 4 * SSM_N), lambda i: (i, 0, 0)),
            pl.BlockSpec((1, 2, 4 * SSM_N), lambda i: (i, 0, 0)),
        ],
        out_shape=[
            jax.ShapeDtypeStruct((g, SSM_LP, SSM_LP), BF16),
            jax.ShapeDtypeStruct((g, SSM_LP, 4 * SSM_N), BF16),
            jax.ShapeDtypeStruct((g, SSM_LP, 4 * SSM_N), BF16),
            jax.ShapeDtypeStruct((g, 2, 4 * SSM_N), F32),
        ],
        compiler_params=_cparams("parallel"),
        name="ssm_params",
    )(lr, li, ldt, bt_re, bt_im, c_re, c_im)


def _ssm_conv_kernel(u_ref, mt_ref, et_ref, fm_ref, dec_ref, y_ref, *, n_batch):
    u = u_ref[0]
    nc = u.shape[0]
    ncb = nc // n_batch
    w = 2 * SSM_N
    s_all = jnp.dot(u, et_ref[0], preferred_element_type=F32)
    dec = dec_ref[0]
    row = lax.broadcasted_iota(jnp.int32, (ncb, w), 0)

    def cmul(x, p, q):
        return p * x + q * pltpu.roll(x, SSM_N, 1)

    h_rows = []
    for b in range(n_batch):
        s_b = s_all[b * ncb:(b + 1) * ncb]
        xf, xb = s_b[:, :w], s_b[:, w:]
        pf, qf = dec[0:1, :w], dec[1:2, :w]
        pb, qb = dec[0:1, w:], dec[1:2, w:]
        k = 1
        while k < ncb:
            xf = xf + jnp.where(row >= k, cmul(pltpu.roll(xf, k, 0), pf, qf), 0.0)
            xb = xb + jnp.where(row < ncb - k, cmul(pltpu.roll(xb, ncb - k, 0), pb, qb), 0.0)
            pf, qf = pf * pf - qf * qf, 2.0 * pf * qf
            pb, qb = pb * pb - qb * qb, 2.0 * pb * qb
            k *= 2
        hf = jnp.where(row >= 1, pltpu.roll(xf, 1, 0), 0.0)
        hb = jnp.where(row < ncb - 1, pltpu.roll(xb, ncb - 1, 0), 0.0)
        h_rows.append(jnp.concatenate([hf, hb], axis=1))
    h = jnp.concatenate(h_rows, axis=0).astype(BF16)
    y = jnp.dot(u, mt_ref[0], preferred_element_type=F32)
    y = y + lax.dot_general(h, fm_ref[0], (((1,), (1,)), ((), ())), preferred_element_type=F32)
    y_ref[0] = y


def _ssm_conv(ug, mt, et, fm, dec, n_batch):
    g, nc, lp = ug.shape
    return pl.pallas_call(
        functools.partial(_ssm_conv_kernel, n_batch=n_batch),
        grid=(g,),
        in_specs=[
            pl.BlockSpec((1, nc, lp), lambda i: (i, 0, 0)),
            pl.BlockSpec((1, lp, lp), lambda i: (i, 0, 0)),
            pl.BlockSpec((1, lp, 4 * SSM_N), lambda i: (i, 0, 0)),
            pl.BlockSpec((1, lp, 4 * SSM_N), lambda i: (i, 0, 0)),
            pl.BlockSpec((1, 2, 4 * SSM_N), lambda i: (i, 0, 0)),
        ],
        out_specs=pl.BlockSpec((1, nc, lp), lambda i: (i, 0, 0)),
        out_shape=jax.ShapeDtypeStruct((g, nc, lp), F32),
        compiler_params=_cparams("parallel"),
        name="ssm_conv",
    )(ug, mt, et, fm, dec)


def _attn_kernel(qt_ref, k_ref, vt_ref, o_ref, q_sc, m_ref, acc_ref, sa_ref, sb_ref, *, tq, tk, nk):
    qt = qt_ref[...]
    for j in range(Q_PER_KV):
        q_sc[:, j * tq:(j + 1) * tq] = qt[j * HEAD_DIM:(j + 1) * HEAD_DIM, :]
    m_ref[...] = jnp.full(m_ref.shape, -jnp.inf, F32)
    acc_ref[...] = jnp.zeros(acc_ref.shape, F32)

    def scores(i):
        kb = k_ref[0, pl.ds(pl.multiple_of(i * tk, tk), tk), :]
        return jnp.dot(kb, q_sc[...], preferred_element_type=F32)

    def accumulate(s_ref, i):
        s = s_ref[...]
        m_prev = m_ref[...]
        m_new = jnp.maximum(m_prev, jnp.max(s, axis=0, keepdims=True))
        alpha = jnp.exp2(m_prev - m_new)
        p = jnp.exp2(s - m_new[0:1, :])
        acc_ref[:V_ROWS, :] = (acc_ref[:V_ROWS, :] * alpha[0:1, :]
                               + jnp.dot(vt_ref[0, i], p.astype(BF16), preferred_element_type=F32))
        m_ref[...] = m_new

    sa_ref[...] = scores(0)

    def body(j, carry):
        sb_ref[...] = scores(2 * j + 1)
        accumulate(sa_ref, 2 * j)
        sa_ref[...] = scores(2 * j + 2)
        accumulate(sb_ref, 2 * j + 1)
        return carry

    lax.fori_loop(0, nk // 2 - 1, body, 0)
    sb_ref[...] = scores(nk - 1)
    accumulate(sa_ref, nk - 2)
    accumulate(sb_ref, nk - 1)
    acc = acc_ref[...]
    out = (acc / acc[HEAD_DIM:HEAD_DIM + 1, :]).T
    o_ref[...] = jnp.concatenate([out[j * tq:(j + 1) * tq, :HEAD_DIM] for j in range(Q_PER_KV)], axis=1)


def _attention(qt, k, vt, n_batch, seq, tq):
    t = qt.shape[1]
    tk = vt.shape[3]
    nq = seq // tq
    nk = seq // tk
    assert nk % 2 == 0, "the key loop handles chunks in pairs"
    w = Q_PER_KV * HEAD_DIM
    return pl.pallas_call(
        functools.partial(_attn_kernel, tq=tq, tk=tk, nk=nk),
        grid=(n_batch, N_KV_HEADS, nq),
        in_specs=[
            pl.BlockSpec((w, tq), lambda b, g, i: (g, b * nq + i)),
            pl.BlockSpec((1, seq, HEAD_DIM), lambda b, g, i: (g, b, 0)),
            pl.BlockSpec((1, nk, V_ROWS, tk), lambda b, g, i: (g, b, 0, 0)),
        ],
        out_specs=pl.BlockSpec((tq, w), lambda b, g, i: (b * nq + i, g)),
        out_shape=jax.ShapeDtypeStruct((t, D_ATTN), F32),
        scratch_shapes=[
            pltpu.VMEM((HEAD_DIM, Q_PER_KV * tq), BF16),
            pltpu.VMEM((8, Q_PER_KV * tq), F32),
            pltpu.VMEM((2 * HEAD_DIM, Q_PER_KV * tq), F32),
            pltpu.VMEM((tk, Q_PER_KV * tq), F32),
            pltpu.VMEM((tk, Q_PER_KV * tq), F32),
        ],
        compiler_params=_cparams("parallel", "parallel", "parallel"),
        name="attention",
    )(qt, k, vt)


def _out_proj_kernel(x_ref, ys_ref, u_ref, at_ref, dsk_ref, wglu_ref, bglu_ref, gs_ref, ga_ref,
                     wo_ref, gf_ref, xo_ref, xn_ref):
    y = ys_ref[...] + dsk_ref[...] * u_ref[...]
    y = jax.nn.gelu(y)
    z = jnp.dot(y.astype(BF16), wglu_ref[...], preferred_element_type=F32) + bglu_ref[...]
    y = y * jax.nn.sigmoid(z)
    ys = _rms(y, gs_ref[...]).astype(BF16)
    ya = _rms(at_ref[...], ga_ref[...]).astype(BF16)
    xo = (x_ref[...]
          + jnp.dot(ys, wo_ref[:D_SSM, :], preferred_element_type=F32)
          + jnp.dot(ya, wo_ref[D_SSM:, :], preferred_element_type=F32))
    xo_ref[...] = xo
    xn_ref[...] = _rms(xo, gf_ref[...]).astype(BF16)


def _out_proj(x, ys, u, at, dsk, wglu, bglu, gs, ga, wo, gf, tm):
    t = x.shape[0]
    row = lambda w: pl.BlockSpec((tm, w), lambda i: (i, 0))
    full = lambda a, b: pl.BlockSpec((a, b), lambda i: (0, 0))
    return pl.pallas_call(
        _out_proj_kernel,
        grid=(t // tm,),
        in_specs=[row(D_MODEL), row(D_SSM), row(D_SSM), row(D_ATTN), full(1, D_SSM),
                  full(D_SSM, D_SSM), full(1, D_SSM), full(1, D_SSM), full(1, D_ATTN),
                  full(D_MODEL, D_MODEL), full(1, D_MODEL)],
        out_specs=[row(D_MODEL), row(D_MODEL)],
        out_shape=[jax.ShapeDtypeStruct((t, D_MODEL), F32), jax.ShapeDtypeStruct((t, D_MODEL), BF16)],
        compiler_params=_cparams("parallel"),
        name="out_proj",
    )(x, ys, u, at, dsk, wglu, bglu, gs, ga, wo, gf)


def _oe_sort_network(n):
    pairs = []
    p = 1
    while p < n:
        k = p
        while k >= 1:
            for j in range(k % p, n - k, 2 * k):
                for i in range(min(k, n - j - k)):
                    if (i + j) // (2 * p) == (i + j + k) // (2 * p):
                        pairs.append((i + j, i + j + k))
            k //= 2
        p *= 2
    return pairs


def _ce(a, b):
    if b is None:
        return a, None
    if a is None:
        return b, None
    return jnp.maximum(a, b), jnp.minimum(a, b)


def _sorted_top16(slabs):
    n = PEER_TOPK
    v = list(slabs) + [None] * (n - len(slabs))
    for i, j in _oe_sort_network(n):
        v[i], v[j] = _ce(v[i], v[j])
    for shift in (4, 2, 1):
        partner = [None if x is None else pltpu.roll(x, shift, 0) for x in v]
        top = []
        for a in range(n):
            x, y = v[a], partner[n - 1 - a]
            top.append(y if x is None else (x if y is None else jnp.maximum(x, y)))
        v = top
        d = n // 2
        while d >= 1:
            for i in range(n):
                if i & d == 0:
                    v[i], v[i + d] = _ce(v[i], v[i + d])
            d //= 2
    return v


_CAND = [(a, b) for a in range(PEER_TOPK) for b in range(PEER_TOPK) if (a + 1) * (b + 1) <= PEER_TOPK]


def _count_prefix(pred, vals):
    pick = lambda m, a, b: jnp.where(m, a, b)
    m8 = pred(vals[7])
    m4 = pred(pick(m8, vals[11], vals[3]))
    m2 = pred(pick(m8, pick(m4, vals[13], vals[9]), pick(m4, vals[5], vals[1])))
    m1 = pred(pick(m8, pick(m4, pick(m2, vals[14], vals[12]), pick(m2, vals[10], vals[8])),
                   pick(m4, pick(m2, vals[6], vals[4]), pick(m2, vals[2], vals[0]))))
    cnt = pick(m8, 8.0, 0.0) + pick(m4, 4.0, 0.0) + pick(m2, 2.0, 0.0) + pick(m1, 1.0, 0.0)
    return cnt + pick(pred(vals[15]), 1.0, 0.0)


def _bf16_pair_words(x):
    b = pltpu.bitcast(x, jnp.uint32)
    b = (b + jnp.uint32(0x7FFF) + ((b >> 16) & jnp.uint32(1))) >> 16
    return b | (b << 16)


def _peer_route_kernel(xn_ref, wq_ref, k1_ref, k2_ref, rho_ref, e2_ref, r1_ref, c1_ref):
    nt = (((1,), (1,)), ((), ()))

    def scores(h):
        qt = lax.dot_general(wq_ref[h * PEER_KEY_DIM:(h + 1) * PEER_KEY_DIM, :], xn_ref[...], nt,
                             preferred_element_type=F32)
        return (jnp.dot(k1_ref[h], qt[:PEER_HALF].astype(BF16), preferred_element_type=F32),
                jnp.dot(k2_ref[h], qt[PEER_HALF:].astype(BF16), preferred_element_type=F32))

    s = scores(0)
    for h in range(PEER_HEADS):
        s_next = scores(h + 1) if h + 1 < PEER_HEADS else None
        _peer_route_head(h, s[0], s[1], rho_ref, e2_ref, r1_ref, c1_ref)
        s = s_next


def _peer_route_head(h, s1, s2, rho_ref, e2_ref, r1_ref, c1_ref):
    tm = s1.shape[1]
    nslab = N_KEYS // 8
    s1 = [s1[8 * i:8 * i + 8] for i in range(nslab)]
    s2 = [s2[8 * i:8 * i + 8] for i in range(nslab)]
    v1 = _sorted_top16(s1)
    v2 = _sorted_top16(s2)

    sub = lax.broadcasted_iota(jnp.int32, (8, tm), 0)
    cand = [v1[a] + v2[b] for a, b in _CAND]
    packed = []
    for base in range(0, len(cand), 8):
        slab = None
        for s, c in enumerate(cand[base:base + 8]):
            slab = c if slab is None else jnp.where(sub == s, c, slab)
        if len(cand) - base < 8:
            slab = jnp.where(sub < len(cand) - base, slab, -jnp.inf)
        packed.append(slab)
    theta = _sorted_top16(packed)[PEER_TOPK - 1]

    top = cand[0]
    z = jnp.zeros((8, tm), F32)
    for c in cand:
        z = z + jnp.where(c >= theta, jnp.exp(c - top), 0.0)
    inv_z = 1.0 / z

    r1, c1, rho, e2 = [], [], [], []
    for i in range(nslab):
        r1.append(_count_prefix(lambda v, x=s1[i]: x + v >= theta, v2))
        rho.append(_count_prefix(lambda v, x=s2[i]: v > x, v2))
        c1.append(jnp.exp(s1[i] - v1[0]) * inv_z)
        e2.append(jnp.exp(s2[i] - v2[0]))
    r1 = _bf16_pair_words(jnp.concatenate(r1, axis=0))
    c1 = _bf16_pair_words(jnp.concatenate(c1, axis=0))
    rho = jnp.concatenate(rho, axis=0).astype(BF16)
    e2 = jnp.concatenate(e2, axis=0).astype(BF16)
    for l in range(tm // 128):
        lanes = slice(l * 128, (l + 1) * 128)
        r1_ref[h, l] = r1[:, lanes]
        c1_ref[h, l] = c1[:, lanes]
    for l in range(tm // 256):
        lanes = slice(l * 256, (l + 1) * 256)
        rho_ref[h, l * N_KEYS:(l + 1) * N_KEYS, :] = rho[:, lanes]
        e2_ref[h, l * N_KEYS:(l + 1) * N_KEYS, :] = e2[:, lanes]


def _peer_route(xn, wq_t, k1, k2, tm):
    t = xn.shape[0]
    sel = pl.BlockSpec((PEER_HEADS, tm // 128, N_KEYS, 128), lambda i: (0, i, 0, 0))
    shp = jax.ShapeDtypeStruct((PEER_HEADS, t // 128, N_KEYS, 128), jnp.uint32)
    tok = pl.BlockSpec((PEER_HEADS, tm // 2, 256), lambda i: (0, i, 0))
    shp16 = jax.ShapeDtypeStruct((PEER_HEADS, t // 2, 256), BF16)
    return pl.pallas_call(
        _peer_route_kernel,
        grid=(t // tm,),
        in_specs=[
            pl.BlockSpec((tm, D_MODEL), lambda i: (i, 0)),
            pl.BlockSpec((PEER_HEADS * PEER_KEY_DIM, D_MODEL), lambda i: (0, 0)),
            pl.BlockSpec((PEER_HEADS, N_KEYS, PEER_HALF), lambda i: (0, 0, 0)),
            pl.BlockSpec((PEER_HEADS, N_KEYS, PEER_HALF), lambda i: (0, 0, 0)),
        ],
        out_specs=[tok, tok, sel, sel],
        out_shape=[shp16, shp16, shp, shp],
        compiler_params=_cparams("parallel"),
        name="peer_route",
    )(xn, wq_t, k1, k2)


def _gelu_tanh(x):
    k1 = -2.0 * math.sqrt(2.0 / math.pi) * LOG2_E
    t = x * (k1 + (k1 * 0.044715) * (x * x))
    return x / (1.0 + jnp.exp2(t))


def _load_row_repeated(ref, h, l, i, n):
    return ref[h, l, pl.ds(i, n, stride=0), :]


def _peer_dense_kernel(xn_ref, u_ref, vt_ref, rho_ref, e2_ref, r1_ref, c1_ref, o_ref, a_ref, *, rows):
    e = pl.program_id(1)

    @pl.when(e == 0)
    def _():
        o_ref[...] = jnp.zeros(o_ref.shape, F32)

    tt = xn_ref.shape[1]
    sub = 16
    strip = PEER_STRIP
    zero = jnp.zeros((sub, strip), BF16)

    def row_tile(ref, h, s, i):
        return jnp.concatenate(
            [pltpu.bitcast(_load_row_repeated(ref, h, s * (strip // 128) + l, i, sub // 2), BF16)
             for l in range(strip // 128)], axis=1)

    blk = PEER_BLOCK_ROWS * N_KEYS
    nblk = rows // PEER_BLOCK_ROWS

    def first_matmul(k):
        return jnp.dot(u_ref[k * blk:(k + 1) * blk, :], xn_ref[...], preferred_element_type=F32)

    act = first_matmul(0)
    for k in range(nblk):
        act_next = first_matmul(k + 1) if k + 1 < nblk else None
        a_ref[k * blk:(k + 1) * blk, :] = _gelu_tanh(act).astype(BF16)
        for s in range(tt // strip):
            lanes = slice(s * strip, (s + 1) * strip)
            for ii in range(PEER_BLOCK_ROWS):
                i = k * PEER_BLOCK_ROWS + ii
                rc = [(row_tile(r1_ref, h, s, i), row_tile(c1_ref, h, s, i)) for h in range(PEER_HEADS)]
                for jb in range(N_KEYS // sub):
                    keys = slice(s * N_KEYS + jb * sub, s * N_KEYS + (jb + 1) * sub)
                    w = None
                    for h in range(PEER_HEADS):
                        r, c = rc[h]
                        term = c * jnp.where(rho_ref[h, keys, :] < r, e2_ref[h, keys, :], zero)
                        w = term if w is None else w + term
                    lo = k * blk + ii * N_KEYS + jb * sub
                    a_ref[lo:lo + sub, lanes] = a_ref[lo:lo + sub, lanes] * w
        o_ref[...] += jnp.dot(vt_ref[:, k * blk:(k + 1) * blk], a_ref[k * blk:(k + 1) * blk, :],
                              preferred_element_type=F32)
        act = act_next


def _peer_dense(xn_t, eu, evt, rho, e2, r1, c1, tt, te):
    t = xn_t.shape[1]
    ne = eu.shape[0]
    rows = te // N_KEYS
    tok = pl.BlockSpec((PEER_HEADS, tt // 2, 256), lambda j, e: (0, j, 0))
    sel = pl.BlockSpec((PEER_HEADS, tt // 128, rows, 128), lambda j, e: (0, j, e, 0))
    return pl.pallas_call(
        functools.partial(_peer_dense_kernel, rows=rows),
        grid=(t // tt, ne // te),
        in_specs=[
            pl.BlockSpec((D_MODEL, tt), lambda j, e: (0, j)),
            pl.BlockSpec((te, D_MODEL), lambda j, e: (e, 0)),
            pl.BlockSpec((D_MODEL, te), lambda j, e: (0, e)),
            tok, tok, sel, sel,
        ],
        out_specs=pl.BlockSpec((D_MODEL, tt), lambda j, e: (0, j)),
        out_shape=jax.ShapeDtypeStruct((D_MODEL, t), F32),
        scratch_shapes=[pltpu.VMEM((te, tt), BF16)],
        compiler_params=_cparams("parallel", "arbitrary"),
        name="peer_dense",
    )(xn_t, eu, evt, rho, e2, r1, c1)


def _ple_kernel(x_ref, pe_ref, p_ref, g_ref, wg_ref, wp_ref, gfin_ref, o_ref, *, final):
    x = x_ref[...] + pe_ref[...]
    gate = jax.nn.sigmoid(jnp.dot(_rms(x, g_ref[...]).astype(BF16), wg_ref[...], preferred_element_type=F32))
    x = x + gate * jnp.dot(p_ref[...].astype(BF16), wp_ref[...], preferred_element_type=F32)
    if final:
        x = _rms(x, gfin_ref[...])
    o_ref[...] = x


def _ple(x, pe, p, g, wg, wp, gfin, final, tm):
    t = x.shape[0]
    row = lambda w: pl.BlockSpec((tm, w), lambda i: (i, 0))
    full = lambda a, b: pl.BlockSpec((a, b), lambda i: (0, 0))
    return pl.pallas_call(
        functools.partial(_ple_kernel, final=final),
        grid=(t // tm,),
        in_specs=[row(D_MODEL), row(D_MODEL), row(D_PLE), full(1, D_MODEL), full(D_MODEL, D_MODEL),
                  full(D_PLE, D_MODEL), full(1, D_MODEL)],
        out_specs=row(D_MODEL),
        out_shape=jax.ShapeDtypeStruct((t, D_MODEL), F32),
        compiler_params=_cparams("parallel"),
        name="ple",
    )(x, pe, p, g, wg, wp, gfin)


def _rope_tables(seq):
    n_rows = seq // GRID_W
    row = jnp.repeat(jnp.arange(n_rows, dtype=jnp.int32), GRID_W).astype(F32)
    col = jnp.tile(jnp.arange(GRID_W, dtype=jnp.int32), n_rows).astype(F32)
    quarter = HEAD_DIM // 4
    inv = ROPE_THETA ** (-jnp.arange(quarter, dtype=F32) / quarter)
    ang_r = row[:, None] * inv[None, :]
    ang_c = col[:, None] * inv[None, :]
    cos = jnp.concatenate([jnp.cos(ang_r)] * 2 + [jnp.cos(ang_c)] * 2, axis=1)
    sin = jnp.concatenate([-jnp.sin(ang_r), jnp.sin(ang_r), -jnp.sin(ang_c), jnp.sin(ang_c)], axis=1)
    return jnp.tile(cos, (1, N_Q_HEADS)), jnp.tile(sin, (1, N_Q_HEADS))


def _forward(x, p, norm_mix_g, w_in, lam_re, lam_im, log_dt, b_re, b_im, c_re, c_im, d_skip, w_glu,
             b_glu, q_norm_g, k_norm_g, ssm_out_g, attn_out_g, w_out, norm_ffn_g, w_query, sub_keys1,
             sub_keys2, expert_u, expert_v, norm_ple_g, w_ple_gate, w_ple_proj, final_norm_g):
    n_batch, seq, _ = x.shape
    depth = w_in.shape[0]
    t = n_batch * seq
    tm = min(256, seq)
    tq = min(512, seq)
    tk = min(512, seq // 2)
    tt = min(512, t)
    te = 2048
    nc = t // SSM_CHUNK

    cq, sq = _rope_tables(seq)
    bd = jnp.kron(jnp.eye(N_Q_HEADS, dtype=F32), jnp.ones((HEAD_DIM, HEAD_DIM), F32)).astype(BF16)
    row = lambda a: a.reshape(1, -1).astype(F32)

    xf = x.reshape(t, D_MODEL)
    for i in range(depth):
        u, qt, k, vt = _in_proj(xf, row(norm_mix_g[i]), w_in[i].astype(BF16), cq, sq,
                                row(jnp.tile(q_norm_g[i], N_Q_HEADS)), row(jnp.tile(k_norm_g[i], N_KV_HEADS)),
                                bd, seq, tk)

        grp = lambda a: jnp.transpose(a, (1, 0, 2))[:, :, None, :]
        ldt = jnp.broadcast_to(log_dt[i][:, :, None], (2, SSM_G, SSM_N))
        mt, et, fm, dec = _ssm_params(
            grp(lam_re[i]), grp(lam_im[i]), grp(ldt),
            jnp.transpose(b_re[i], (1, 0, 3, 2)), jnp.transpose(b_im[i], (1, 0, 3, 2)),
            jnp.transpose(c_re[i], (1, 0, 2, 3)), jnp.transpose(c_im[i], (1, 0, 2, 3)))
        ug = (u.astype(BF16).reshape(nc, SSM_CHUNK, SSM_G, SSM_P)
              .transpose(2, 0, 1, 3).reshape(SSM_G, nc, SSM_LP))
        yg = _ssm_conv(ug, mt, et, fm, dec, n_batch)
        y_ssm = yg.reshape(SSM_G, nc, SSM_CHUNK, SSM_P).transpose(1, 2, 0, 3).reshape(t, D_SSM)

        attn = _attention(qt, k, vt, n_batch, seq, tq)

        xf, xn = _out_proj(xf, y_ssm, u, attn, row(d_skip[i]), w_glu[i].astype(BF16), row(b_glu[i]),
                           row(ssm_out_g[i]), row(attn_out_g[i]), w_out[i].astype(BF16),
                           row(norm_ffn_g[i]), tm)

        rho, e2, r1, c1 = _peer_route(xn, w_query[i].T.astype(BF16), sub_keys1[i].astype(BF16),
                                      sub_keys2[i].astype(BF16), min(256, t))
        peer_t = _peer_dense(xn.T, expert_u[i].astype(BF16), expert_v[i].T.astype(BF16),
                             rho, e2, r1, c1, tt, te)

        xf = _ple(xf, peer_t.T, p[i].reshape(t, D_PLE), row(norm_ple_g[i]), w_ple_gate[i].astype(BF16),
                  w_ple_proj[i].astype(BF16), row(final_norm_g), i == depth - 1, tm)
    return xf.reshape(n_batch, seq, D_MODEL)


def kernel(x, p, norm_mix_g, w_in, lam_re, lam_im, log_dt, b_re, b_im, c_re, c_im, d_skip, w_glu, b_glu, q_norm_g, k_norm_g, ssm_out_g, attn_out_g, w_out, norm_ffn_g, w_query, sub_keys1, sub_keys2, expert_u, expert_v, norm_ple_g, w_ple_gate, w_ple_proj, final_norm_g):
    return _forward(x, p, norm_mix_g, w_in, lam_re, lam_im, log_dt, b_re, b_im, c_re, c_im, d_skip, w_glu,
                    b_glu, q_norm_g, k_norm_g, ssm_out_g, attn_out_g, w_out, norm_ffn_g, w_query,
                    sub_keys1, sub_keys2, expert_u, expert_v, norm_ple_g, w_ple_gate, w_ple_proj,
                    final_norm_g)
```

```python
import functools
import math

import jax
import jax.numpy as jnp
from jax import lax
from jax.experimental import pallas as pl
from jax.experimental.pallas import tpu as pltpu

F32 = jnp.float32
BF16 = jnp.bfloat16

D_MODEL = 1024
D_PLE = 256
GRID_W = 64
EPS = 1e-6
LOG2_E = 1.4426950408889634
D_SSM = 512
SSM_P = 16
SSM_G = D_SSM // SSM_P
SSM_N = 64
D_ATTN = 512
HEAD_DIM = 64
V_ROWS = HEAD_DIM + 16
N_Q_HEADS = 8
N_KV_HEADS = 2
Q_PER_KV = 4
D_KV = 128
D_IN = D_SSM + D_ATTN + 2 * D_KV
ROPE_THETA = 10000.0
PEER_HEADS = 8
PEER_KEY_DIM = 256
PEER_HALF = 128
N_KEYS = 128
PEER_TOPK = 16

PEER_BLOCK_ROWS = 2
PEER_STRIP = 256
SSM_CHUNK = 32
SSM_LP = SSM_CHUNK * SSM_P
VMEM_LIMIT = 48 * 1024 * 1024

def _cparams(*sem):
    return pltpu.CompilerParams(dimension_semantics=sem, vmem_limit_bytes=VMEM_LIMIT)


def _rms(x, g):
    return x * lax.rsqrt(jnp.mean(x * x, axis=-1, keepdims=True) + EPS) * g


def _in_proj_kernel(x_ref, g_ref, w_ref, cq_ref, sq_ref, gq_ref, gk_ref, bd_ref,
                    u_ref, qt_ref, k_ref, vt_ref):
    h = _rms(x_ref[...], g_ref[...])
    z = jnp.dot(h.astype(BF16), w_ref[...], preferred_element_type=F32)
    u_ref[...] = z[:, :D_SSM]

    def head_norm_rope(t, gain, bd, c, s):
        ss = jnp.dot((t * t).astype(BF16), bd, preferred_element_type=F32)
        t = t * lax.rsqrt(ss * (1.0 / HEAD_DIM) + EPS) * gain
        w = t.shape[1]
        lane = lax.broadcasted_iota(jnp.int32, t.shape, 1)
        partner = jnp.where((lane % 32) < 16, pltpu.roll(t, w - 16, 1), pltpu.roll(t, 16, 1))
        return t * c + partner * s

    cq = cq_ref[...]
    sq = sq_ref[...]
    q = head_norm_rope(z[:, D_SSM:D_SSM + D_ATTN], gq_ref[...], bd_ref[...], cq, sq)
    qt_ref[...] = (q * (HEAD_DIM ** -0.5 * LOG2_E)).T.astype(BF16)
    k = head_norm_rope(z[:, D_SSM + D_ATTN:D_SSM + D_ATTN + D_KV], gk_ref[...],
                       bd_ref[:D_KV, :D_KV], cq[:, :D_KV], sq[:, :D_KV])
    vt = z[:, D_SSM + D_ATTN + D_KV:].T
    row = lax.broadcasted_iota(jnp.int32, (V_ROWS - HEAD_DIM, vt.shape[1]), 0)
    ones_row = jnp.where(row == 0, 1.0, 0.0).astype(F32)
    for j in range(N_KV_HEADS):
        k_ref[j] = k[:, j * HEAD_DIM:(j + 1) * HEAD_DIM].astype(BF16)
        vt_ref[j, 0] = jnp.concatenate([vt[j * HEAD_DIM:(j + 1) * HEAD_DIM], ones_row], axis=0).astype(BF16)


def _in_proj(x, g, w, cq, sq, gq, gk, bd, seq, tm):
    t = x.shape[0]
    nseq = seq // tm
    return pl.pallas_call(
        _in_proj_kernel,
        grid=(t // tm,),
        in_specs=[
            pl.BlockSpec((tm, D_MODEL), lambda i: (i, 0)),
            pl.BlockSpec((1, D_MODEL), lambda i: (0, 0)),
            pl.BlockSpec((D_MODEL, D_IN), lambda i: (0, 0)),
            pl.BlockSpec((tm, D_ATTN), lambda i: (i % nseq, 0)),
            pl.BlockSpec((tm, D_ATTN), lambda i: (i % nseq, 0)),
            pl.BlockSpec((1, D_ATTN), lambda i: (0, 0)),
            pl.BlockSpec((1, D_KV), lambda i: (0, 0)),
            pl.BlockSpec((D_ATTN, D_ATTN), lambda i: (0, 0)),
        ],
        out_specs=[
            pl.BlockSpec((tm, D_SSM), lambda i: (i, 0)),
            pl.BlockSpec((D_ATTN, tm), lambda i: (0, i)),
            pl.BlockSpec((N_KV_HEADS, tm, HEAD_DIM), lambda i: (0, i, 0)),
            pl.BlockSpec((N_KV_HEADS, 1, V_ROWS, tm), lambda i: (0, i, 0, 0)),
        ],
        out_shape=[
            jax.ShapeDtypeStruct((t, D_SSM), F32),
            jax.ShapeDtypeStruct((D_ATTN, t), BF16),
            jax.ShapeDtypeStruct((N_KV_HEADS, t, HEAD_DIM), BF16),
            jax.ShapeDtypeStruct((N_KV_HEADS, t // tm, V_ROWS, tm), BF16),
        ],
        compiler_params=_cparams("parallel"),
        name="in_proj",
    )(x, g, w, cq, sq, gq, gk, bd)


def _ssm_params_kernel(lr_ref, li_ref, ldt_ref, bt_re_ref, bt_im_ref, c_re_ref, c_im_ref,
                       mt_ref, et_ref, fm_ref, dec_ref):
    L = SSM_CHUNK
    hi = lax.Precision.HIGHEST
    r_parts, e_parts, f_parts, p_parts, q_parts = [], [], [], [], []
    idx = lax.broadcasted_iota(jnp.int32, (L, 1, SSM_N), 0).astype(F32)
    for d in range(2):
        lr = lr_ref[0, d]
        li = li_ref[0, d]
        dt = jnp.exp(ldt_ref[0, d])

        def apow(tau):
            mag = jnp.exp(tau * (lr * dt))
            ang = tau * (li * dt)
            return mag * jnp.cos(ang), mag * jnp.sin(ang)

        ab_re, ab_im = apow(jnp.ones((1, SSM_N), F32))
        den = lr * lr + li * li
        nr = ab_re - 1.0
        f_re = (nr * lr + ab_im * li) / den
        f_im = (ab_im * lr - nr * li) / den
        bt_re = bt_re_ref[0, d]
        bt_im = bt_im_ref[0, d]
        bb_re = f_re * bt_re - f_im * bt_im
        bb_im = f_re * bt_im + f_im * bt_re
        c_re = c_re_ref[0, d]
        c_im = c_im_ref[0, d]

        tau_k = idx if d == 0 else (L - 1.0) - idx
        pr, pi = apow(tau_k)
        lm_re = (c_re[None] * pr - c_im[None] * pi).reshape(L * SSM_P, SSM_N)
        lm_im = (c_re[None] * pi + c_im[None] * pr).reshape(L * SSM_P, SSM_N)
        nt = (((1,), (1,)), ((), ()))
        r_parts.append(lax.dot_general(bb_re, lm_re, nt, precision=hi, preferred_element_type=F32)
                       - lax.dot_general(bb_im, lm_im, nt, precision=hi, preferred_element_type=F32))

        tau_e = (L - 1.0) - idx if d == 0 else idx
        pr, pi = apow(tau_e)
        e_parts.append((pr * bb_re[None] - pi * bb_im[None]).reshape(L * SSM_P, SSM_N))
        e_parts.append((pr * bb_im[None] + pi * bb_re[None]).reshape(L * SSM_P, SSM_N))

        tau_f = idx + 1.0 if d == 0 else L - idx
        pr, pi = apow(tau_f)
        f_parts.append((c_re[None] * pr - c_im[None] * pi).reshape(L * SSM_P, SSM_N))
        f_parts.append(-(c_re[None] * pi + c_im[None] * pr).reshape(L * SSM_P, SSM_N))

        al_re, al_im = apow(jnp.full((1, SSM_N), float(L), F32))
        p_parts += [al_re, al_re]
        q_parts += [-al_im, al_im]

    zeros = jnp.zeros((SSM_P, SSM_LP), F32)
    r_bwd = jnp.concatenate([r_parts[1], zeros], axis=1)
    r_fwd = pltpu.roll(jnp.concatenate([r_parts[0], zeros], axis=1), SSM_LP - SSM_P, 1)
    r_cat = r_bwd + r_fwd
    for s in range(L):
        shift = (2 * SSM_LP - (L - 1 - s) * SSM_P) % (2 * SSM_LP)
        win = r_cat if shift == 0 else pltpu.roll(r_cat, shift, 1)
        mt_ref[0, s * SSM_P:(s + 1) * SSM_P, :] = win[:, :SSM_LP].astype(BF16)
    et_ref[0] = jnp.concatenate(e_parts, axis=1).astype(BF16)
    fm_ref[0] = jnp.concatenate(f_parts, axis=1).astype(BF16)
    dec_ref[0] = jnp.concatenate([jnp.concatenate(p_parts, axis=1),
                                  jnp.concatenate(q_parts, axis=1)], axis=0)


def _ssm_params(lr, li, ldt, bt_re, bt_im, c_re, c_im):
    g = lr.shape[0]
    vec = pl.BlockSpec((1, 2, 1, SSM_N), lambda i: (i, 0, 0, 0))
    mat = pl.BlockSpec((1, 2, SSM_P, SSM_N), lambda i: (i, 0, 0, 0))
    return pl.pallas_call(
        _ssm_params_kernel,
        grid=(g,),
        in_specs=[vec, vec, vec, mat, mat, mat, mat],
        out_specs=[
            pl.BlockSpec((1, SSM_LP, SSM_LP), lambda i: (i, 0, 0)),
            pl.BlockSpec((1, SSM_LP, 4 * SSM_N), lambda i: (i, 0, 0)),
            pl.BlockSpec((1, SSM_LP, 4 * SSM_N), lambda i: (i, 0, 0)),
            pl.BlockSpec((1, 2, 4 * SSM_N), lambda i: (i, 0, 0)),
        ],
        out_shape=[
            jax.ShapeDtypeStruct((g, SSM_LP, SSM_LP), BF16),
            jax.ShapeDtypeStruct((g, SSM_LP, 4 * SSM_N), BF16),
            jax.ShapeDtypeStruct((g, SSM_LP, 4 * SSM_N), BF16),
            jax.ShapeDtypeStruct((g, 2, 4 * SSM_N), F32),
        ],
        compiler_params=_cparams("parallel"),
        name="ssm_params",
    )(lr, li, ldt, bt_re, bt_im, c_re, c_im)


def _ssm_conv_kernel(u_ref, mt_ref, et_ref, fm_ref, dec_ref, y_ref, *, n_batch):
    u = u_ref[0]
    nc = u.shape[0]
    ncb = nc // n_batch
    w = 2 * SSM_N
    s_all = jnp.dot(u, et_ref[0], preferred_element_type=F32)
    dec = dec_ref[0]
    row = lax.broadcasted_iota(jnp.int32, (ncb, w), 0)

    def cmul(x, p, q):
        return p * x + q * pltpu.roll(x, SSM_N, 1)

    h_rows = []
    for b in range(n_batch):
        s_b = s_all[b * ncb:(b + 1) * ncb]
        xf, xb = s_b[:, :w], s_b[:, w:]
        pf, qf = dec[0:1, :w], dec[1:2, :w]
        pb, qb = dec[0:1, w:], dec[1:2, w:]
        k = 1
        while k < ncb:
            xf = xf + jnp.where(row >= k, cmul(pltpu.roll(xf, k, 0), pf, qf), 0.0)
            xb = xb + jnp.where(row < ncb - k, cmul(pltpu.roll(xb, ncb - k, 0), pb, qb), 0.0)
            pf, qf = pf * pf - qf * qf, 2.0 * pf * qf
            pb, qb = pb * pb - qb * qb, 2.0 * pb * qb
            k *= 2
        hf = jnp.where(row >= 1, pltpu.roll(xf, 1, 0), 0.0)
        hb = jnp.where(row < ncb - 1, pltpu.roll(xb, ncb - 1, 0), 0.0)
        h_rows.append(jnp.concatenate([hf, hb], axis=1))
    h = jnp.concatenate(h_rows, axis=0).astype(BF16)
    y = jnp.dot(u, mt_ref[0], preferred_element_type=F32)
    y = y + lax.dot_general(h, fm_ref[0], (((1,), (1,)), ((), ())), preferred_element_type=F32)
    y_ref[0] = y


def _ssm_conv(ug, mt, et, fm, dec, n_batch):
    g, nc, lp = ug.shape
    return pl.pallas_call(
        functools.partial(_ssm_conv_kernel, n_batch=n_batch),
        grid=(g,),
        in_specs=[
            pl.BlockSpec((1, nc, lp), lambda i: (i, 0, 0)),
            pl.BlockSpec((1, lp, lp), lambda i: (i, 0, 0)),
            pl.BlockSpec((1, lp, 4 * SSM_N), lambda i: (i, 0, 0)),
            pl.BlockSpec((1, lp, 4 * SSM_N), lambda i: (i, 0, 0)),
            pl.BlockSpec((1, 2, 4 * SSM_N), lambda i: (i, 0, 0)),
        ],
        out_specs=pl.BlockSpec((1, nc, lp), lambda i: (i, 0, 0)),
        out_shape=jax.ShapeDtypeStruct((g, nc, lp), F32),
        compiler_params=_cparams("parallel"),
        name="ssm_conv",
    )(ug, mt, et, fm, dec)


def _attn_kernel(qt_ref, k_ref, vt_ref, o_ref, q_sc, m_ref, acc_ref, sa_ref, sb_ref, *, tq, tk, nk):
    qt = qt_ref[...]
    for j in range(Q_PER_KV):
        q_sc[:, j * tq:(j + 1) * tq] = qt[j * HEAD_DIM:(j + 1) * HEAD_DIM, :]
    m_ref[...] = jnp.full(m_ref.shape, -jnp.inf, F32)
    acc_ref[...] = jnp.zeros(acc_ref.shape, F32)

    def scores(i):
        kb = k_ref[0, pl.ds(pl.multiple_of(i * tk, tk), tk), :]
        return jnp.dot(kb, q_sc[...], preferred_element_type=F32)

    def accumulate(s_ref, i):
        s = s_ref[...]
        m_prev = m_ref[...]
        m_new = jnp.maximum(m_prev, jnp.max(s, axis=0, keepdims=True))
        alpha = jnp.exp2(m_prev - m_new)
        p = jnp.exp2(s - m_new[0:1, :])
        acc_ref[:V_ROWS, :] = (acc_ref[:V_ROWS, :] * alpha[0:1, :]
                               + jnp.dot(vt_ref[0, i], p.astype(BF16), preferred_element_type=F32))
        m_ref[...] = m_new

    sa_ref[...] = scores(0)

    def body(j, carry):
        sb_ref[...] = scores(2 * j + 1)
        accumulate(sa_ref, 2 * j)
        sa_ref[...] = scores(2 * j + 2)
        accumulate(sb_ref, 2 * j + 1)
        return carry

    lax.fori_loop(0, nk // 2 - 1, body, 0)
    sb_ref[...] = scores(nk - 1)
    accumulate(sa_ref, nk - 2)
    accumulate(sb_ref, nk - 1)
    acc = acc_ref[...]
    out = (acc / acc[HEAD_DIM:HEAD_DIM + 1, :]).T
    o_ref[...] = jnp.concatenate([out[j * tq:(j + 1) * tq, :HEAD_DIM] for j in range(Q_PER_KV)], axis=1)


def _attention(qt, k, vt, n_batch, seq, tq):
    t = qt.shape[1]
    tk = vt.shape[3]
    nq = seq // tq
    nk = seq // tk
    assert nk % 2 == 0, "the key loop handles chunks in pairs"
    w = Q_PER_KV * HEAD_DIM
    return pl.pallas_call(
        functools.partial(_attn_kernel, tq=tq, tk=tk, nk=nk),
        grid=(n_batch, N_KV_HEADS, nq),
        in_specs=[
            pl.BlockSpec((w, tq), lambda b, g, i: (g, b * nq + i)),
            pl.BlockSpec((1, seq, HEAD_DIM), lambda b, g, i: (g, b, 0)),
            pl.BlockSpec((1, nk, V_ROWS, tk), lambda b, g, i: (g, b, 0, 0)),
        ],
        out_specs=pl.BlockSpec((tq, w), lambda b, g, i: (b * nq + i, g)),
        out_shape=jax.ShapeDtypeStruct((t, D_ATTN), F32),
        scratch_shapes=[
            pltpu.VMEM((HEAD_DIM, Q_PER_KV * tq), BF16),
            pltpu.VMEM((8, Q_PER_KV * tq), F32),
            pltpu.VMEM((2 * HEAD_DIM, Q_PER_KV * tq), F32),
            pltpu.VMEM((tk, Q_PER_KV * tq), F32),
            pltpu.VMEM((tk, Q_PER_KV * tq), F32),
        ],
        compiler_params=_cparams("parallel", "parallel", "parallel"),
        name="attention",
    )(qt, k, vt)


def _out_proj_kernel(x_ref, ys_ref, u_ref, at_ref, dsk_ref, wglu_ref, bglu_ref, gs_ref, ga_ref,
                     wo_ref, gf_ref, xo_ref, xnt_ref):
    y = ys_ref[...] + dsk_ref[...] * u_ref[...]
    y = jax.nn.gelu(y)
    z = jnp.dot(y.astype(BF16), wglu_ref[...], preferred_element_type=F32) + bglu_ref[...]
    y = y * jax.nn.sigmoid(z)
    ys = _rms(y, gs_ref[...]).astype(BF16)
    ya = _rms(at_ref[...], ga_ref[...]).astype(BF16)
    xo = (x_ref[...]
          + jnp.dot(ys, wo_ref[:D_SSM, :], preferred_element_type=F32)
          + jnp.dot(ya, wo_ref[D_SSM:, :], preferred_element_type=F32))
    xo_ref[...] = xo
    xnt_ref[...] = _rms(xo, gf_ref[...]).T.astype(BF16)


def _out_proj(x, ys, u, at, dsk, wglu, bglu, gs, ga, wo, gf, tm):
    t = x.shape[0]
    row = lambda w: pl.BlockSpec((tm, w), lambda i: (i, 0))
    full = lambda a, b: pl.BlockSpec((a, b), lambda i: (0, 0))
    return pl.pallas_call(
        _out_proj_kernel,
        grid=(t // tm,),
        in_specs=[row(D_MODEL), row(D_SSM), row(D_SSM), row(D_ATTN), full(1, D_SSM),
                  full(D_SSM, D_SSM), full(1, D_SSM), full(1, D_SSM), full(1, D_ATTN),
                  full(D_MODEL, D_MODEL), full(1, D_MODEL)],
        out_specs=[row(D_MODEL), pl.BlockSpec((D_MODEL, tm), lambda i: (0, i))],
        out_shape=[jax.ShapeDtypeStruct((t, D_MODEL), F32), jax.ShapeDtypeStruct((D_MODEL, t), BF16)],
        compiler_params=_cparams("parallel"),
        name="out_proj",
    )(x, ys, u, at, dsk, wglu, bglu, gs, ga, wo, gf)


def _oe_sort_network(n):
    pairs = []
    p = 1
    while p < n:
        k = p
        while k >= 1:
            for j in range(k % p, n - k, 2 * k):
                for i in range(min(k, n - j - k)):
                    if (i + j) // (2 * p) == (i + j + k) // (2 * p):
                        pairs.append((i + j, i + j + k))
            k //= 2
        p *= 2
    return pairs


def _ce(a, b):
    if b is None:
        return a, None
    if a is None:
        return b, None
    return jnp.maximum(a, b), jnp.minimum(a, b)


def _sorted_top16(slabs):
    n = PEER_TOPK
    v = list(slabs) + [None] * (n - len(slabs))
    for i, j in _oe_sort_network(n):
        v[i], v[j] = _ce(v[i], v[j])
    for shift in (4, 2, 1):
        partner = [None if x is None else pltpu.roll(x, shift, 0) for x in v]
        top = []
        for a in range(n):
            x, y = v[a], partner[n - 1 - a]
            top.append(y if x is None else (x if y is None else jnp.maximum(x, y)))
        v = top
        d = n // 2
        while d >= 1:
            for i in range(n):
                if i & d == 0:
                    v[i], v[i + d] = _ce(v[i], v[i + d])
            d //= 2
    return v


_CAND = [(a, b) for a in range(PEER_TOPK) for b in range(PEER_TOPK) if (a + 1) * (b + 1) <= PEER_TOPK]


def _count_prefix(pred, vals):
    pick = lambda m, a, b: jnp.where(m, a, b)
    m8 = pred(vals[7])
    m4 = pred(pick(m8, vals[11], vals[3]))
    m2 = pred(pick(m8, pick(m4, vals[13], vals[9]), pick(m4, vals[5], vals[1])))
    m1 = pred(pick(m8, pick(m4, pick(m2, vals[14], vals[12]), pick(m2, vals[10], vals[8])),
                   pick(m4, pick(m2, vals[6], vals[4]), pick(m2, vals[2], vals[0]))))
    cnt = pick(m8, 8.0, 0.0) + pick(m4, 4.0, 0.0) + pick(m2, 2.0, 0.0) + pick(m1, 1.0, 0.0)
    return cnt + pick(pred(vals[15]), 1.0, 0.0)


def _bf16_pair_words(x):
    b = pltpu.bitcast(x, jnp.uint32)
    b = (b + jnp.uint32(0x7FFF) + ((b >> 16) & jnp.uint32(1))) >> 16
    return b | (b << 16)


def _peer_route_kernel(xnt_ref, wq_ref, k1_ref, k2_ref, rho_ref, e2_ref, r1_ref, c1_ref):
    def scores(h):
        qt = jnp.dot(wq_ref[h * PEER_KEY_DIM:(h + 1) * PEER_KEY_DIM, :], xnt_ref[...],
                     preferred_element_type=F32)
        return (jnp.dot(k1_ref[h], qt[:PEER_HALF].astype(BF16), preferred_element_type=F32),
                jnp.dot(k2_ref[h], qt[PEER_HALF:].astype(BF16), preferred_element_type=F32))

    s = scores(0)
    for h in range(PEER_HEADS):
        s_next = scores(h + 1) if h + 1 < PEER_HEADS else None
        _peer_route_head(h, s[0], s[1], rho_ref, e2_ref, r1_ref, c1_ref)
        s = s_next


def _peer_route_head(h, s1, s2, rho_ref, e2_ref, r1_ref, c1_ref):
    tm = s1.shape[1]
    nslab = N_KEYS // 8
    s1 = [s1[8 * i:8 * i + 8] for i in range(nslab)]
    s2 = [s2[8 * i:8 * i + 8] for i in range(nslab)]
    v1 = _sorted_top16(s1)
    v2 = _sorted_top16(s2)

    sub = lax.broadcasted_iota(jnp.int32, (8, tm), 0)
    cand = [v1[a] + v2[b] for a, b in _CAND]
    packed = []
    for base in range(0, len(cand), 8):
        slab = None
        for s, c in enumerate(cand[base:base + 8]):
            slab = c if slab is None else jnp.where(sub == s, c, slab)
        if len(cand) - base < 8:
            slab = jnp.where(sub < len(cand) - base, slab, -jnp.inf)
        packed.append(slab)
    theta = _sorted_top16(packed)[PEER_TOPK - 1]

    top = cand[0]
    z = jnp.zeros((8, tm), F32)
    for c in cand:
        z = z + jnp.where(c >= theta, jnp.exp(c - top), 0.0)
    inv_z = 1.0 / z

    r1, c1, rho, e2 = [], [], [], []
    for i in range(nslab):
        r1.append(_count_prefix(lambda v, x=s1[i]: x + v >= theta, v2))
        rho.append(_count_prefix(lambda v, x=s2[i]: v > x, v2))
        c1.append(jnp.exp(s1[i] - v1[0]) * inv_z)
        e2.append(jnp.exp(s2[i] - v2[0]))
    r1 = _bf16_pair_words(jnp.concatenate(r1, axis=0))
    c1 = _bf16_pair_words(jnp.concatenate(c1, axis=0))
    rho = jnp.concatenate(rho, axis=0).astype(BF16)
    e2 = jnp.concatenate(e2, axis=0).astype(BF16)
    for l in range(tm // 128):
        lanes = slice(l * 128, (l + 1) * 128)
        r1_ref[h, l] = r1[:, lanes]
        c1_ref[h, l] = c1[:, lanes]
    for l in range(tm // 256):
        lanes = slice(l * 256, (l + 1) * 256)
        rho_ref[h, l * N_KEYS:(l + 1) * N_KEYS, :] = rho[:, lanes]
        e2_ref[h, l * N_KEYS:(l + 1) * N_KEYS, :] = e2[:, lanes]


def _peer_route(xn_t, wq_t, k1, k2, tm):
    t = xn_t.shape[1]
    sel = pl.BlockSpec((PEER_HEADS, tm // 128, N_KEYS, 128), lambda i: (0, i, 0, 0))
    shp = jax.ShapeDtypeStruct((PEER_HEADS, t // 128, N_KEYS, 128), jnp.uint32)
    tok = pl.BlockSpec((PEER_HEADS, tm // 2, 256), lambda i: (0, i, 0))
    shp16 = jax.ShapeDtypeStruct((PEER_HEADS, t // 2, 256), BF16)
    return pl.pallas_call(
        _peer_route_kernel,
        grid=(t // tm,),
        in_specs=[
            pl.BlockSpec((D_MODEL, tm), lambda i: (0, i)),
            pl.BlockSpec((PEER_HEADS * PEER_KEY_DIM, D_MODEL), lambda i: (0, 0)),
            pl.BlockSpec((PEER_HEADS, N_KEYS, PEER_HALF), lambda i: (0, 0, 0)),
            pl.BlockSpec((PEER_HEADS, N_KEYS, PEER_HALF), lambda i: (0, 0, 0)),
        ],
        out_specs=[tok, tok, sel, sel],
        out_shape=[shp16, shp16, shp, shp],
        compiler_params=_cparams("parallel"),
        name="peer_route",
    )(xn_t, wq_t, k1, k2)


def _gelu_tanh(x):
    k1 = -2.0 * math.sqrt(2.0 / math.pi) * LOG2_E
    t = x * (k1 + (k1 * 0.044715) * (x * x))
    return x / (1.0 + jnp.exp2(t))


def _load_row_repeated(ref, h, l, i, n):
    return ref[h, l, pl.ds(i, n, stride=0), :]


def _peer_dense_kernel(xn_ref, u_ref, vt_ref, rho_ref, e2_ref, r1_ref, c1_ref, o_ref, a_ref, *, rows):
    e = pl.program_id(1)

    @pl.when(e == 0)
    def _():
        o_ref[...] = jnp.zeros(o_ref.shape, F32)

    tt = xn_ref.shape[1]
    sub = 16
    strip = PEER_STRIP
    zero = jnp.zeros((sub, strip), BF16)

    def row_tile(ref, h, s, i):
        return jnp.concatenate(
            [pltpu.bitcast(_load_row_repeated(ref, h, s * (strip // 128) + l, i, sub // 2), BF16)
             for l in range(strip // 128)], axis=1)

    blk = PEER_BLOCK_ROWS * N_KEYS
    nblk = rows // PEER_BLOCK_ROWS

    def first_matmul(k):
        return jnp.dot(u_ref[k * blk:(k + 1) * blk, :].astype(BF16), xn_ref[...], preferred_element_type=F32)

    act = first_matmul(0)
    for k in range(nblk):
        act_next = first_matmul(k + 1) if k + 1 < nblk else None
        a_ref[k * blk:(k + 1) * blk, :] = _gelu_tanh(act).astype(BF16)
        for s in range(tt // strip):
            lanes = slice(s * strip, (s + 1) * strip)
            for ii in range(PEER_BLOCK_ROWS):
                i = k * PEER_BLOCK_ROWS + ii
                rc = [(row_tile(r1_ref, h, s, i), row_tile(c1_ref, h, s, i)) for h in range(PEER_HEADS)]
                for jb in range(N_KEYS // sub):
                    keys = slice(s * N_KEYS + jb * sub, s * N_KEYS + (jb + 1) * sub)
                    w = None
                    for h in range(PEER_HEADS):
                        r, c = rc[h]
                        term = c * jnp.where(rho_ref[h, keys, :] < r, e2_ref[h, keys, :], zero)
                        w = term if w is None else w + term
                    lo = k * blk + ii * N_KEYS + jb * sub
                    a_ref[lo:lo + sub, lanes] = a_ref[lo:lo + sub, lanes] * w
        o_ref[...] += jnp.dot(vt_ref[:, k * blk:(k + 1) * blk], a_ref[k * blk:(k + 1) * blk, :],
                              preferred_element_type=F32)
        act = act_next


def _peer_dense(xn_t, eu, evt, rho, e2, r1, c1, tt, te):
    t = xn_t.shape[1]
    ne = eu.shape[0]
    rows = te // N_KEYS
    tok = pl.BlockSpec((PEER_HEADS, tt // 2, 256), lambda j, e: (0, j, 0))
    sel = pl.BlockSpec((PEER_HEADS, tt // 128, rows, 128), lambda j, e: (0, j, e, 0))
    return pl.pallas_call(
        functools.partial(_peer_dense_kernel, rows=rows),
        grid=(t // tt, ne // te),
        in_specs=[
            pl.BlockSpec((D_MODEL, tt), lambda j, e: (0, j)),
            pl.BlockSpec((te, D_MODEL), lambda j, e: (e, 0)),
            pl.BlockSpec((D_MODEL, te), lambda j, e: (0, e)),
            tok, tok, sel, sel,
        ],
        out_specs=pl.BlockSpec((D_MODEL, tt), lambda j, e: (0, j)),
        out_shape=jax.ShapeDtypeStruct((D_MODEL, t), F32),
        scratch_shapes=[pltpu.VMEM((te, tt), BF16)],
        compiler_params=_cparams("parallel", "arbitrary"),
        name="peer_dense",
    )(xn_t, eu, evt, rho, e2, r1, c1)


def _ple_kernel(x_ref, pe_ref, p_ref, g_ref, wg_ref, wp_ref, gfin_ref, o_ref, *, final):
    x = x_ref[...] + pe_ref[...].T
    gate = jax.nn.sigmoid(jnp.dot(_rms(x, g_ref[...]).astype(BF16), wg_ref[...], preferred_element_type=F32))
    x = x + gate * jnp.dot(p_ref[...].astype(BF16), wp_ref[...], preferred_element_type=F32)
    if final:
        x = _rms(x, gfin_ref[...])
    o_ref[...] = x


def _ple(x, pe, p, g, wg, wp, gfin, final, tm):
    t = x.shape[0]
    row = lambda w: pl.BlockSpec((tm, w), lambda i: (i, 0))
    full = lambda a, b: pl.BlockSpec((a, b), lambda i: (0, 0))
    return pl.pallas_call(
        functools.partial(_ple_kernel, final=final),
        grid=(t // tm,),
        in_specs=[row(D_MODEL), pl.BlockSpec((D_MODEL, tm), lambda i: (0, i)), row(D_PLE), full(1, D_MODEL),
                  full(D_MODEL, D_MODEL), full(D_PLE, D_MODEL), full(1, D_MODEL)],
        out_specs=row(D_MODEL),
        out_shape=jax.ShapeDtypeStruct((t, D_MODEL), F32),
        compiler_params=_cparams("parallel"),
        name="ple",
    )(x, pe, p, g, wg, wp, gfin)


def _rope_tables(seq):
    n_rows = seq // GRID_W
    row = jnp.repeat(jnp.arange(n_rows, dtype=jnp.int32), GRID_W).astype(F32)
    col = jnp.tile(jnp.arange(GRID_W, dtype=jnp.int32), n_rows).astype(F32)
    quarter = HEAD_DIM // 4
    inv = ROPE_THETA ** (-jnp.arange(quarter, dtype=F32) / quarter)
    ang_r = row[:, None] * inv[None, :]
    ang_c = col[:, None] * inv[None, :]
    cos = jnp.concatenate([jnp.cos(ang_r)] * 2 + [jnp.cos(ang_c)] * 2, axis=1)
    sin = jnp.concatenate([-jnp.sin(ang_r), jnp.sin(ang_r), -jnp.sin(ang_c), jnp.sin(ang_c)], axis=1)
    return jnp.tile(cos, (1, N_Q_HEADS)), jnp.tile(sin, (1, N_Q_HEADS))


def _forward(x, p, norm_mix_g, w_in, lam_re, lam_im, log_dt, b_re, b_im, c_re, c_im, d_skip, w_glu,
             b_glu, q_norm_g, k_norm_g, ssm_out_g, attn_out_g, w_out, norm_ffn_g, w_query, sub_keys1,
             sub_keys2, expert_u, expert_v, norm_ple_g, w_ple_gate, w_ple_proj, final_norm_g):
    n_batch, seq, _ = x.shape
    depth = w_in.shape[0]
    t = n_batch * seq
    tm = min(512, seq)
    tq = min(512, seq)
    tk = min(512, seq // 2)
    tt = min(512, t)
    te = 2048
    nc = t // SSM_CHUNK

    cq, sq = _rope_tables(seq)
    bd = jnp.kron(jnp.eye(N_Q_HEADS, dtype=F32), jnp.ones((HEAD_DIM, HEAD_DIM), F32)).astype(BF16)
    row = lambda a: a.reshape(1, -1).astype(F32)

    xf = x.reshape(t, D_MODEL)
    for i in range(depth):
        u, qt, k, vt = _in_proj(xf, row(norm_mix_g[i]), w_in[i].astype(BF16), cq, sq,
                                row(jnp.tile(q_norm_g[i], N_Q_HEADS)), row(jnp.tile(k_norm_g[i], N_KV_HEADS)),
                                bd, seq, tk)

        grp = lambda a: jnp.transpose(a, (1, 0, 2))[:, :, None, :]
        ldt = jnp.broadcast_to(log_dt[i][:, :, None], (2, SSM_G, SSM_N))
        mt, et, fm, dec = _ssm_params(
            grp(lam_re[i]), grp(lam_im[i]), grp(ldt),
            jnp.transpose(b_re[i], (1, 0, 3, 2)), jnp.transpose(b_im[i], (1, 0, 3, 2)),
            jnp.transpose(c_re[i], (1, 0, 2, 3)), jnp.transpose(c_im[i], (1, 0, 2, 3)))
        ug = (u.astype(BF16).reshape(nc, SSM_CHUNK, SSM_G, SSM_P)
              .transpose(2, 0, 1, 3).reshape(SSM_G, nc, SSM_LP))
        yg = _ssm_conv(ug, mt, et, fm, dec, n_batch)
        y_ssm = yg.reshape(SSM_G, nc, SSM_CHUNK, SSM_P).transpose(1, 2, 0, 3).reshape(t, D_SSM)

        attn = _attention(qt, k, vt, n_batch, seq, tq)

        xf, xnt = _out_proj(xf, y_ssm, u, attn, row(d_skip[i]), w_glu[i].astype(BF16), row(b_glu[i]),
                            row(ssm_out_g[i]), row(attn_out_g[i]), w_out[i].astype(BF16),
                            row(norm_ffn_g[i]), tm)

        rho, e2, r1, c1 = _peer_route(xnt, w_query[i].T.astype(BF16), sub_keys1[i].astype(BF16),
                                      sub_keys2[i].astype(BF16), min(256, t))
        peer_t = _peer_dense(xnt, expert_u[i], expert_v[i].T.astype(BF16),
                             rho, e2, r1, c1, tt, te)

        xf = _ple(xf, peer_t, p[i].reshape(t, D_PLE), row(norm_ple_g[i]), w_ple_gate[i].astype(BF16),
                  w_ple_proj[i].astype(BF16), row(final_norm_g), i == depth - 1, tm)
    return xf.reshape(n_batch, seq, D_MODEL)


def kernel(x, p, norm_mix_g, w_in, lam_re, lam_im, log_dt, b_re, b_im, c_re, c_im, d_skip, w_glu, b_glu, q_norm_g, k_norm_g, ssm_out_g, attn_out_g, w_out, norm_ffn_g, w_query, sub_keys1, sub_keys2, expert_u, expert_v, norm_ple_g, w_ple_gate, w_ple_proj, final_norm_g):
    return _forward(x, p, norm_mix_g, w_in, lam_re, lam_im, log_dt, b_re, b_im, c_re, c_im, d_skip, w_glu,
                    b_glu, q_norm_g, k_norm_g, ssm_out_g, attn_out_g, w_out, norm_ffn_g, w_query,
                    sub_keys1, sub_keys2, expert_u, expert_v, norm_ple_g, w_ple_gate, w_ple_proj,
                    final_norm_g)
```

```python
import functools
import math

import jax
import jax.numpy as jnp
from jax import lax
from jax.experimental import pallas as pl
from jax.experimental.pallas import tpu as pltpu

F32 = jnp.float32
BF16 = jnp.bfloat16

D_MODEL = 1024
D_PLE = 256
GRID_W = 64
EPS = 1e-6
LOG2_E = 1.4426950408889634
D_SSM = 512
SSM_P = 16
SSM_G = D_SSM // SSM_P
SSM_N = 64
D_ATTN = 512
HEAD_DIM = 64
V_ROWS = HEAD_DIM + 16
N_Q_HEADS = 8
N_KV_HEADS = 2
Q_PER_KV = 4
D_KV = 128
D_IN = D_SSM + D_ATTN + 2 * D_KV
ROPE_THETA = 10000.0
PEER_HEADS = 8
PEER_KEY_DIM = 256
PEER_HALF = 128
N_KEYS = 128
PEER_TOPK = 16

PEER_BLOCK_ROWS = 2
PEER_STRIP = 256
SSM_CHUNK = 32
SSM_LP = SSM_CHUNK * SSM_P
VMEM_LIMIT = 48 * 1024 * 1024

def _cparams(*sem):
    return pltpu.CompilerParams(dimension_semantics=sem, vmem_limit_bytes=VMEM_LIMIT)


def _rms(x, g):
    return x * lax.rsqrt(jnp.mean(x * x, axis=-1, keepdims=True) + EPS) * g


def _in_proj_kernel(x_ref, g_ref, w_ref, cq_ref, sq_ref, gq_ref, gk_ref, bd_ref,
                    u_ref, qt_ref, k_ref, vt_ref):
    h = _rms(x_ref[...], g_ref[...])
    z = jnp.dot(h.astype(BF16), w_ref[...], preferred_element_type=F32)
    u_ref[...] = z[:, :D_SSM]

    def head_norm_rope(t, gain, bd, c, s):
        ss = jnp.dot((t * t).astype(BF16), bd, preferred_element_type=F32)
        t = t * lax.rsqrt(ss * (1.0 / HEAD_DIM) + EPS) * gain
        w = t.shape[1]
        lane = lax.broadcasted_iota(jnp.int32, t.shape, 1)
        partner = jnp.where((lane % 32) < 16, pltpu.roll(t, w - 16, 1), pltpu.roll(t, 16, 1))
        return t * c + partner * s

    cq = cq_ref[...]
    sq = sq_ref[...]
    q = head_norm_rope(z[:, D_SSM:D_SSM + D_ATTN], gq_ref[...], bd_ref[...], cq, sq)
    qt_ref[...] = (q * (HEAD_DIM ** -0.5 * LOG2_E)).T.astype(BF16)
    k = head_norm_rope(z[:, D_SSM + D_ATTN:D_SSM + D_ATTN + D_KV], gk_ref[...],
                       bd_ref[:D_KV, :D_KV], cq[:, :D_KV], sq[:, :D_KV])
    vt = z[:, D_SSM + D_ATTN + D_KV:].T
    row = lax.broadcasted_iota(jnp.int32, (V_ROWS - HEAD_DIM, vt.shape[1]), 0)
    ones_row = jnp.where(row == 0, 1.0, 0.0).astype(F32)
    for j in range(N_KV_HEADS):
        k_ref[j] = k[:, j * HEAD_DIM:(j + 1) * HEAD_DIM].astype(BF16)
        vt_ref[j, 0] = jnp.concatenate([vt[j * HEAD_DIM:(j + 1) * HEAD_DIM], ones_row], axis=0).astype(BF16)


def _in_proj(x, g, w, cq, sq, gq, gk, bd, seq, tm):
    t = x.shape[0]
    nseq = seq // tm
    return pl.pallas_call(
        _in_proj_kernel,
        grid=(t // tm,),
        in_specs=[
            pl.BlockSpec((tm, D_MODEL), lambda i: (i, 0)),
            pl.BlockSpec((1, D_MODEL), lambda i: (0, 0)),
            pl.BlockSpec((D_MODEL, D_IN), lambda i: (0, 0)),
            pl.BlockSpec((tm, D_ATTN), lambda i: (i % nseq, 0)),
            pl.BlockSpec((tm, D_ATTN), lambda i: (i % nseq, 0)),
            pl.BlockSpec((1, D_ATTN), lambda i: (0, 0)),
            pl.BlockSpec((1, D_KV), lambda i: (0, 0)),
            pl.BlockSpec((D_ATTN, D_ATTN), lambda i: (0, 0)),
        ],
        out_specs=[
            pl.BlockSpec((tm, D_SSM), lambda i: (i, 0)),
            pl.BlockSpec((D_ATTN, tm), lambda i: (0, i)),
            pl.BlockSpec((N_KV_HEADS, tm, HEAD_DIM), lambda i: (0, i, 0)),
            pl.BlockSpec((N_KV_HEADS, 1, V_ROWS, tm), lambda i: (0, i, 0, 0)),
        ],
        out_shape=[
            jax.ShapeDtypeStruct((t, D_SSM), F32),
            jax.ShapeDtypeStruct((D_ATTN, t), BF16),
            jax.ShapeDtypeStruct((N_KV_HEADS, t, HEAD_DIM), BF16),
            jax.ShapeDtypeStruct((N_KV_HEADS, t // tm, V_ROWS, tm), BF16),
        ],
        compiler_params=_cparams("parallel"),
        name="in_proj",
    )(x, g, w, cq, sq, gq, gk, bd)


def _ssm_params_kernel(lr_ref, li_ref, ldt_ref, bt_re_ref, bt_im_ref, c_re_ref, c_im_ref,
                       mt_ref, et_ref, fm_ref, dec_ref):
    L = SSM_CHUNK
    hi = lax.Precision.HIGHEST
    r_parts, e_parts, f_parts, p_parts, q_parts = [], [], [], [], []
    idx = lax.broadcasted_iota(jnp.int32, (L, 1, SSM_N), 0).astype(F32)
    for d in range(2):
        lr = lr_ref[0, d]
        li = li_ref[0, d]
        dt = jnp.exp(ldt_ref[0, d])

        def apow(tau):
            mag = jnp.exp(tau * (lr * dt))
            ang = tau * (li * dt)
            return mag * jnp.cos(ang), mag * jnp.sin(ang)

        ab_re, ab_im = apow(jnp.ones((1, SSM_N), F32))
        den = lr * lr + li * li
        nr = ab_re - 1.0
        f_re = (nr * lr + ab_im * li) / den
        f_im = (ab_im * lr - nr * li) / den
        bt_re = bt_re_ref[0, d]
        bt_im = bt_im_ref[0, d]
        bb_re = f_re * bt_re - f_im * bt_im
        bb_im = f_re * bt_im + f_im * bt_re
        c_re = c_re_ref[0, d]
        c_im = c_im_ref[0, d]

        tau_k = idx if d == 0 else (L - 1.0) - idx
        pr, pi = apow(tau_k)
        lm_re = (c_re[None] * pr - c_im[None] * pi).reshape(L * SSM_P, SSM_N)
        lm_im = (c_re[None] * pi + c_im[None] * pr).reshape(L * SSM_P, SSM_N)
        nt = (((1,), (1,)), ((), ()))
        r_parts.append(lax.dot_general(bb_re, lm_re, nt, precision=hi, preferred_element_type=F32)
                       - lax.dot_general(bb_im, lm_im, nt, precision=hi, preferred_element_type=F32))

        tau_e = (L - 1.0) - idx if d == 0 else idx
        pr, pi = apow(tau_e)
        e_parts.append((pr * bb_re[None] - pi * bb_im[None]).reshape(L * SSM_P, SSM_N))
        e_parts.append((pr * bb_im[None] + pi * bb_re[None]).reshape(L * SSM_P, SSM_N))

        tau_f = idx + 1.0 if d == 0 else L - idx
        pr, pi = apow(tau_f)
        f_parts.append((c_re[None] * pr - c_im[None] * pi).reshape(L * SSM_P, SSM_N))
        f_parts.append(-(c_re[None] * pi + c_im[None] * pr).reshape(L * SSM_P, SSM_N))

        al_re, al_im = apow(jnp.full((1, SSM_N), float(L), F32))
        p_parts += [al_re, al_re]
        q_parts += [-al_im, al_im]

    zeros = jnp.zeros((SSM_P, SSM_LP), F32)
    r_bwd = jnp.concatenate([r_parts[1], zeros], axis=1)
    r_fwd = pltpu.roll(jnp.concatenate([r_parts[0], zeros], axis=1), SSM_LP - SSM_P, 1)
    r_cat = r_bwd + r_fwd
    for s in range(L):
        shift = (2 * SSM_LP - (L - 1 - s) * SSM_P) % (2 * SSM_LP)
        win = r_cat if shift == 0 else pltpu.roll(r_cat, shift, 1)
        mt_ref[0, s * SSM_P:(s + 1) * SSM_P, :] = win[:, :SSM_LP].astype(BF16)
    et_ref[0] = jnp.concatenate(e_parts, axis=1).astype(BF16)
    fm_ref[0] = jnp.concatenate(f_parts, axis=1).astype(BF16)
    dec_ref[0] = jnp.concatenate([jnp.concatenate(p_parts, axis=1),
                                  jnp.concatenate(q_parts, axis=1)], axis=0)


def _ssm_params(lr, li, ldt, bt_re, bt_im, c_re, c_im):
    g = lr.shape[0]
    vec = pl.BlockSpec((1, 2, 1, SSM_N), lambda i: (i, 0, 0, 0))
    mat = pl.BlockSpec((1, 2, SSM_P, SSM_N), lambda i: (i, 0, 0, 0))
    return pl.pallas_call(
        _ssm_params_kernel,
        grid=(g,),
        in_specs=[vec, vec, vec, mat, mat, mat, mat],
        out_specs=[
            pl.BlockSpec((1, SSM_LP, SSM_LP), lambda i: (i, 0, 0)),
            pl.BlockSpec((1, SSM_LP, 4 * SSM_N), lambda i: (i, 0, 0)),
            pl.BlockSpec((1, SSM_LP, 4 * SSM_N), lambda i: (i, 0, 0)),
            pl.BlockSpec((1, 2, 4 * SSM_N), lambda i: (i, 0, 0)),
        ],
        out_shape=[
            jax.ShapeDtypeStruct((g, SSM_LP, SSM_LP), BF16),
            jax.ShapeDtypeStruct((g, SSM_LP, 4 * SSM_N), BF16),
            jax.ShapeDtypeStruct((g, SSM_LP, 4 * SSM_N), BF16),
            jax.ShapeDtypeStruct((g, 2, 4 * SSM_N), F32),
        ],
        compiler_params=_cparams("parallel"),
        name="ssm_params",
    )(lr, li, ldt, bt_re, bt_im, c_re, c_im)


def _ssm_conv_kernel(u_ref, mt_ref, et_ref, fm_ref, dec_ref, y_ref, *, n_batch):
    u = u_ref[0]
    nc = u.shape[0]
    ncb = nc // n_batch
    w = 2 * SSM_N
    s_all = jnp.dot(u, et_ref[0], preferred_element_type=F32)
    dec = dec_ref[0]
    row = lax.broadcasted_iota(jnp.int32, (ncb, w), 0)

    def cmul(x, p, q):
        return p * x + q * pltpu.roll(x, SSM_N, 1)

    h_rows = []
    for b in range(n_batch):
        s_b = s_all[b * ncb:(b + 1) * ncb]
        xf, xb = s_b[:, :w], s_b[:, w:]
        pf, qf = dec[0:1, :w], dec[1:2, :w]
        pb, qb = dec[0:1, w:], dec[1:2, w:]
        k = 1
        while k < ncb:
            xf = xf + jnp.where(row >= k, cmul(pltpu.roll(xf, k, 0), pf, qf), 0.0)
            xb = xb + jnp.where(row < ncb - k, cmul(pltpu.roll(xb, ncb - k, 0), pb, qb), 0.0)
            pf, qf = pf * pf - qf * qf, 2.0 * pf * qf
            pb, qb = pb * pb - qb * qb, 2.0 * pb * qb
            k *= 2
        hf = jnp.where(row >= 1, pltpu.roll(xf, 1, 0), 0.0)
        hb = jnp.where(row < ncb - 1, pltpu.roll(xb, ncb - 1, 0), 0.0)
        h_rows.append(jnp.concatenate([hf, hb], axis=1))
    h = jnp.concatenate(h_rows, axis=0).astype(BF16)
    y = jnp.dot(u, mt_ref[0], preferred_element_type=F32)
    y = y + lax.dot_general(h, fm_ref[0], (((1,), (1,)), ((), ())), preferred_element_type=F32)
    y_ref[0] = y


def _ssm_conv(ug, mt, et, fm, dec, n_batch):
    g, nc, lp = ug.shape
    return pl.pallas_call(
        functools.partial(_ssm_conv_kernel, n_batch=n_batch),
        grid=(g,),
        in_specs=[
            pl.BlockSpec((1, nc, lp), lambda i: (i, 0, 0)),
            pl.BlockSpec((1, lp, lp), lambda i: (i, 0, 0)),
            pl.BlockSpec((1, lp, 4 * SSM_N), lambda i: (i, 0, 0)),
            pl.BlockSpec((1, lp, 4 * SSM_N), lambda i: (i, 0, 0)),
            pl.BlockSpec((1, 2, 4 * SSM_N), lambda i: (i, 0, 0)),
        ],
        out_specs=pl.BlockSpec((1, nc, lp), lambda i: (i, 0, 0)),
        out_shape=jax.ShapeDtypeStruct((g, nc, lp), F32),
        compiler_params=_cparams("parallel"),
        name="ssm_conv",
    )(ug, mt, et, fm, dec)


def _attn_kernel(qt_ref, k_ref, vt_ref, o_ref, q_sc, m_ref, acc_ref, sa_ref, sb_ref, *, tq, tk, nk):
    qt = qt_ref[...]
    for j in range(Q_PER_KV):
        q_sc[:, j * tq:(j + 1) * tq] = qt[j * HEAD_DIM:(j + 1) * HEAD_DIM, :]
    m_ref[...] = jnp.full(m_ref.shape, -jnp.inf, F32)
    acc_ref[...] = jnp.zeros(acc_ref.shape, F32)

    def scores(i):
        kb = k_ref[0, pl.ds(pl.multiple_of(i * tk, tk), tk), :]
        return jnp.dot(kb, q_sc[...], preferred_element_type=F32)

    def accumulate(s_ref, i):
        s = s_ref[...]
        m_prev = m_ref[...]
        m_new = jnp.maximum(m_prev, jnp.max(s, axis=0, keepdims=True))
        alpha = jnp.exp2(m_prev - m_new)
        p = jnp.exp2(s - m_new[0:1, :])
        acc_ref[:V_ROWS, :] = (acc_ref[:V_ROWS, :] * alpha[0:1, :]
                               + jnp.dot(vt_ref[0, i], p.astype(BF16), preferred_element_type=F32))
        m_ref[...] = m_new

    sa_ref[...] = scores(0)

    def body(j, carry):
        sb_ref[...] = scores(2 * j + 1)
        accumulate(sa_ref, 2 * j)
        sa_ref[...] = scores(2 * j + 2)
        accumulate(sb_ref, 2 * j + 1)
        return carry

    lax.fori_loop(0, nk // 2 - 1, body, 0)
    sb_ref[...] = scores(nk - 1)
    accumulate(sa_ref, nk - 2)
    accumulate(sb_ref, nk - 1)
    acc = acc_ref[...]
    out = (acc / acc[HEAD_DIM:HEAD_DIM + 1, :]).T
    o_ref[...] = jnp.concatenate([out[j * tq:(j + 1) * tq, :HEAD_DIM] for j in range(Q_PER_KV)], axis=1)


def _attention(qt, k, vt, n_batch, seq, tq):
    t = qt.shape[1]
    tk = vt.shape[3]
    nq = seq // tq
    nk = seq // tk
    assert nk % 2 == 0, "the key loop handles chunks in pairs"
    w = Q_PER_KV * HEAD_DIM
    return pl.pallas_call(
        functools.partial(_attn_kernel, tq=tq, tk=tk, nk=nk),
        grid=(n_batch, N_KV_HEADS, nq),
        in_specs=[
            pl.BlockSpec((w, tq), lambda b, g, i: (g, b * nq + i)),
            pl.BlockSpec((1, seq, HEAD_DIM), lambda b, g, i: (g, b, 0)),
            pl.BlockSpec((1, nk, V_ROWS, tk), lambda b, g, i: (g, b, 0, 0)),
        ],
        out_specs=pl.BlockSpec((tq, w), lambda b, g, i: (b * nq + i, g)),
        out_shape=jax.ShapeDtypeStruct((t, D_ATTN), F32),
        scratch_shapes=[
            pltpu.VMEM((HEAD_DIM, Q_PER_KV * tq), BF16),
            pltpu.VMEM((8, Q_PER_KV * tq), F32),
            pltpu.VMEM((2 * HEAD_DIM, Q_PER_KV * tq), F32),
            pltpu.VMEM((tk, Q_PER_KV * tq), F32),
            pltpu.VMEM((tk, Q_PER_KV * tq), F32),
        ],
        compiler_params=_cparams("parallel", "parallel", "parallel"),
        name="attention",
    )(qt, k, vt)


def _out_proj_kernel(x_ref, ys_ref, u_ref, at_ref, dsk_ref, wglu_ref, bglu_ref, gs_ref, ga_ref,
                     wo_ref, gf_ref, xo_ref, xnt_ref):
    y = ys_ref[...] + dsk_ref[...] * u_ref[...]
    y = jax.nn.gelu(y)
    z = jnp.dot(y.astype(BF16), wglu_ref[...], preferred_element_type=F32) + bglu_ref[...]
    y = y * jax.nn.sigmoid(z)
    ys = _rms(y, gs_ref[...]).astype(BF16)
    ya = _rms(at_ref[...], ga_ref[...]).astype(BF16)
    xo = (x_ref[...]
          + jnp.dot(ys, wo_ref[:D_SSM, :], preferred_element_type=F32)
          + jnp.dot(ya, wo_ref[D_SSM:, :], preferred_element_type=F32))
    xo_ref[...] = xo
    xnt_ref[...] = _rms(xo, gf_ref[...]).T.astype(BF16)


def _out_proj(x, ys, u, at, dsk, wglu, bglu, gs, ga, wo, gf, tm):
    t = x.shape[0]
    row = lambda w: pl.BlockSpec((tm, w), lambda i: (i, 0))
    full = lambda a, b: pl.BlockSpec((a, b), lambda i: (0, 0))
    return pl.pallas_call(
        _out_proj_kernel,
        grid=(t // tm,),
        in_specs=[row(D_MODEL), row(D_SSM), row(D_SSM), row(D_ATTN), full(1, D_SSM),
                  full(D_SSM, D_SSM), full(1, D_SSM), full(1, D_SSM), full(1, D_ATTN),
                  full(D_MODEL, D_MODEL), full(1, D_MODEL)],
        out_specs=[row(D_MODEL), pl.BlockSpec((D_MODEL, tm), lambda i: (0, i))],
        out_shape=[jax.ShapeDtypeStruct((t, D_MODEL), F32), jax.ShapeDtypeStruct((D_MODEL, t), BF16)],
        compiler_params=_cparams("parallel"),
        name="out_proj",
    )(x, ys, u, at, dsk, wglu, bglu, gs, ga, wo, gf)


def _oe_sort_network(n):
    pairs = []
    p = 1
    while p < n:
        k = p
        while k >= 1:
            for j in range(k % p, n - k, 2 * k):
                for i in range(min(k, n - j - k)):
                    if (i + j) // (2 * p) == (i + j + k) // (2 * p):
                        pairs.append((i + j, i + j + k))
            k //= 2
        p *= 2
    return pairs


def _ce(a, b):
    if b is None:
        return a, None
    if a is None:
        return b, None
    return jnp.maximum(a, b), jnp.minimum(a, b)


def _sorted_top16(slabs):
    n = PEER_TOPK
    v = list(slabs) + [None] * (n - len(slabs))
    for i, j in _oe_sort_network(n):
        v[i], v[j] = _ce(v[i], v[j])
    for shift in (4, 2, 1):
        partner = [None if x is None else pltpu.roll(x, shift, 0) for x in v]
        top = []
        for a in range(n):
            x, y = v[a], partner[n - 1 - a]
            top.append(y if x is None else (x if y is None else jnp.maximum(x, y)))
        v = top
        d = n // 2
        while d >= 1:
            for i in range(n):
                if i & d == 0:
                    v[i], v[i + d] = _ce(v[i], v[i + d])
            d //= 2
    return v


_CAND = [(a, b) for a in range(PEER_TOPK) for b in range(PEER_TOPK) if (a + 1) * (b + 1) <= PEER_TOPK]


def _count_prefix(pred, vals):
    pick = lambda m, a, b: jnp.where(m, a, b)
    m8 = pred(vals[7])
    m4 = pred(pick(m8, vals[11], vals[3]))
    m2 = pred(pick(m8, pick(m4, vals[13], vals[9]), pick(m4, vals[5], vals[1])))
    m1 = pred(pick(m8, pick(m4, pick(m2, vals[14], vals[12]), pick(m2, vals[10], vals[8])),
                   pick(m4, pick(m2, vals[6], vals[4]), pick(m2, vals[2], vals[0]))))
    cnt = pick(m8, 8.0, 0.0) + pick(m4, 4.0, 0.0) + pick(m2, 2.0, 0.0) + pick(m1, 1.0, 0.0)
    return cnt + pick(pred(vals[15]), 1.0, 0.0)


def _bf16_pair_words(x):
    b = pltpu.bitcast(x, jnp.uint32)
    b = (b + jnp.uint32(0x7FFF) + ((b >> 16) & jnp.uint32(1))) >> 16
    return b | (b << 16)


def _peer_route_kernel(xnt_ref, wq_ref, k1_ref, k2_ref, rho_ref, e2_ref, r1_ref, c1_ref):
    def scores(h):
        qt = jnp.dot(wq_ref[h * PEER_KEY_DIM:(h + 1) * PEER_KEY_DIM, :], xnt_ref[...],
                     preferred_element_type=F32)
        return (jnp.dot(k1_ref[h], qt[:PEER_HALF].astype(BF16), preferred_element_type=F32),
                jnp.dot(k2_ref[h], qt[PEER_HALF:].astype(BF16), preferred_element_type=F32))

    s = scores(0)
    for h in range(PEER_HEADS):
        s_next = scores(h + 1) if h + 1 < PEER_HEADS else None
        _peer_route_head(h, s[0], s[1], rho_ref, e2_ref, r1_ref, c1_ref)
        s = s_next


def _peer_route_head(h, s1, s2, rho_ref, e2_ref, r1_ref, c1_ref):
    tm = s1.shape[1]
    nslab = N_KEYS // 8
    s1 = [s1[8 * i:8 * i + 8] for i in range(nslab)]
    s2 = [s2[8 * i:8 * i + 8] for i in range(nslab)]
    v1 = _sorted_top16(s1)
    v2 = _sorted_top16(s2)

    sub = lax.broadcasted_iota(jnp.int32, (8, tm), 0)
    cand = [v1[a] + v2[b] for a, b in _CAND]
    packed = []
    for base in range(0, len(cand), 8):
        slab = None
        for s, c in enumerate(cand[base:base + 8]):
            slab = c if slab is None else jnp.where(sub == s, c, slab)
        if len(cand) - base < 8:
            slab = jnp.where(sub < len(cand) - base, slab, -jnp.inf)
        packed.append(slab)
    theta = _sorted_top16(packed)[PEER_TOPK - 1]

    top = cand[0]
    z = jnp.zeros((8, tm), F32)
    for c in cand:
        z = z + jnp.where(c >= theta, jnp.exp(c - top), 0.0)
    inv_z = 1.0 / z

    r1, c1, rho, e2 = [], [], [], []
    for i in range(nslab):
        r1.append(_count_prefix(lambda v, x=s1[i]: x + v >= theta, v2))
        rho.append(_count_prefix(lambda v, x=s2[i]: v > x, v2))
        c1.append(jnp.exp(s1[i] - v1[0]) * inv_z)
        e2.append(jnp.exp(s2[i] - v2[0]))
    r1 = _bf16_pair_words(jnp.concatenate(r1, axis=0))
    c1 = _bf16_pair_words(jnp.concatenate(c1, axis=0))
    rho = jnp.concatenate(rho, axis=0).astype(BF16)
    e2 = jnp.concatenate(e2, axis=0).astype(BF16)
    for l in range(tm // 128):
        lanes = slice(l * 128, (l + 1) * 128)
        r1_ref[h, l] = r1[:, lanes]
        c1_ref[h, l] = c1[:, lanes]
    for l in range(tm // 256):
        lanes = slice(l * 256, (l + 1) * 256)
        rho_ref[h, l * N_KEYS:(l + 1) * N_KEYS, :] = rho[:, lanes]
        e2_ref[h, l * N_KEYS:(l + 1) * N_KEYS, :] = e2[:, lanes]


def _peer_route(xn_t, wq_t, k1, k2, tm):
    t = xn_t.shape[1]
    sel = pl.BlockSpec((PEER_HEADS, tm // 128, N_KEYS, 128), lambda i: (0, i, 0, 0))
    shp = jax.ShapeDtypeStruct((PEER_HEADS, t // 128, N_KEYS, 128), jnp.uint32)
    tok = pl.BlockSpec((PEER_HEADS, tm // 2, 256), lambda i: (0, i, 0))
    shp16 = jax.ShapeDtypeStruct((PEER_HEADS, t // 2, 256), BF16)
    return pl.pallas_call(
        _peer_route_kernel,
        grid=(t // tm,),
        in_specs=[
            pl.BlockSpec((D_MODEL, tm), lambda i: (0, i)),
            pl.BlockSpec((PEER_HEADS * PEER_KEY_DIM, D_MODEL), lambda i: (0, 0)),
            pl.BlockSpec((PEER_HEADS, N_KEYS, PEER_HALF), lambda i: (0, 0, 0)),
            pl.BlockSpec((PEER_HEADS, N_KEYS, PEER_HALF), lambda i: (0, 0, 0)),
        ],
        out_specs=[tok, tok, sel, sel],
        out_shape=[shp16, shp16, shp, shp],
        compiler_params=_cparams("parallel"),
        name="peer_route",
    )(xn_t, wq_t, k1, k2)


def _gelu_tanh(x):
    k1 = -2.0 * math.sqrt(2.0 / math.pi) * LOG2_E
    t = x * (k1 + (k1 * 0.044715) * (x * x))
    return x / (1.0 + jnp.exp2(t))


def _load_row_repeated(ref, h, l, i, n):
    return ref[h, l, pl.ds(i, n, stride=0), :]


def _peer_dense_kernel(xn_ref, u_ref, vt_ref, rho_ref, e2_ref, r1_ref, c1_ref, o_ref, a_ref, *, rows):
    e = pl.program_id(1)

    @pl.when(e == 0)
    def _():
        o_ref[...] = jnp.zeros(o_ref.shape, F32)

    tt = xn_ref.shape[1]
    sub = 16
    strip = PEER_STRIP
    zero = jnp.zeros((sub, strip), BF16)

    def row_tile(ref, h, s, i):
        return jnp.concatenate(
            [pltpu.bitcast(_load_row_repeated(ref, h, s * (strip // 128) + l, i, sub // 2), BF16)
             for l in range(strip // 128)], axis=1)

    blk = PEER_BLOCK_ROWS * N_KEYS
    nblk = rows // PEER_BLOCK_ROWS

    def first_matmul(k):
        return jnp.dot(u_ref[k * blk:(k + 1) * blk, :].astype(BF16), xn_ref[...], preferred_element_type=F32)

    act = first_matmul(0)
    for k in range(nblk):
        act_next = first_matmul(k + 1) if k + 1 < nblk else None
        a_ref[k * blk:(k + 1) * blk, :] = _gelu_tanh(act).astype(BF16)
        for s in range(tt // strip):
            lanes = slice(s * strip, (s + 1) * strip)
            for ii in range(PEER_BLOCK_ROWS):
                i = k * PEER_BLOCK_ROWS + ii
                rc = [(row_tile(r1_ref, h, s, i), row_tile(c1_ref, h, s, i)) for h in range(PEER_HEADS)]
                for jb in range(N_KEYS // sub):
                    keys = slice(s * N_KEYS + jb * sub, s * N_KEYS + (jb + 1) * sub)
                    w = None
                    for h in range(PEER_HEADS):
                        r, c = rc[h]
                        term = c * jnp.where(rho_ref[h, keys, :] < r, e2_ref[h, keys, :], zero)
                        w = term if w is None else w + term
                    lo = k * blk + ii * N_KEYS + jb * sub
                    a_ref[lo:lo + sub, lanes] = a_ref[lo:lo + sub, lanes] * w
        o_ref[...] += lax.dot_general(vt_ref[k * blk:(k + 1) * blk, :].astype(BF16),
                                      a_ref[k * blk:(k + 1) * blk, :], (((0,), (0,)), ((), ())),
                                      preferred_element_type=F32)
        act = act_next


def _peer_dense(xn_t, eu, evt, rho, e2, r1, c1, tt, te):
    t = xn_t.shape[1]
    ne = eu.shape[0]
    rows = te // N_KEYS
    tok = pl.BlockSpec((PEER_HEADS, tt // 2, 256), lambda j, e: (0, j, 0))
    sel = pl.BlockSpec((PEER_HEADS, tt // 128, rows, 128), lambda j, e: (0, j, e, 0))
    return pl.pallas_call(
        functools.partial(_peer_dense_kernel, rows=rows),
        grid=(t // tt, ne // te),
        in_specs=[
            pl.BlockSpec((D_MODEL, tt), lambda j, e: (0, j)),
            pl.BlockSpec((te, D_MODEL), lambda j, e: (e, 0)),
            pl.BlockSpec((te, D_MODEL), lambda j, e: (e, 0)),
            tok, tok, sel, sel,
        ],
        out_specs=pl.BlockSpec((D_MODEL, tt), lambda j, e: (0, j)),
        out_shape=jax.ShapeDtypeStruct((D_MODEL, t), F32),
        scratch_shapes=[pltpu.VMEM((te, tt), BF16)],
        compiler_params=_cparams("parallel", "arbitrary"),
        name="peer_dense",
    )(xn_t, eu, evt, rho, e2, r1, c1)


def _ple_kernel(x_ref, pe_ref, p_ref, g_ref, wg_ref, wp_ref, gfin_ref, o_ref, *, final):
    x = x_ref[...] + pe_ref[...].T
    gate = jax.nn.sigmoid(jnp.dot(_rms(x, g_ref[...]).astype(BF16), wg_ref[...], preferred_element_type=F32))
    x = x + gate * jnp.dot(p_ref[...].astype(BF16), wp_ref[...], preferred_element_type=F32)
    if final:
        x = _rms(x, gfin_ref[...])
    o_ref[...] = x


def _ple(x, pe, p, g, wg, wp, gfin, final, tm):
    t = x.shape[0]
    row = lambda w: pl.BlockSpec((tm, w), lambda i: (i, 0))
    full = lambda a, b: pl.BlockSpec((a, b), lambda i: (0, 0))
    return pl.pallas_call(
        functools.partial(_ple_kernel, final=final),
        grid=(t // tm,),
        in_specs=[row(D_MODEL), pl.BlockSpec((D_MODEL, tm), lambda i: (0, i)), row(D_PLE), full(1, D_MODEL),
                  full(D_MODEL, D_MODEL), full(D_PLE, D_MODEL), full(1, D_MODEL)],
        out_specs=row(D_MODEL),
        out_shape=jax.ShapeDtypeStruct((t, D_MODEL), F32),
        compiler_params=_cparams("parallel"),
        name="ple",
    )(x, pe, p, g, wg, wp, gfin)


def _rope_tables(seq):
    n_rows = seq // GRID_W
    row = jnp.repeat(jnp.arange(n_rows, dtype=jnp.int32), GRID_W).astype(F32)
    col = jnp.tile(jnp.arange(GRID_W, dtype=jnp.int32), n_rows).astype(F32)
    quarter = HEAD_DIM // 4
    inv = ROPE_THETA ** (-jnp.arange(quarter, dtype=F32) / quarter)
    ang_r = row[:, None] * inv[None, :]
    ang_c = col[:, None] * inv[None, :]
    cos = jnp.concatenate([jnp.cos(ang_r)] * 2 + [jnp.cos(ang_c)] * 2, axis=1)
    sin = jnp.concatenate([-jnp.sin(ang_r), jnp.sin(ang_r), -jnp.sin(ang_c), jnp.sin(ang_c)], axis=1)
    return jnp.tile(cos, (1, N_Q_HEADS)), jnp.tile(sin, (1, N_Q_HEADS))


def _forward(x, p, norm_mix_g, w_in, lam_re, lam_im, log_dt, b_re, b_im, c_re, c_im, d_skip, w_glu,
             b_glu, q_norm_g, k_norm_g, ssm_out_g, attn_out_g, w_out, norm_ffn_g, w_query, sub_keys1,
             sub_keys2, expert_u, expert_v, norm_ple_g, w_ple_gate, w_ple_proj, final_norm_g):
    n_batch, seq, _ = x.shape
    depth = w_in.shape[0]
    t = n_batch * seq
    tm = min(512, seq)
    tq = min(512, seq)
    tk = min(512, seq // 2)
    tt = min(512, t)
    te = 2048
    nc = t // SSM_CHUNK

    cq, sq = _rope_tables(seq)
    bd = jnp.kron(jnp.eye(N_Q_HEADS, dtype=F32), jnp.ones((HEAD_DIM, HEAD_DIM), F32)).astype(BF16)
    row = lambda a: a.reshape(1, -1).astype(F32)

    xf = x.reshape(t, D_MODEL)
    for i in range(depth):
        u, qt, k, vt = _in_proj(xf, row(norm_mix_g[i]), w_in[i].astype(BF16), cq, sq,
                                row(jnp.tile(q_norm_g[i], N_Q_HEADS)), row(jnp.tile(k_norm_g[i], N_KV_HEADS)),
                                bd, seq, tk)

        grp = lambda a: jnp.transpose(a, (1, 0, 2))[:, :, None, :]
        ldt = jnp.broadcast_to(log_dt[i][:, :, None], (2, SSM_G, SSM_N))
        mt, et, fm, dec = _ssm_params(
            grp(lam_re[i]), grp(lam_im[i]), grp(ldt),
            jnp.transpose(b_re[i], (1, 0, 3, 2)), jnp.transpose(b_im[i], (1, 0, 3, 2)),
            jnp.transpose(c_re[i], (1, 0, 2, 3)), jnp.transpose(c_im[i], (1, 0, 2, 3)))
        ug = (u.astype(BF16).reshape(nc, SSM_CHUNK, SSM_G, SSM_P)
              .transpose(2, 0, 1, 3).reshape(SSM_G, nc, SSM_LP))
        yg = _ssm_conv(ug, mt, et, fm, dec, n_batch)
        y_ssm = yg.reshape(SSM_G, nc, SSM_CHUNK, SSM_P).transpose(1, 2, 0, 3).reshape(t, D_SSM)

        attn = _attention(qt, k, vt, n_batch, seq, tq)

        xf, xnt = _out_proj(xf, y_ssm, u, attn, row(d_skip[i]), w_glu[i].astype(BF16), row(b_glu[i]),
                            row(ssm_out_g[i]), row(attn_out_g[i]), w_out[i].astype(BF16),
                            row(norm_ffn_g[i]), tm)

        rho, e2, r1, c1 = _peer_route(xnt, w_query[i].T.astype(BF16), sub_keys1[i].astype(BF16),
                                      sub_keys2[i].astype(BF16), min(256, t))
        peer_t = _peer_dense(xnt, expert_u[i], expert_v[i],
                             rho, e2, r1, c1, tt, te)

        xf = _ple(xf, peer_t, p[i].reshape(t, D_PLE), row(norm_ple_g[i]), w_ple_gate[i].astype(BF16),
                  w_ple_proj[i].astype(BF16), row(final_norm_g), i == depth - 1, tm)
    return xf.reshape(n_batch, seq, D_MODEL)


def kernel(x, p, norm_mix_g, w_in, lam_re, lam_im, log_dt, b_re, b_im, c_re, c_im, d_skip, w_glu, b_glu, q_norm_g, k_norm_g, ssm_out_g, attn_out_g, w_out, norm_ffn_g, w_query, sub_keys1, sub_keys2, expert_u, expert_v, norm_ple_g, w_ple_gate, w_ple_proj, final_norm_g):
    return _forward(x, p, norm_mix_g, w_in, lam_re, lam_im, log_dt, b_re, b_im, c_re, c_im, d_skip, w_glu,
                    b_glu, q_norm_g, k_norm_g, ssm_out_g, attn_out_g, w_out, norm_ffn_g, w_query,
                    sub_keys1, sub_keys2, expert_u, expert_v, norm_ple_g, w_ple_gate, w_ple_proj,
                    final_norm_g)
```

```python
import functools
import math

import jax
import jax.numpy as jnp
from jax import lax
from jax.experimental import pallas as pl
from jax.experimental.pallas import tpu as pltpu

F32 = jnp.float32
BF16 = jnp.bfloat16

D_MODEL = 1024
D_PLE = 256
GRID_W = 64
EPS = 1e-6
LOG2_E = 1.4426950408889634
D_SSM = 512
SSM_P = 16
SSM_G = D_SSM // SSM_P
SSM_N = 64
D_ATTN = 512
HEAD_DIM = 64
V_ROWS = HEAD_DIM + 16
N_Q_HEADS = 8
N_KV_HEADS = 2
Q_PER_KV = 4
D_KV = 128
D_IN = D_SSM + D_ATTN + 2 * D_KV
ROPE_THETA = 10000.0
PEER_HEADS = 8
PEER_KEY_DIM = 256
PEER_HALF = 128
N_KEYS = 128
PEER_TOPK = 16

PEER_BLOCK_ROWS = 2
PEER_STRIP = 256
SSM_CHUNK = 32
SSM_LP = SSM_CHUNK * SSM_P
VMEM_LIMIT = 48 * 1024 * 1024

def _cparams(*sem):
    return pltpu.CompilerParams(dimension_semantics=sem, vmem_limit_bytes=VMEM_LIMIT)


def _rms(x, g):
    return x * lax.rsqrt(jnp.mean(x * x, axis=-1, keepdims=True) + EPS) * g


def _in_proj_kernel(x_ref, g_ref, w_ref, cq_ref, sq_ref, gq_ref, gk_ref, bd_ref,
                    u_ref, qt_ref, k_ref, vt_ref):
    h = _rms(x_ref[...], g_ref[...])
    z = jnp.dot(h.astype(BF16), w_ref[...], preferred_element_type=F32)
    u_ref[...] = z[:, :D_SSM]

    def head_norm_rope(t, gain, bd, c, s):
        ss = jnp.dot((t * t).astype(BF16), bd, preferred_element_type=F32)
        t = t * lax.rsqrt(ss * (1.0 / HEAD_DIM) + EPS) * gain
        w = t.shape[1]
        lane = lax.broadcasted_iota(jnp.int32, t.shape, 1)
        partner = jnp.where((lane % 32) < 16, pltpu.roll(t, w - 16, 1), pltpu.roll(t, 16, 1))
        return t * c + partner * s

    cq = cq_ref[...]
    sq = sq_ref[...]
    q = head_norm_rope(z[:, D_SSM:D_SSM + D_ATTN], gq_ref[...], bd_ref[...], cq, sq)
    qt_ref[...] = (q * (HEAD_DIM ** -0.5 * LOG2_E)).T.astype(BF16)
    k = head_norm_rope(z[:, D_SSM + D_ATTN:D_SSM + D_ATTN + D_KV], gk_ref[...],
                       bd_ref[:D_KV, :D_KV], cq[:, :D_KV], sq[:, :D_KV])
    vt = z[:, D_SSM + D_ATTN + D_KV:].T
    row = lax.broadcasted_iota(jnp.int32, (V_ROWS - HEAD_DIM, vt.shape[1]), 0)
    ones_row = jnp.where(row == 0, 1.0, 0.0).astype(F32)
    for j in range(N_KV_HEADS):
        k_ref[j] = k[:, j * HEAD_DIM:(j + 1) * HEAD_DIM].astype(BF16)
        vt_ref[j, 0] = jnp.concatenate([vt[j * HEAD_DIM:(j + 1) * HEAD_DIM], ones_row], axis=0).astype(BF16)


def _in_proj(x, g, w, cq, sq, gq, gk, bd, seq, tm):
    t = x.shape[0]
    nseq = seq // tm
    return pl.pallas_call(
        _in_proj_kernel,
        grid=(t // tm,),
        in_specs=[
            pl.BlockSpec((tm, D_MODEL), lambda i: (i, 0)),
            pl.BlockSpec((1, D_MODEL), lambda i: (0, 0)),
            pl.BlockSpec((D_MODEL, D_IN), lambda i: (0, 0)),
            pl.BlockSpec((tm, D_ATTN), lambda i: (i % nseq, 0)),
            pl.BlockSpec((tm, D_ATTN), lambda i: (i % nseq, 0)),
            pl.BlockSpec((1, D_ATTN), lambda i: (0, 0)),
            pl.BlockSpec((1, D_KV), lambda i: (0, 0)),
            pl.BlockSpec((D_ATTN, D_ATTN), lambda i: (0, 0)),
        ],
        out_specs=[
            pl.BlockSpec((tm, D_SSM), lambda i: (i, 0)),
            pl.BlockSpec((D_ATTN, tm), lambda i: (0, i)),
            pl.BlockSpec((N_KV_HEADS, tm, HEAD_DIM), lambda i: (0, i, 0)),
            pl.BlockSpec((N_KV_HEADS, 1, V_ROWS, tm), lambda i: (0, i, 0, 0)),
        ],
        out_shape=[
            jax.ShapeDtypeStruct((t, D_SSM), F32),
            jax.ShapeDtypeStruct((D_ATTN, t), BF16),
            jax.ShapeDtypeStruct((N_KV_HEADS, t, HEAD_DIM), BF16),
            jax.ShapeDtypeStruct((N_KV_HEADS, t // tm, V_ROWS, tm), BF16),
        ],
        compiler_params=_cparams("parallel"),
        name="in_proj",
    )(x, g, w, cq, sq, gq, gk, bd)


def _ssm_params_kernel(lr_ref, li_ref, ldt_ref, bt_re_ref, bt_im_ref, c_re_ref, c_im_ref,
                       mt_ref, et_ref, fm_ref, dec_ref):
    L = SSM_CHUNK
    hi = lax.Precision.HIGHEST
    r_parts, e_parts, f_parts, p_parts, q_parts = [], [], [], [], []
    idx = lax.broadcasted_iota(jnp.int32, (L, 1, SSM_N), 0).astype(F32)
    for d in range(2):
        lr = lr_ref[0, d]
        li = li_ref[0, d]
        dt = jnp.exp(ldt_ref[0, d])

        def apow(tau):
            mag = jnp.exp(tau * (lr * dt))
            ang = tau * (li * dt)
            return mag * jnp.cos(ang), mag * jnp.sin(ang)

        ab_re, ab_im = apow(jnp.ones((1, SSM_N), F32))
        den = lr * lr + li * li
        nr = ab_re - 1.0
        f_re = (nr * lr + ab_im * li) / den
        f_im = (ab_im * lr - nr * li) / den
        bt_re = bt_re_ref[0, d]
        bt_im = bt_im_ref[0, d]
        bb_re = f_re * bt_re - f_im * bt_im
        bb_im = f_re * bt_im + f_im * bt_re
        c_re = c_re_ref[0, d]
        c_im = c_im_ref[0, d]

        tau_k = idx if d == 0 else (L - 1.0) - idx
        pr, pi = apow(tau_k)
        lm_re = (c_re[None] * pr - c_im[None] * pi).reshape(L * SSM_P, SSM_N)
        lm_im = (c_re[None] * pi + c_im[None] * pr).reshape(L * SSM_P, SSM_N)
        nt = (((1,), (1,)), ((), ()))
        r_parts.append(lax.dot_general(bb_re, lm_re, nt, precision=hi, preferred_element_type=F32)
                       - lax.dot_general(bb_im, lm_im, nt, precision=hi, preferred_element_type=F32))

        tau_e = (L - 1.0) - idx if d == 0 else idx
        pr, pi = apow(tau_e)
        e_parts.append((pr * bb_re[None] - pi * bb_im[None]).reshape(L * SSM_P, SSM_N))
        e_parts.append((pr * bb_im[None] + pi * bb_re[None]).reshape(L * SSM_P, SSM_N))

        tau_f = idx + 1.0 if d == 0 else L - idx
        pr, pi = apow(tau_f)
        f_parts.append((c_re[None] * pr - c_im[None] * pi).reshape(L * SSM_P, SSM_N))
        f_parts.append(-(c_re[None] * pi + c_im[None] * pr).reshape(L * SSM_P, SSM_N))

        al_re, al_im = apow(jnp.full((1, SSM_N), float(L), F32))
        p_parts += [al_re, al_re]
        q_parts += [-al_im, al_im]

    zeros = jnp.zeros((SSM_P, SSM_LP), F32)
    r_bwd = jnp.concatenate([r_parts[1], zeros], axis=1)
    r_fwd = pltpu.roll(jnp.concatenate([r_parts[0], zeros], axis=1), SSM_LP - SSM_P, 1)
    r_cat = r_bwd + r_fwd
    for s in range(L):
        shift = (2 * SSM_LP - (L - 1 - s) * SSM_P) % (2 * SSM_LP)
        win = r_cat if shift == 0 else pltpu.roll(r_cat, shift, 1)
        mt_ref[0, s * SSM_P:(s + 1) * SSM_P, :] = win[:, :SSM_LP].astype(BF16)
    et_ref[0] = jnp.concatenate(e_parts, axis=1).astype(BF16)
    fm_ref[0] = jnp.concatenate(f_parts, axis=1).astype(BF16)
    dec_ref[0] = jnp.concatenate([jnp.concatenate(p_parts, axis=1),
                                  jnp.concatenate(q_parts, axis=1)], axis=0)


def _ssm_params(lr, li, ldt, bt_re, bt_im, c_re, c_im):
    g = lr.shape[0]
    vec = pl.BlockSpec((1, 2, 1, SSM_N), lambda i: (i, 0, 0, 0))
    mat = pl.BlockSpec((1, 2, SSM_P, SSM_N), lambda i: (i, 0, 0, 0))
    return pl.pallas_call(
        _ssm_params_kernel,
        grid=(g,),
        in_specs=[vec, vec, vec, mat, mat, mat, mat],
        out_specs=[
            pl.BlockSpec((1, SSM_LP, SSM_LP), lambda i: (i, 0, 0)),
            pl.BlockSpec((1, SSM_LP, 4 * SSM_N), lambda i: (i, 0, 0)),
            pl.BlockSpec((1, SSM_LP, 4 * SSM_N), lambda i: (i, 0, 0)),
            pl.BlockSpec((1, 2, 4 * SSM_N), lambda i: (i, 0, 0)),
        ],
        out_shape=[
            jax.ShapeDtypeStruct((g, SSM_LP, SSM_LP), BF16),
            jax.ShapeDtypeStruct((g, SSM_LP, 4 * SSM_N), BF16),
            jax.ShapeDtypeStruct((g, SSM_LP, 4 * SSM_N), BF16),
            jax.ShapeDtypeStruct((g, 2, 4 * SSM_N), F32),
        ],
        compiler_params=_cparams("parallel"),
        name="ssm_params",
    )(lr, li, ldt, bt_re, bt_im, c_re, c_im)


def _ssm_conv_kernel(u_ref, mt_ref, et_ref, fm_ref, dec_ref, y_ref, *, n_batch):
    u = u_ref[0]
    nc = u.shape[0]
    ncb = nc // n_batch
    w = 2 * SSM_N
    s_all = jnp.dot(u, et_ref[0], preferred_element_type=F32)
    dec = dec_ref[0]
    row = lax.broadcasted_iota(jnp.int32, (ncb, w), 0)

    def cmul(x, p, q):
        return p * x + q * pltpu.roll(x, SSM_N, 1)

    h_rows = []
    for b in range(n_batch):
        s_b = s_all[b * ncb:(b + 1) * ncb]
        xf, xb = s_b[:, :w], s_b[:, w:]
        pf, qf = dec[0:1, :w], dec[1:2, :w]
        pb, qb = dec[0:1, w:], dec[1:2, w:]
        k = 1
        while k < ncb:
            xf = xf + jnp.where(row >= k, cmul(pltpu.roll(xf, k, 0), pf, qf), 0.0)
            xb = xb + jnp.where(row < ncb - k, cmul(pltpu.roll(xb, ncb - k, 0), pb, qb), 0.0)
            pf, qf = pf * pf - qf * qf, 2.0 * pf * qf
            pb, qb = pb * pb - qb * qb, 2.0 * pb * qb
            k *= 2
        hf = jnp.where(row >= 1, pltpu.roll(xf, 1, 0), 0.0)
        hb = jnp.where(row < ncb - 1, pltpu.roll(xb, ncb - 1, 0), 0.0)
        h_rows.append(jnp.concatenate([hf, hb], axis=1))
    h = jnp.concatenate(h_rows, axis=0).astype(BF16)
    y = jnp.dot(u, mt_ref[0], preferred_element_type=F32)
    y = y + lax.dot_general(h, fm_ref[0], (((1,), (1,)), ((), ())), preferred_element_type=F32)
    y_ref[0] = y


def _ssm_conv(ug, mt, et, fm, dec, n_batch):
    g, nc, lp = ug.shape
    return pl.pallas_call(
        functools.partial(_ssm_conv_kernel, n_batch=n_batch),
        grid=(g,),
        in_specs=[
            pl.BlockSpec((1, nc, lp), lambda i: (i, 0, 0)),
            pl.BlockSpec((1, lp, lp), lambda i: (i, 0, 0)),
            pl.BlockSpec((1, lp, 4 * SSM_N), lambda i: (i, 0, 0)),
            pl.BlockSpec((1, lp, 4 * SSM_N), lambda i: (i, 0, 0)),
            pl.BlockSpec((1, 2, 4 * SSM_N), lambda i: (i, 0, 0)),
        ],
        out_specs=pl.BlockSpec((1, nc, lp), lambda i: (i, 0, 0)),
        out_shape=jax.ShapeDtypeStruct((g, nc, lp), F32),
        compiler_params=_cparams("parallel"),
        name="ssm_conv",
    )(ug, mt, et, fm, dec)


def _attn_kernel(qt_ref, k_ref, vt_ref, o_ref, q_sc, m_ref, acc_ref, sa_ref, sb_ref, *, tq, tk, nk):
    qt = qt_ref[...]
    for j in range(Q_PER_KV):
        q_sc[:, j * tq:(j + 1) * tq] = qt[j * HEAD_DIM:(j + 1) * HEAD_DIM, :]
    m_ref[...] = jnp.full(m_ref.shape, -jnp.inf, F32)
    acc_ref[...] = jnp.zeros(acc_ref.shape, F32)

    def scores(i):
        kb = k_ref[0, pl.ds(pl.multiple_of(i * tk, tk), tk), :]
        return jnp.dot(kb, q_sc[...], preferred_element_type=F32)

    def accumulate(s_ref, i):
        s = s_ref[...]
        m_prev = m_ref[...]
        m_new = jnp.maximum(m_prev, jnp.max(s, axis=0, keepdims=True))
        alpha = jnp.exp2(m_prev - m_new)
        p = jnp.exp2(s - m_new[0:1, :])
        acc_ref[:V_ROWS, :] = (acc_ref[:V_ROWS, :] * alpha[0:1, :]
                               + jnp.dot(vt_ref[0, i], p.astype(BF16), preferred_element_type=F32))
        m_ref[...] = m_new

    sa_ref[...] = scores(0)

    def body(j, carry):
        sb_ref[...] = scores(2 * j + 1)
        accumulate(sa_ref, 2 * j)
        sa_ref[...] = scores(2 * j + 2)
        accumulate(sb_ref, 2 * j + 1)
        return carry

    lax.fori_loop(0, nk // 2 - 1, body, 0)
    sb_ref[...] = scores(nk - 1)
    accumulate(sa_ref, nk - 2)
    accumulate(sb_ref, nk - 1)
    acc = acc_ref[...]
    out = (acc / acc[HEAD_DIM:HEAD_DIM + 1, :]).T
    o_ref[...] = jnp.concatenate([out[j * tq:(j + 1) * tq, :HEAD_DIM] for j in range(Q_PER_KV)], axis=1)


def _attention(qt, k, vt, n_batch, seq, tq):
    t = qt.shape[1]
    tk = vt.shape[3]
    nq = seq // tq
    nk = seq // tk
    assert nk % 2 == 0, "the key loop handles chunks in pairs"
    w = Q_PER_KV * HEAD_DIM
    return pl.pallas_call(
        functools.partial(_attn_kernel, tq=tq, tk=tk, nk=nk),
        grid=(n_batch, N_KV_HEADS, nq),
        in_specs=[
            pl.BlockSpec((w, tq), lambda b, g, i: (g, b * nq + i)),
            pl.BlockSpec((1, seq, HEAD_DIM), lambda b, g, i: (g, b, 0)),
            pl.BlockSpec((1, nk, V_ROWS, tk), lambda b, g, i: (g, b, 0, 0)),
        ],
        out_specs=pl.BlockSpec((tq, w), lambda b, g, i: (b * nq + i, g)),
        out_shape=jax.ShapeDtypeStruct((t, D_ATTN), F32),
        scratch_shapes=[
            pltpu.VMEM((HEAD_DIM, Q_PER_KV * tq), BF16),
            pltpu.VMEM((8, Q_PER_KV * tq), F32),
            pltpu.VMEM((2 * HEAD_DIM, Q_PER_KV * tq), F32),
            pltpu.VMEM((tk, Q_PER_KV * tq), F32),
            pltpu.VMEM((tk, Q_PER_KV * tq), F32),
        ],
        compiler_params=_cparams("parallel", "parallel", "parallel"),
        name="attention",
    )(qt, k, vt)


def _out_proj_kernel(x_ref, ys_ref, u_ref, at_ref, dsk_ref, wglu_ref, bglu_ref, gs_ref, ga_ref,
                     wo_ref, gf_ref, xo_ref, xnt_ref):
    y = ys_ref[...] + dsk_ref[...] * u_ref[...]
    y = jax.nn.gelu(y)
    z = jnp.dot(y.astype(BF16), wglu_ref[...], preferred_element_type=F32) + bglu_ref[...]
    y = y * jax.nn.sigmoid(z)
    ys = _rms(y, gs_ref[...]).astype(BF16)
    ya = _rms(at_ref[...], ga_ref[...]).astype(BF16)
    xo = (x_ref[...]
          + jnp.dot(ys, wo_ref[:D_SSM, :], preferred_element_type=F32)
          + jnp.dot(ya, wo_ref[D_SSM:, :], preferred_element_type=F32))
    xo_ref[...] = xo
    xnt_ref[...] = _rms(xo, gf_ref[...]).T.astype(BF16)


def _out_proj(x, ys, u, at, dsk, wglu, bglu, gs, ga, wo, gf, tm):
    t = x.shape[0]
    row = lambda w: pl.BlockSpec((tm, w), lambda i: (i, 0))
    full = lambda a, b: pl.BlockSpec((a, b), lambda i: (0, 0))
    return pl.pallas_call(
        _out_proj_kernel,
        grid=(t // tm,),
        in_specs=[row(D_MODEL), row(D_SSM), row(D_SSM), row(D_ATTN), full(1, D_SSM),
                  full(D_SSM, D_SSM), full(1, D_SSM), full(1, D_SSM), full(1, D_ATTN),
                  full(D_MODEL, D_MODEL), full(1, D_MODEL)],
        out_specs=[row(D_MODEL), pl.BlockSpec((D_MODEL, tm), lambda i: (0, i))],
        out_shape=[jax.ShapeDtypeStruct((t, D_MODEL), F32), jax.ShapeDtypeStruct((D_MODEL, t), BF16)],
        compiler_params=_cparams("parallel"),
        name="out_proj",
    )(x, ys, u, at, dsk, wglu, bglu, gs, ga, wo, gf)


def _oe_sort_network(n):
    pairs = []
    p = 1
    while p < n:
        k = p
        while k >= 1:
            for j in range(k % p, n - k, 2 * k):
                for i in range(min(k, n - j - k)):
                    if (i + j) // (2 * p) == (i + j + k) // (2 * p):
                        pairs.append((i + j, i + j + k))
            k //= 2
        p *= 2
    return pairs


def _ce(a, b):
    if b is None:
        return a, None
    if a is None:
        return b, None
    return jnp.maximum(a, b), jnp.minimum(a, b)


def _sorted_top16(slabs):
    n = PEER_TOPK
    v = list(slabs) + [None] * (n - len(slabs))
    for i, j in _oe_sort_network(n):
        v[i], v[j] = _ce(v[i], v[j])
    for shift in (4, 2, 1):
        partner = [None if x is None else pltpu.roll(x, shift, 0) for x in v]
        top = []
        for a in range(n):
            x, y = v[a], partner[n - 1 - a]
            top.append(y if x is None else (x if y is None else jnp.maximum(x, y)))
        v = top
        d = n // 2
        while d >= 1:
            for i in range(n):
                if i & d == 0:
                    v[i], v[i + d] = _ce(v[i], v[i + d])
            d //= 2
    return v


_CAND = [(a, b) for a in range(PEER_TOPK) for b in range(PEER_TOPK) if (a + 1) * (b + 1) <= PEER_TOPK]


def _count_prefix(pred, vals):
    pick = lambda m, a, b: jnp.where(m, a, b)
    m8 = pred(vals[7])
    m4 = pred(pick(m8, vals[11], vals[3]))
    m2 = pred(pick(m8, pick(m4, vals[13], vals[9]), pick(m4, vals[5], vals[1])))
    m1 = pred(pick(m8, pick(m4, pick(m2, vals[14], vals[12]), pick(m2, vals[10], vals[8])),
                   pick(m4, pick(m2, vals[6], vals[4]), pick(m2, vals[2], vals[0]))))
    cnt = pick(m8, 8.0, 0.0) + pick(m4, 4.0, 0.0) + pick(m2, 2.0, 0.0) + pick(m1, 1.0, 0.0)
    return cnt + pick(pred(vals[15]), 1.0, 0.0)


def _bf16_pair_words(x):
    b = pltpu.bitcast(x, jnp.uint32)
    b = (b + jnp.uint32(0x7FFF) + ((b >> 16) & jnp.uint32(1))) >> 16
    return b | (b << 16)


def _peer_route_kernel(xnt_ref, wq_ref, k1_ref, k2_ref, rho_ref, e2_ref, r1_ref, c1_ref):
    def scores(h):
        qt = jnp.dot(wq_ref[h * PEER_KEY_DIM:(h + 1) * PEER_KEY_DIM, :], xnt_ref[...],
                     preferred_element_type=F32)
        return (jnp.dot(k1_ref[h], qt[:PEER_HALF].astype(BF16), preferred_element_type=F32),
                jnp.dot(k2_ref[h], qt[PEER_HALF:].astype(BF16), preferred_element_type=F32))

    s = scores(0)
    for h in range(PEER_HEADS):
        s_next = scores(h + 1) if h + 1 < PEER_HEADS else None
        _peer_route_head(h, s[0], s[1], rho_ref, e2_ref, r1_ref, c1_ref)
        s = s_next


def _peer_route_head(h, s1, s2, rho_ref, e2_ref, r1_ref, c1_ref):
    tm = s1.shape[1]
    nslab = N_KEYS // 8
    s1 = [s1[8 * i:8 * i + 8] for i in range(nslab)]
    s2 = [s2[8 * i:8 * i + 8] for i in range(nslab)]
    v1 = _sorted_top16(s1)
    v2 = _sorted_top16(s2)

    sub = lax.broadcasted_iota(jnp.int32, (8, tm), 0)
    cand = [v1[a] + v2[b] for a, b in _CAND]
    packed = []
    for base in range(0, len(cand), 8):
        slab = None
        for s, c in enumerate(cand[base:base + 8]):
            slab = c if slab is None else jnp.where(sub == s, c, slab)
        if len(cand) - base < 8:
            slab = jnp.where(sub < len(cand) - base, slab, -jnp.inf)
        packed.append(slab)
    theta = _sorted_top16(packed)[PEER_TOPK - 1]

    top = cand[0]
    z = jnp.zeros((8, tm), F32)
    for c in cand:
        z = z + jnp.where(c >= theta, jnp.exp(c - top), 0.0)
    inv_z = 1.0 / z

    r1, c1, rho, e2 = [], [], [], []
    for i in range(nslab):
        r1.append(_count_prefix(lambda v, x=s1[i]: x + v >= theta, v2))
        rho.append(_count_prefix(lambda v, x=s2[i]: v > x, v2))
        c1.append(jnp.exp(s1[i] - v1[0]) * inv_z)
        e2.append(jnp.exp(s2[i] - v2[0]))
    r1 = _bf16_pair_words(jnp.concatenate(r1, axis=0))
    c1 = _bf16_pair_words(jnp.concatenate(c1, axis=0))
    rho = jnp.concatenate(rho, axis=0).astype(BF16)
    e2 = jnp.concatenate(e2, axis=0).astype(BF16)
    for l in range(tm // 128):
        lanes = slice(l * 128, (l + 1) * 128)
        r1_ref[h, l] = r1[:, lanes]
        c1_ref[h, l] = c1[:, lanes]
    for l in range(tm // 256):
        lanes = slice(l * 256, (l + 1) * 256)
        rho_ref[h, l * N_KEYS:(l + 1) * N_KEYS, :] = rho[:, lanes]
        e2_ref[h, l * N_KEYS:(l + 1) * N_KEYS, :] = e2[:, lanes]


def _peer_route(xn_t, wq_t, k1, k2, tm):
    t = xn_t.shape[1]
    sel = pl.BlockSpec((PEER_HEADS, tm // 128, N_KEYS, 128), lambda i: (0, i, 0, 0))
    shp = jax.ShapeDtypeStruct((PEER_HEADS, t // 128, N_KEYS, 128), jnp.uint32)
    tok = pl.BlockSpec((PEER_HEADS, tm // 2, 256), lambda i: (0, i, 0))
    shp16 = jax.ShapeDtypeStruct((PEER_HEADS, t // 2, 256), BF16)
    return pl.pallas_call(
        _peer_route_kernel,
        grid=(t // tm,),
        in_specs=[
            pl.BlockSpec((D_MODEL, tm), lambda i: (0, i)),
            pl.BlockSpec((PEER_HEADS * PEER_KEY_DIM, D_MODEL), lambda i: (0, 0)),
            pl.BlockSpec((PEER_HEADS, N_KEYS, PEER_HALF), lambda i: (0, 0, 0)),
            pl.BlockSpec((PEER_HEADS, N_KEYS, PEER_HALF), lambda i: (0, 0, 0)),
        ],
        out_specs=[tok, tok, sel, sel],
        out_shape=[shp16, shp16, shp, shp],
        compiler_params=_cparams("parallel"),
        name="peer_route",
    )(xn_t, wq_t, k1, k2)


def _gelu_tanh(x):
    k1 = -2.0 * math.sqrt(2.0 / math.pi) * LOG2_E
    t = x * (k1 + (k1 * 0.044715) * (x * x))
    return x / (1.0 + jnp.exp2(t))


def _load_row_repeated(ref, h, l, i, n):
    return ref[h, l, pl.ds(i, n, stride=0), :]


def _peer_dense_kernel(xn_ref, u_ref, vt_ref, rho_ref, e2_ref, r1_ref, c1_ref, o_ref, a_ref, *, rows):
    e = pl.program_id(1)

    @pl.when(e == 0)
    def _():
        o_ref[...] = jnp.zeros(o_ref.shape, F32)

    tt = xn_ref.shape[1]
    sub = 16
    strip = PEER_STRIP
    zero = jnp.zeros((sub, strip), BF16)

    def row_tile(ref, h, s, i):
        return jnp.concatenate(
            [pltpu.bitcast(_load_row_repeated(ref, h, s * (strip // 128) + l, i, sub // 2), BF16)
             for l in range(strip // 128)], axis=1)

    blk = PEER_BLOCK_ROWS * N_KEYS
    nblk = rows // PEER_BLOCK_ROWS

    def first_matmul(k):
        return jnp.dot(u_ref[k * blk:(k + 1) * blk, :].astype(BF16), xn_ref[...], preferred_element_type=F32)

    act = first_matmul(0)
    for k in range(nblk):
        act_next = first_matmul(k + 1) if k + 1 < nblk else None
        a_ref[k * blk:(k + 1) * blk, :] = _gelu_tanh(act).astype(BF16)
        for s in range(tt // strip):
            lanes = slice(s * strip, (s + 1) * strip)
            for ii in range(PEER_BLOCK_ROWS):
                i = k * PEER_BLOCK_ROWS + ii
                rc = [(row_tile(r1_ref, h, s, i), row_tile(c1_ref, h, s, i)) for h in range(PEER_HEADS)]
                for jb in range(N_KEYS // sub):
                    keys = slice(s * N_KEYS + jb * sub, s * N_KEYS + (jb + 1) * sub)
                    w = None
                    for h in range(PEER_HEADS):
                        r, c = rc[h]
                        term = c * jnp.where(rho_ref[h, keys, :] < r, e2_ref[h, keys, :], zero)
                        w = term if w is None else w + term
                    lo = k * blk + ii * N_KEYS + jb * sub
                    a_ref[lo:lo + sub, lanes] = a_ref[lo:lo + sub, lanes] * w
        o_ref[...] += lax.dot_general(vt_ref[k * blk:(k + 1) * blk, :].astype(BF16),
                                      a_ref[k * blk:(k + 1) * blk, :], (((0,), (0,)), ((), ())),
                                      preferred_element_type=F32)
        act = act_next


def _peer_dense(xn_t, eu, ev, layer, rho, e2, r1, c1, tt, te):
    t = xn_t.shape[1]
    ne = eu.shape[1]
    rows = te // N_KEYS
    tok = pl.BlockSpec((PEER_HEADS, tt // 2, 256), lambda j, e: (0, j, 0))
    sel = pl.BlockSpec((PEER_HEADS, tt // 128, rows, 128), lambda j, e: (0, j, e, 0))
    return pl.pallas_call(
        functools.partial(_peer_dense_kernel, rows=rows),
        grid=(t // tt, ne // te),
        in_specs=[
            pl.BlockSpec((D_MODEL, tt), lambda j, e: (0, j)),
            pl.BlockSpec((None, te, D_MODEL), lambda j, e: (layer, e, 0)),
            pl.BlockSpec((None, te, D_MODEL), lambda j, e: (layer, e, 0)),
            tok, tok, sel, sel,
        ],
        out_specs=pl.BlockSpec((D_MODEL, tt), lambda j, e: (0, j)),
        out_shape=jax.ShapeDtypeStruct((D_MODEL, t), F32),
        scratch_shapes=[pltpu.VMEM((te, tt), BF16)],
        compiler_params=_cparams("parallel", "arbitrary"),
        name="peer_dense",
    )(xn_t, eu, ev, rho, e2, r1, c1)


def _ple_kernel(x_ref, pe_ref, p_ref, g_ref, wg_ref, wp_ref, gfin_ref, o_ref, *, final):
    x = x_ref[...] + pe_ref[...].T
    gate = jax.nn.sigmoid(jnp.dot(_rms(x, g_ref[...]).astype(BF16), wg_ref[...], preferred_element_type=F32))
    x = x + gate * jnp.dot(p_ref[...].astype(BF16), wp_ref[...], preferred_element_type=F32)
    if final:
        x = _rms(x, gfin_ref[...])
    o_ref[...] = x


def _ple(x, pe, p, g, wg, wp, gfin, final, tm):
    t = x.shape[0]
    row = lambda w: pl.BlockSpec((tm, w), lambda i: (i, 0))
    full = lambda a, b: pl.BlockSpec((a, b), lambda i: (0, 0))
    return pl.pallas_call(
        functools.partial(_ple_kernel, final=final),
        grid=(t // tm,),
        in_specs=[row(D_MODEL), pl.BlockSpec((D_MODEL, tm), lambda i: (0, i)), row(D_PLE), full(1, D_MODEL),
                  full(D_MODEL, D_MODEL), full(D_PLE, D_MODEL), full(1, D_MODEL)],
        out_specs=row(D_MODEL),
        out_shape=jax.ShapeDtypeStruct((t, D_MODEL), F32),
        compiler_params=_cparams("parallel"),
        name="ple",
    )(x, pe, p, g, wg, wp, gfin)


def _rope_tables(seq):
    n_rows = seq // GRID_W
    row = jnp.repeat(jnp.arange(n_rows, dtype=jnp.int32), GRID_W).astype(F32)
    col = jnp.tile(jnp.arange(GRID_W, dtype=jnp.int32), n_rows).astype(F32)
    quarter = HEAD_DIM // 4
    inv = ROPE_THETA ** (-jnp.arange(quarter, dtype=F32) / quarter)
    ang_r = row[:, None] * inv[None, :]
    ang_c = col[:, None] * inv[None, :]
    cos = jnp.concatenate([jnp.cos(ang_r)] * 2 + [jnp.cos(ang_c)] * 2, axis=1)
    sin = jnp.concatenate([-jnp.sin(ang_r), jnp.sin(ang_r), -jnp.sin(ang_c), jnp.sin(ang_c)], axis=1)
    return jnp.tile(cos, (1, N_Q_HEADS)), jnp.tile(sin, (1, N_Q_HEADS))


def _forward(x, p, norm_mix_g, w_in, lam_re, lam_im, log_dt, b_re, b_im, c_re, c_im, d_skip, w_glu,
             b_glu, q_norm_g, k_norm_g, ssm_out_g, attn_out_g, w_out, norm_ffn_g, w_query, sub_keys1,
             sub_keys2, expert_u, expert_v, norm_ple_g, w_ple_gate, w_ple_proj, final_norm_g):
    n_batch, seq, _ = x.shape
    depth = w_in.shape[0]
    t = n_batch * seq
    tm = min(512, seq)
    tq = min(512, seq)
    tk = min(512, seq // 2)
    tt = min(512, t)
    te = 2048
    nc = t // SSM_CHUNK

    cq, sq = _rope_tables(seq)
    bd = jnp.kron(jnp.eye(N_Q_HEADS, dtype=F32), jnp.ones((HEAD_DIM, HEAD_DIM), F32)).astype(BF16)
    row = lambda a: a.reshape(1, -1).astype(F32)

    xf = x.reshape(t, D_MODEL)
    for i in range(depth):
        u, qt, k, vt = _in_proj(xf, row(norm_mix_g[i]), w_in[i].astype(BF16), cq, sq,
                                row(jnp.tile(q_norm_g[i], N_Q_HEADS)), row(jnp.tile(k_norm_g[i], N_KV_HEADS)),
                                bd, seq, tk)

        grp = lambda a: jnp.transpose(a, (1, 0, 2))[:, :, None, :]
        ldt = jnp.broadcast_to(log_dt[i][:, :, None], (2, SSM_G, SSM_N))
        mt, et, fm, dec = _ssm_params(
            grp(lam_re[i]), grp(lam_im[i]), grp(ldt),
            jnp.transpose(b_re[i], (1, 0, 3, 2)), jnp.transpose(b_im[i], (1, 0, 3, 2)),
            jnp.transpose(c_re[i], (1, 0, 2, 3)), jnp.transpose(c_im[i], (1, 0, 2, 3)))
        ug = (u.astype(BF16).reshape(nc, SSM_CHUNK, SSM_G, SSM_P)
              .transpose(2, 0, 1, 3).reshape(SSM_G, nc, SSM_LP))
        yg = _ssm_conv(ug, mt, et, fm, dec, n_batch)
        y_ssm = yg.reshape(SSM_G, nc, SSM_CHUNK, SSM_P).transpose(1, 2, 0, 3).reshape(t, D_SSM)

        attn = _attention(qt, k, vt, n_batch, seq, tq)

        xf, xnt = _out_proj(xf, y_ssm, u, attn, row(d_skip[i]), w_glu[i].astype(BF16), row(b_glu[i]),
                            row(ssm_out_g[i]), row(attn_out_g[i]), w_out[i].astype(BF16),
                            row(norm_ffn_g[i]), tm)

        rho, e2, r1, c1 = _peer_route(xnt, w_query[i].T.astype(BF16), sub_keys1[i].astype(BF16),
                                      sub_keys2[i].astype(BF16), min(256, t))
        peer_t = _peer_dense(xnt, expert_u, expert_v, i,
                             rho, e2, r1, c1, tt, te)

        xf = _ple(xf, peer_t, p[i].reshape(t, D_PLE), row(norm_ple_g[i]), w_ple_gate[i].astype(BF16),
                  w_ple_proj[i].astype(BF16), row(final_norm_g), i == depth - 1, tm)
    return xf.reshape(n_batch, seq, D_MODEL)


def kernel(x, p, norm_mix_g, w_in, lam_re, lam_im, log_dt, b_re, b_im, c_re, c_im, d_skip, w_glu, b_glu, q_norm_g, k_norm_g, ssm_out_g, attn_out_g, w_out, norm_ffn_g, w_query, sub_keys1, sub_keys2, expert_u, expert_v, norm_ple_g, w_ple_gate, w_ple_proj, final_norm_g):
    return _forward(x, p, norm_mix_g, w_in, lam_re, lam_im, log_dt, b_re, b_im, c_re, c_im, d_skip, w_glu,
                    b_glu, q_norm_g, k_norm_g, ssm_out_g, attn_out_g, w_out, norm_ffn_g, w_query,
                    sub_keys1, sub_keys2, expert_u, expert_v, norm_ple_g, w_ple_gate, w_ple_proj,
                    final_norm_g)
```

```python
import functools
import math

import jax
import jax.numpy as jnp
from jax import lax
from jax.experimental import pallas as pl
from jax.experimental.pallas import tpu as pltpu

F32 = jnp.float32
BF16 = jnp.bfloat16

D_MODEL = 1024
D_PLE = 256
GRID_W = 64
EPS = 1e-6
LOG2_E = 1.4426950408889634
D_SSM = 512
SSM_P = 16
SSM_G = D_SSM // SSM_P
SSM_N = 64
D_ATTN = 512
HEAD_DIM = 64
V_ROWS = HEAD_DIM + 16
N_Q_HEADS = 8
N_KV_HEADS = 2
Q_PER_KV = 4
D_KV = 128
D_IN = D_SSM + D_ATTN + 2 * D_KV
ROPE_THETA = 10000.0
PEER_HEADS = 8
PEER_KEY_DIM = 256
PEER_HALF = 128
N_KEYS = 128
PEER_TOPK = 16

PEER_BLOCK_ROWS = 8
PEER_STRIP = 256
SSM_CHUNK = 32
SSM_LP = SSM_CHUNK * SSM_P
VMEM_LIMIT = 48 * 1024 * 1024

def _cparams(*sem):
    return pltpu.CompilerParams(dimension_semantics=sem, vmem_limit_bytes=VMEM_LIMIT)


def _rms(x, g):
    return x * lax.rsqrt(jnp.mean(x * x, axis=-1, keepdims=True) + EPS) * g


def _in_proj_kernel(x_ref, g_ref, w_ref, cq_ref, sq_ref, gq_ref, gk_ref, bd_ref,
                    u_ref, qt_ref, k_ref, vt_ref):
    h = _rms(x_ref[...], g_ref[...])
    z = jnp.dot(h.astype(BF16), w_ref[...], preferred_element_type=F32)
    u_ref[...] = z[:, :D_SSM]

    def head_norm_rope(t, gain, bd, c, s):
        ss = jnp.dot((t * t).astype(BF16), bd, preferred_element_type=F32)
        t = t * lax.rsqrt(ss * (1.0 / HEAD_DIM) + EPS) * gain
        w = t.shape[1]
        lane = lax.broadcasted_iota(jnp.int32, t.shape, 1)
        partner = jnp.where((lane % 32) < 16, pltpu.roll(t, w - 16, 1), pltpu.roll(t, 16, 1))
        return t * c + partner * s

    cq = cq_ref[...]
    sq = sq_ref[...]
    q = head_norm_rope(z[:, D_SSM:D_SSM + D_ATTN], gq_ref[...], bd_ref[...], cq, sq)
    qt_ref[...] = (q * (HEAD_DIM ** -0.5 * LOG2_E)).T.astype(BF16)
    k = head_norm_rope(z[:, D_SSM + D_ATTN:D_SSM + D_ATTN + D_KV], gk_ref[...],
                       bd_ref[:D_KV, :D_KV], cq[:, :D_KV], sq[:, :D_KV])
    vt = z[:, D_SSM + D_ATTN + D_KV:].T
    row = lax.broadcasted_iota(jnp.int32, (V_ROWS - HEAD_DIM, vt.shape[1]), 0)
    ones_row = jnp.where(row == 0, 1.0, 0.0).astype(F32)
    for j in range(N_KV_HEADS):
        k_ref[j] = k[:, j * HEAD_DIM:(j + 1) * HEAD_DIM].astype(BF16)
        vt_ref[j, 0] = jnp.concatenate([vt[j * HEAD_DIM:(j + 1) * HEAD_DIM], ones_row], axis=0).astype(BF16)


def _in_proj(x, g, w, cq, sq, gq, gk, bd, seq, tm):
    t = x.shape[0]
    nseq = seq // tm
    return pl.pallas_call(
        _in_proj_kernel,
        grid=(t // tm,),
        in_specs=[
            pl.BlockSpec((tm, D_MODEL), lambda i: (i, 0)),
            pl.BlockSpec((1, D_MODEL), lambda i: (0, 0)),
            pl.BlockSpec((D_MODEL, D_IN), lambda i: (0, 0)),
            pl.BlockSpec((tm, D_ATTN), lambda i: (i % nseq, 0)),
            pl.BlockSpec((tm, D_ATTN), lambda i: (i % nseq, 0)),
            pl.BlockSpec((1, D_ATTN), lambda i: (0, 0)),
            pl.BlockSpec((1, D_KV), lambda i: (0, 0)),
            pl.BlockSpec((D_ATTN, D_ATTN), lambda i: (0, 0)),
        ],
        out_specs=[
            pl.BlockSpec((tm, D_SSM), lambda i: (i, 0)),
            pl.BlockSpec((D_ATTN, tm), lambda i: (0, i)),
            pl.BlockSpec((N_KV_HEADS, tm, HEAD_DIM), lambda i: (0, i, 0)),
            pl.BlockSpec((N_KV_HEADS, 1, V_ROWS, tm), lambda i: (0, i, 0, 0)),
        ],
        out_shape=[
            jax.ShapeDtypeStruct((t, D_SSM), F32),
            jax.ShapeDtypeStruct((D_ATTN, t), BF16),
            jax.ShapeDtypeStruct((N_KV_HEADS, t, HEAD_DIM), BF16),
            jax.ShapeDtypeStruct((N_KV_HEADS, t // tm, V_ROWS, tm), BF16),
        ],
        compiler_params=_cparams("parallel"),
        name="in_proj",
    )(x, g, w, cq, sq, gq, gk, bd)


def _ssm_params_kernel(lr_ref, li_ref, ldt_ref, bt_re_ref, bt_im_ref, c_re_ref, c_im_ref,
                       mt_ref, et_ref, fm_ref, dec_ref):
    L = SSM_CHUNK
    hi = lax.Precision.HIGHEST
    r_parts, e_parts, f_parts, p_parts, q_parts = [], [], [], [], []
    idx = lax.broadcasted_iota(jnp.int32, (L, 1, SSM_N), 0).astype(F32)
    for d in range(2):
        lr = lr_ref[0, d]
        li = li_ref[0, d]
        dt = jnp.exp(ldt_ref[0, d])

        def apow(tau):
            mag = jnp.exp(tau * (lr * dt))
            ang = tau * (li * dt)
            return mag * jnp.cos(ang), mag * jnp.sin(ang)

        ab_re, ab_im = apow(jnp.ones((1, SSM_N), F32))
        den = lr * lr + li * li
        nr = ab_re - 1.0
        f_re = (nr * lr + ab_im * li) / den
        f_im = (ab_im * lr - nr * li) / den
        bt_re = bt_re_ref[0, d]
        bt_im = bt_im_ref[0, d]
        bb_re = f_re * bt_re - f_im * bt_im
        bb_im = f_re * bt_im + f_im * bt_re
        c_re = c_re_ref[0, d]
        c_im = c_im_ref[0, d]

        tau_k = idx if d == 0 else (L - 1.0) - idx
        pr, pi = apow(tau_k)
        lm_re = (c_re[None] * pr - c_im[None] * pi).reshape(L * SSM_P, SSM_N)
        lm_im = (c_re[None] * pi + c_im[None] * pr).reshape(L * SSM_P, SSM_N)
        nt = (((1,), (1,)), ((), ()))
        r_parts.append(lax.dot_general(bb_re, lm_re, nt, precision=hi, preferred_element_type=F32)
                       - lax.dot_general(bb_im, lm_im, nt, precision=hi, preferred_element_type=F32))

        tau_e = (L - 1.0) - idx if d == 0 else idx
        pr, pi = apow(tau_e)
        e_parts.append((pr * bb_re[None] - pi * bb_im[None]).reshape(L * SSM_P, SSM_N))
        e_parts.append((pr * bb_im[None] + pi * bb_re[None]).reshape(L * SSM_P, SSM_N))

        tau_f = idx + 1.0 if d == 0 else L - idx
        pr, pi = apow(tau_f)
        f_parts.append((c_re[None] * pr - c_im[None] * pi).reshape(L * SSM_P, SSM_N))
        f_parts.append(-(c_re[None] * pi + c_im[None] * pr).reshape(L * SSM_P, SSM_N))

        al_re, al_im = apow(jnp.full((1, SSM_N), float(L), F32))
        p_parts += [al_re, al_re]
        q_parts += [-al_im, al_im]

    zeros = jnp.zeros((SSM_P, SSM_LP), F32)
    r_bwd = jnp.concatenate([r_parts[1], zeros], axis=1)
    r_fwd = pltpu.roll(jnp.concatenate([r_parts[0], zeros], axis=1), SSM_LP - SSM_P, 1)
    r_cat = r_bwd + r_fwd
    for s in range(L):
        shift = (2 * SSM_LP - (L - 1 - s) * SSM_P) % (2 * SSM_LP)
        win = r_cat if shift == 0 else pltpu.roll(r_cat, shift, 1)
        mt_ref[0, s * SSM_P:(s + 1) * SSM_P, :] = win[:, :SSM_LP].astype(BF16)
    et_ref[0] = jnp.concatenate(e_parts, axis=1).astype(BF16)
    fm_ref[0] = jnp.concatenate(f_parts, axis=1).astype(BF16)
    dec_ref[0] = jnp.concatenate([jnp.concatenate(p_parts, axis=1),
                                  jnp.concatenate(q_parts, axis=1)], axis=0)


def _ssm_params(lr, li, ldt, bt_re, bt_im, c_re, c_im):
    g = lr.shape[0]
    vec = pl.BlockSpec((1, 2, 1, SSM_N), lambda i: (i, 0, 0, 0))
    mat = pl.BlockSpec((1, 2, SSM_P, SSM_N), lambda i: (i, 0, 0, 0))
    return pl.pallas_call(
        _ssm_params_kernel,
        grid=(g,),
        in_specs=[vec, vec, vec, mat, mat, mat, mat],
        out_specs=[
            pl.BlockSpec((1, SSM_LP, SSM_LP), lambda i: (i, 0, 0)),
            pl.BlockSpec((1, SSM_LP, 4 * SSM_N), lambda i: (i, 0, 0)),
            pl.BlockSpec((1, SSM_LP, 4 * SSM_N), lambda i: (i, 0, 0)),
            pl.BlockSpec((1, 2, 4 * SSM_N), lambda i: (i, 0, 0)),
        ],
        out_shape=[
            jax.ShapeDtypeStruct((g, SSM_LP, SSM_LP), BF16),
            jax.ShapeDtypeStruct((g, SSM_LP, 4 * SSM_N), BF16),
            jax.ShapeDtypeStruct((g, SSM_LP, 4 * SSM_N), BF16),
            jax.ShapeDtypeStruct((g, 2, 4 * SSM_N), F32),
        ],
        compiler_params=_cparams("parallel"),
        name="ssm_params",
    )(lr, li, ldt, bt_re, bt_im, c_re, c_im)


def _ssm_conv_kernel(u_ref, mt_ref, et_ref, fm_ref, dec_ref, y_ref, *, n_batch):
    u = u_ref[0]
    nc = u.shape[0]
    ncb = nc // n_batch
    w = 2 * SSM_N
    s_all = jnp.dot(u, et_ref[0], preferred_element_type=F32)
    dec = dec_ref[0]
    row = lax.broadcasted_iota(jnp.int32, (ncb, w), 0)

    def cmul(x, p, q):
        return p * x + q * pltpu.roll(x, SSM_N, 1)

    h_rows = []
    for b in range(n_batch):
        s_b = s_all[b * ncb:(b + 1) * ncb]
        xf, xb = s_b[:, :w], s_b[:, w:]
        pf, qf = dec[0:1, :w], dec[1:2, :w]
        pb, qb = dec[0:1, w:], dec[1:2, w:]
        k = 1
        while k < ncb:
            xf = xf + jnp.where(row >= k, cmul(pltpu.roll(xf, k, 0), pf, qf), 0.0)
            xb = xb + jnp.where(row < ncb - k, cmul(pltpu.roll(xb, ncb - k, 0), pb, qb), 0.0)
            pf, qf = pf * pf - qf * qf, 2.0 * pf * qf
            pb, qb = pb * pb - qb * qb, 2.0 * pb * qb
            k *= 2
        hf = jnp.where(row >= 1, pltpu.roll(xf, 1, 0), 0.0)
        hb = jnp.where(row < ncb - 1, pltpu.roll(xb, ncb - 1, 0), 0.0)
        h_rows.append(jnp.concatenate([hf, hb], axis=1))
    h = jnp.concatenate(h_rows, axis=0).astype(BF16)
    y = jnp.dot(u, mt_ref[0], preferred_element_type=F32)
    y = y + lax.dot_general(h, fm_ref[0], (((1,), (1,)), ((), ())), preferred_element_type=F32)
    y_ref[0] = y


def _ssm_conv(ug, mt, et, fm, dec, n_batch):
    g, nc, lp = ug.shape
    return pl.pallas_call(
        functools.partial(_ssm_conv_kernel, n_batch=n_batch),
        grid=(g,),
        in_specs=[
            pl.BlockSpec((1, nc, lp), lambda i: (i, 0, 0)),
            pl.BlockSpec((1, lp, lp), lambda i: (i, 0, 0)),
            pl.BlockSpec((1, lp, 4 * SSM_N), lambda i: (i, 0, 0)),
            pl.BlockSpec((1, lp, 4 * SSM_N), lambda i: (i, 0, 0)),
            pl.BlockSpec((1, 2, 4 * SSM_N), lambda i: (i, 0, 0)),
        ],
        out_specs=pl.BlockSpec((1, nc, lp), lambda i: (i, 0, 0)),
        out_shape=jax.ShapeDtypeStruct((g, nc, lp), F32),
        compiler_params=_cparams("parallel"),
        name="ssm_conv",
    )(ug, mt, et, fm, dec)


def _attn_kernel(qt_ref, k_ref, vt_ref, o_ref, q_sc, m_ref, acc_ref, sa_ref, sb_ref, *, tq, tk, nk):
    qt = qt_ref[...]
    for j in range(Q_PER_KV):
        q_sc[:, j * tq:(j + 1) * tq] = qt[j * HEAD_DIM:(j + 1) * HEAD_DIM, :]
    m_ref[...] = jnp.full(m_ref.shape, -jnp.inf, F32)
    acc_ref[...] = jnp.zeros(acc_ref.shape, F32)

    def scores(i):
        kb = k_ref[0, pl.ds(pl.multiple_of(i * tk, tk), tk), :]
        return jnp.dot(kb, q_sc[...], preferred_element_type=F32)

    def accumulate(s_ref, i):
        s = s_ref[...]
        m_prev = m_ref[...]
        m_new = jnp.maximum(m_prev, jnp.max(s, axis=0, keepdims=True))
        alpha = jnp.exp2(m_prev - m_new)
        p = jnp.exp2(s - m_new[0:1, :])
        acc_ref[:V_ROWS, :] = (acc_ref[:V_ROWS, :] * alpha[0:1, :]
                               + jnp.dot(vt_ref[0, i], p.astype(BF16), preferred_element_type=F32))
        m_ref[...] = m_new

    sa_ref[...] = scores(0)

    def body(j, carry):
        sb_ref[...] = scores(2 * j + 1)
        accumulate(sa_ref, 2 * j)
        sa_ref[...] = scores(2 * j + 2)
        accumulate(sb_ref, 2 * j + 1)
        return carry

    lax.fori_loop(0, nk // 2 - 1, body, 0)
    sb_ref[...] = scores(nk - 1)
    accumulate(sa_ref, nk - 2)
    accumulate(sb_ref, nk - 1)
    acc = acc_ref[...]
    out = (acc / acc[HEAD_DIM:HEAD_DIM + 1, :]).T
    o_ref[...] = jnp.concatenate([out[j * tq:(j + 1) * tq, :HEAD_DIM] for j in range(Q_PER_KV)], axis=1)


def _attention(qt, k, vt, n_batch, seq, tq):
    t = qt.shape[1]
    tk = vt.shape[3]
    nq = seq // tq
    nk = seq // tk
    assert nk % 2 == 0, "the key loop handles chunks in pairs"
    w = Q_PER_KV * HEAD_DIM
    return pl.pallas_call(
        functools.partial(_attn_kernel, tq=tq, tk=tk, nk=nk),
        grid=(n_batch, N_KV_HEADS, nq),
        in_specs=[
            pl.BlockSpec((w, tq), lambda b, g, i: (g, b * nq + i)),
            pl.BlockSpec((1, seq, HEAD_DIM), lambda b, g, i: (g, b, 0)),
            pl.BlockSpec((1, nk, V_ROWS, tk), lambda b, g, i: (g, b, 0, 0)),
        ],
        out_specs=pl.BlockSpec((tq, w), lambda b, g, i: (b * nq + i, g)),
        out_shape=jax.ShapeDtypeStruct((t, D_ATTN), F32),
        scratch_shapes=[
            pltpu.VMEM((HEAD_DIM, Q_PER_KV * tq), BF16),
            pltpu.VMEM((8, Q_PER_KV * tq), F32),
            pltpu.VMEM((2 * HEAD_DIM, Q_PER_KV * tq), F32),
            pltpu.VMEM((tk, Q_PER_KV * tq), F32),
            pltpu.VMEM((tk, Q_PER_KV * tq), F32),
        ],
        compiler_params=_cparams("parallel", "parallel", "parallel"),
        name="attention",
    )(qt, k, vt)


def _out_proj_kernel(x_ref, ys_ref, u_ref, at_ref, dsk_ref, wglu_ref, bglu_ref, gs_ref, ga_ref,
                     wo_ref, gf_ref, xo_ref, xnt_ref):
    y = ys_ref[...] + dsk_ref[...] * u_ref[...]
    y = jax.nn.gelu(y)
    z = jnp.dot(y.astype(BF16), wglu_ref[...], preferred_element_type=F32) + bglu_ref[...]
    y = y * jax.nn.sigmoid(z)
    ys = _rms(y, gs_ref[...]).astype(BF16)
    ya = _rms(at_ref[...], ga_ref[...]).astype(BF16)
    xo = (x_ref[...]
          + jnp.dot(ys, wo_ref[:D_SSM, :], preferred_element_type=F32)
          + jnp.dot(ya, wo_ref[D_SSM:, :], preferred_element_type=F32))
    xo_ref[...] = xo
    xnt_ref[...] = _rms(xo, gf_ref[...]).T.astype(BF16)


def _out_proj(x, ys, u, at, dsk, wglu, bglu, gs, ga, wo, gf, tm):
    t = x.shape[0]
    row = lambda w: pl.BlockSpec((tm, w), lambda i: (i, 0))
    full = lambda a, b: pl.BlockSpec((a, b), lambda i: (0, 0))
    return pl.pallas_call(
        _out_proj_kernel,
        grid=(t // tm,),
        in_specs=[row(D_MODEL), row(D_SSM), row(D_SSM), row(D_ATTN), full(1, D_SSM),
                  full(D_SSM, D_SSM), full(1, D_SSM), full(1, D_SSM), full(1, D_ATTN),
                  full(D_MODEL, D_MODEL), full(1, D_MODEL)],
        out_specs=[row(D_MODEL), pl.BlockSpec((D_MODEL, tm), lambda i: (0, i))],
        out_shape=[jax.ShapeDtypeStruct((t, D_MODEL), F32), jax.ShapeDtypeStruct((D_MODEL, t), BF16)],
        compiler_params=_cparams("parallel"),
        name="out_proj",
    )(x, ys, u, at, dsk, wglu, bglu, gs, ga, wo, gf)


def _oe_sort_network(n):
    pairs = []
    p = 1
    while p < n:
        k = p
        while k >= 1:
            for j in range(k % p, n - k, 2 * k):
                for i in range(min(k, n - j - k)):
                    if (i + j) // (2 * p) == (i + j + k) // (2 * p):
                        pairs.append((i + j, i + j + k))
            k //= 2
        p *= 2
    return pairs


def _ce(a, b):
    if b is None:
        return a, None
    if a is None:
        return b, None
    return jnp.maximum(a, b), jnp.minimum(a, b)


def _sorted_top16(slabs):
    n = PEER_TOPK
    v = list(slabs) + [None] * (n - len(slabs))
    for i, j in _oe_sort_network(n):
        v[i], v[j] = _ce(v[i], v[j])
    for shift in (4, 2, 1):
        partner = [None if x is None else pltpu.roll(x, shift, 0) for x in v]
        top = []
        for a in range(n):
            x, y = v[a], partner[n - 1 - a]
            top.append(y if x is None else (x if y is None else jnp.maximum(x, y)))
        v = top
        d = n // 2
        while d >= 1:
            for i in range(n):
                if i & d == 0:
                    v[i], v[i + d] = _ce(v[i], v[i + d])
            d //= 2
    return v


_CAND = [(a, b) for a in range(PEER_TOPK) for b in range(PEER_TOPK) if (a + 1) * (b + 1) <= PEER_TOPK]


def _count_prefix(pred, vals):
    pick = lambda m, a, b: jnp.where(m, a, b)
    m8 = pred(vals[7])
    m4 = pred(pick(m8, vals[11], vals[3]))
    m2 = pred(pick(m8, pick(m4, vals[13], vals[9]), pick(m4, vals[5], vals[1])))
    m1 = pred(pick(m8, pick(m4, pick(m2, vals[14], vals[12]), pick(m2, vals[10], vals[8])),
                   pick(m4, pick(m2, vals[6], vals[4]), pick(m2, vals[2], vals[0]))))
    cnt = pick(m8, 8.0, 0.0) + pick(m4, 4.0, 0.0) + pick(m2, 2.0, 0.0) + pick(m1, 1.0, 0.0)
    return cnt + pick(pred(vals[15]), 1.0, 0.0)


def _bf16_pair_words(x):
    b = pltpu.bitcast(x, jnp.uint32)
    b = (b + jnp.uint32(0x7FFF) + ((b >> 16) & jnp.uint32(1))) >> 16
    return b | (b << 16)


def _peer_route_kernel(xnt_ref, wq_ref, k1_ref, k2_ref, rho_ref, e2_ref, r1_ref, c1_ref):
    def scores(h):
        qt = jnp.dot(wq_ref[h * PEER_KEY_DIM:(h + 1) * PEER_KEY_DIM, :], xnt_ref[...],
                     preferred_element_type=F32)
        return (jnp.dot(k1_ref[h], qt[:PEER_HALF].astype(BF16), preferred_element_type=F32),
                jnp.dot(k2_ref[h], qt[PEER_HALF:].astype(BF16), preferred_element_type=F32))

    s = scores(0)
    for h in range(PEER_HEADS):
        s_next = scores(h + 1) if h + 1 < PEER_HEADS else None
        _peer_route_head(h, s[0], s[1], rho_ref, e2_ref, r1_ref, c1_ref)
        s = s_next


def _peer_route_head(h, s1, s2, rho_ref, e2_ref, r1_ref, c1_ref):
    tm = s1.shape[1]
    nslab = N_KEYS // 8
    s1 = [s1[8 * i:8 * i + 8] for i in range(nslab)]
    s2 = [s2[8 * i:8 * i + 8] for i in range(nslab)]
    v1 = _sorted_top16(s1)
    v2 = _sorted_top16(s2)

    sub = lax.broadcasted_iota(jnp.int32, (8, tm), 0)
    cand = [v1[a] + v2[b] for a, b in _CAND]
    packed = []
    for base in range(0, len(cand), 8):
        slab = None
        for s, c in enumerate(cand[base:base + 8]):
            slab = c if slab is None else jnp.where(sub == s, c, slab)
        if len(cand) - base < 8:
            slab = jnp.where(sub < len(cand) - base, slab, -jnp.inf)
        packed.append(slab)
    best = _sorted_top16(packed)
    theta = best[PEER_TOPK - 1]
    z = jnp.ones((8, tm), F32)
    for c in best[1:]:
        z = z + jnp.exp(c - best[0])
    inv_z = 1.0 / z

    r1, c1, rho, e2 = [], [], [], []
    for i in range(nslab):
        r1.append(_count_prefix(lambda v, x=s1[i]: x + v >= theta, v2))
        rho.append(_count_prefix(lambda v, x=s2[i]: v > x, v2))
        c1.append(jnp.exp(s1[i] - v1[0]) * inv_z)
        e2.append(jnp.exp(s2[i] - v2[0]))
    r1 = _bf16_pair_words(jnp.concatenate(r1, axis=0))
    c1 = _bf16_pair_words(jnp.concatenate(c1, axis=0))
    rho = jnp.concatenate(rho, axis=0).astype(BF16)
    e2 = jnp.concatenate(e2, axis=0).astype(BF16)
    for l in range(tm // 128):
        lanes = slice(l * 128, (l + 1) * 128)
        r1_ref[h, l] = r1[:, lanes]
        c1_ref[h, l] = c1[:, lanes]
    for l in range(tm // 256):
        lanes = slice(l * 256, (l + 1) * 256)
        rho_ref[h, l * N_KEYS:(l + 1) * N_KEYS, :] = rho[:, lanes]
        e2_ref[h, l * N_KEYS:(l + 1) * N_KEYS, :] = e2[:, lanes]


def _peer_route(xn_t, wq_t, k1, k2, tm):
    t = xn_t.shape[1]
    sel = pl.BlockSpec((PEER_HEADS, tm // 128, N_KEYS, 128), lambda i: (0, i, 0, 0))
    shp = jax.ShapeDtypeStruct((PEER_HEADS, t // 128, N_KEYS, 128), jnp.uint32)
    tok = pl.BlockSpec((PEER_HEADS, tm // 2, 256), lambda i: (0, i, 0))
    shp16 = jax.ShapeDtypeStruct((PEER_HEADS, t // 2, 256), BF16)
    return pl.pallas_call(
        _peer_route_kernel,
        grid=(t // tm,),
        in_specs=[
            pl.BlockSpec((D_MODEL, tm), lambda i: (0, i)),
            pl.BlockSpec((PEER_HEADS * PEER_KEY_DIM, D_MODEL), lambda i: (0, 0)),
            pl.BlockSpec((PEER_HEADS, N_KEYS, PEER_HALF), lambda i: (0, 0, 0)),
            pl.BlockSpec((PEER_HEADS, N_KEYS, PEER_HALF), lambda i: (0, 0, 0)),
        ],
        out_specs=[tok, tok, sel, sel],
        out_shape=[shp16, shp16, shp, shp],
        compiler_params=_cparams("parallel"),
        name="peer_route",
    )(xn_t, wq_t, k1, k2)


def _gelu_tanh(x):
    k1 = -2.0 * math.sqrt(2.0 / math.pi) * LOG2_E
    t = x * (k1 + (k1 * 0.044715) * (x * x))
    return x / (1.0 + jnp.exp2(t))


def _load_row_repeated(ref, h, l, i, n):
    return ref[h, l, pl.ds(i, n, stride=0), :]


def _peer_dense_kernel(xn_ref, u_ref, vt_ref, rho_ref, e2_ref, r1_ref, c1_ref, o_ref, a_ref, *, rows):
    e = pl.program_id(1)

    @pl.when(e == 0)
    def _():
        o_ref[...] = jnp.zeros(o_ref.shape, F32)

    tt = xn_ref.shape[1]
    sub = 16
    strip = PEER_STRIP
    zero = jnp.zeros((sub, strip), BF16)

    def row_tile(ref, h, s, i):
        return jnp.concatenate(
            [pltpu.bitcast(_load_row_repeated(ref, h, s * (strip // 128) + l, i, sub // 2), BF16)
             for l in range(strip // 128)], axis=1)

    blk = PEER_BLOCK_ROWS * N_KEYS
    nblk = rows // PEER_BLOCK_ROWS

    def first_matmul(k):
        return jnp.dot(u_ref[k * blk:(k + 1) * blk, :].astype(BF16), xn_ref[...], preferred_element_type=F32)

    act = first_matmul(0)
    for k in range(nblk):
        act_next = first_matmul(k + 1) if k + 1 < nblk else None
        a_ref[k * blk:(k + 1) * blk, :] = _gelu_tanh(act).astype(BF16)
        for s in range(tt // strip):
            lanes = slice(s * strip, (s + 1) * strip)
            for ii in range(PEER_BLOCK_ROWS):
                i = k * PEER_BLOCK_ROWS + ii
                rc = [(row_tile(r1_ref, h, s, i), row_tile(c1_ref, h, s, i)) for h in range(PEER_HEADS)]
                for jb in range(N_KEYS // sub):
                    keys = slice(s * N_KEYS + jb * sub, s * N_KEYS + (jb + 1) * sub)
                    w = None
                    for h in range(PEER_HEADS):
                        r, c = rc[h]
                        term = c * jnp.where(rho_ref[h, keys, :] < r, e2_ref[h, keys, :], zero)
                        w = term if w is None else w + term
                    lo = k * blk + ii * N_KEYS + jb * sub
                    a_ref[lo:lo + sub, lanes] = a_ref[lo:lo + sub, lanes] * w
        o_ref[...] += lax.dot_general(vt_ref[k * blk:(k + 1) * blk, :].astype(BF16),
                                      a_ref[k * blk:(k + 1) * blk, :], (((0,), (0,)), ((), ())),
                                      preferred_element_type=F32)
        act = act_next


def _peer_dense(xn_t, eu, ev, layer, rho, e2, r1, c1, tt, te):
    t = xn_t.shape[1]
    ne = eu.shape[1]
    rows = te // N_KEYS
    tok = pl.BlockSpec((PEER_HEADS, tt // 2, 256), lambda j, e: (0, j, 0))
    sel = pl.BlockSpec((PEER_HEADS, tt // 128, rows, 128), lambda j, e: (0, j, e, 0))
    return pl.pallas_call(
        functools.partial(_peer_dense_kernel, rows=rows),
        grid=(t // tt, ne // te),
        in_specs=[
            pl.BlockSpec((D_MODEL, tt), lambda j, e: (0, j)),
            pl.BlockSpec((None, te, D_MODEL), lambda j, e: (layer, e, 0)),
            pl.BlockSpec((None, te, D_MODEL), lambda j, e: (layer, e, 0)),
            tok, tok, sel, sel,
        ],
        out_specs=pl.BlockSpec((D_MODEL, tt), lambda j, e: (0, j)),
        out_shape=jax.ShapeDtypeStruct((D_MODEL, t), F32),
        scratch_shapes=[pltpu.VMEM((te, tt), BF16)],
        compiler_params=_cparams("parallel", "arbitrary"),
        name="peer_dense",
    )(xn_t, eu, ev, rho, e2, r1, c1)


def _ple_kernel(x_ref, pe_ref, p_ref, g_ref, wg_ref, wp_ref, gfin_ref, o_ref, *, final):
    x = x_ref[...] + pe_ref[...].T
    gate = jax.nn.sigmoid(jnp.dot(_rms(x, g_ref[...]).astype(BF16), wg_ref[...], preferred_element_type=F32))
    x = x + gate * jnp.dot(p_ref[...].astype(BF16), wp_ref[...], preferred_element_type=F32)
    if final:
        x = _rms(x, gfin_ref[...])
    o_ref[...] = x


def _ple(x, pe, p, g, wg, wp, gfin, final, tm):
    t = x.shape[0]
    row = lambda w: pl.BlockSpec((tm, w), lambda i: (i, 0))
    full = lambda a, b: pl.BlockSpec((a, b), lambda i: (0, 0))
    return pl.pallas_call(
        functools.partial(_ple_kernel, final=final),
        grid=(t // tm,),
        in_specs=[row(D_MODEL), pl.BlockSpec((D_MODEL, tm), lambda i: (0, i)), row(D_PLE), full(1, D_MODEL),
                  full(D_MODEL, D_MODEL), full(D_PLE, D_MODEL), full(1, D_MODEL)],
        out_specs=row(D_MODEL),
        out_shape=jax.ShapeDtypeStruct((t, D_MODEL), F32),
        compiler_params=_cparams("parallel"),
        name="ple",
    )(x, pe, p, g, wg, wp, gfin)


def _rope_tables(seq):
    n_rows = seq // GRID_W
    row = jnp.repeat(jnp.arange(n_rows, dtype=jnp.int32), GRID_W).astype(F32)
    col = jnp.tile(jnp.arange(GRID_W, dtype=jnp.int32), n_rows).astype(F32)
    quarter = HEAD_DIM // 4
    inv = ROPE_THETA ** (-jnp.arange(quarter, dtype=F32) / quarter)
    ang_r = row[:, None] * inv[None, :]
    ang_c = col[:, None] * inv[None, :]
    cos = jnp.concatenate([jnp.cos(ang_r)] * 2 + [jnp.cos(ang_c)] * 2, axis=1)
    sin = jnp.concatenate([-jnp.sin(ang_r), jnp.sin(ang_r), -jnp.sin(ang_c), jnp.sin(ang_c)], axis=1)
    return jnp.tile(cos, (1, N_Q_HEADS)), jnp.tile(sin, (1, N_Q_HEADS))


def _forward(x, p, norm_mix_g, w_in, lam_re, lam_im, log_dt, b_re, b_im, c_re, c_im, d_skip, w_glu,
             b_glu, q_norm_g, k_norm_g, ssm_out_g, attn_out_g, w_out, norm_ffn_g, w_query, sub_keys1,
             sub_keys2, expert_u, expert_v, norm_ple_g, w_ple_gate, w_ple_proj, final_norm_g):
    n_batch, seq, _ = x.shape
    depth = w_in.shape[0]
    t = n_batch * seq
    tm = min(512, seq)
    tq = min(512, seq)
    tk = min(512, seq // 2)
    tt = min(512, t)
    te = 2048
    nc = t // SSM_CHUNK

    cq, sq = _rope_tables(seq)
    bd = jnp.kron(jnp.eye(N_Q_HEADS, dtype=F32), jnp.ones((HEAD_DIM, HEAD_DIM), F32)).astype(BF16)
    row = lambda a: a.reshape(1, -1).astype(F32)

    xf = x.reshape(t, D_MODEL)
    for i in range(depth):
        u, qt, k, vt = _in_proj(xf, row(norm_mix_g[i]), w_in[i].astype(BF16), cq, sq,
                                row(jnp.tile(q_norm_g[i], N_Q_HEADS)), row(jnp.tile(k_norm_g[i], N_KV_HEADS)),
                                bd, seq, tk)

        grp = lambda a: jnp.transpose(a, (1, 0, 2))[:, :, None, :]
        ldt = jnp.broadcast_to(log_dt[i][:, :, None], (2, SSM_G, SSM_N))
        mt, et, fm, dec = _ssm_params(
            grp(lam_re[i]), grp(lam_im[i]), grp(ldt),
            jnp.transpose(b_re[i], (1, 0, 3, 2)), jnp.transpose(b_im[i], (1, 0, 3, 2)),
            jnp.transpose(c_re[i], (1, 0, 2, 3)), jnp.transpose(c_im[i], (1, 0, 2, 3)))
        ug = (u.astype(BF16).reshape(nc, SSM_CHUNK, SSM_G, SSM_P)
              .transpose(2, 0, 1, 3).reshape(SSM_G, nc, SSM_LP))
        yg = _ssm_conv(ug, mt, et, fm, dec, n_batch)
        y_ssm = yg.reshape(SSM_G, nc, SSM_CHUNK, SSM_P).transpose(1, 2, 0, 3).reshape(t, D_SSM)

        attn = _attention(qt, k, vt, n_batch, seq, tq)

        xf, xnt = _out_proj(xf, y_ssm, u, attn, row(d_skip[i]), w_glu[i].astype(BF16), row(b_glu[i]),
                            row(ssm_out_g[i]), row(attn_out_g[i]), w_out[i].astype(BF16),
                            row(norm_ffn_g[i]), tm)

        rho, e2, r1, c1 = _peer_route(xnt, w_query[i].T.astype(BF16), sub_keys1[i].astype(BF16),
                                      sub_keys2[i].astype(BF16), min(256, t))
        peer_t = _peer_dense(xnt, expert_u, expert_v, i,
                             rho, e2, r1, c1, tt, te)

        xf = _ple(xf, peer_t, p[i].reshape(t, D_PLE), row(norm_ple_g[i]), w_ple_gate[i].astype(BF16),
                  w_ple_proj[i].astype(BF16), row(final_norm_g), i == depth - 1, tm)
    return xf.reshape(n_batch, seq, D_MODEL)


def kernel(x, p, norm_mix_g, w_in, lam_re, lam_im, log_dt, b_re, b_im, c_re, c_im, d_skip, w_glu, b_glu, q_norm_g, k_norm_g, ssm_out_g, attn_out_g, w_out, norm_ffn_g, w_query, sub_keys1, sub_keys2, expert_u, expert_v, norm_ple_g, w_ple_gate, w_ple_proj, final_norm_g):
    return _forward(x, p, norm_mix_g, w_in, lam_re, lam_im, log_dt, b_re, b_im, c_re, c_im, d_skip, w_glu,
                    b_glu, q_norm_g, k_norm_g, ssm_out_g, attn_out_g, w_out, norm_ffn_g, w_query,
                    sub_keys1, sub_keys2, expert_u, expert_v, norm_ple_g, w_ple_gate, w_ple_proj,
                    final_norm_g)
```

```python
import functools
import math

import jax
import jax.numpy as jnp
from jax import lax
from jax.experimental import pallas as pl
from jax.experimental.pallas import tpu as pltpu

F32 = jnp.float32
BF16 = jnp.bfloat16

D_MODEL = 1024
D_PLE = 256
GRID_W = 64
EPS = 1e-6
LOG2_E = 1.4426950408889634
D_SSM = 512
SSM_P = 16
SSM_G = D_SSM // SSM_P
SSM_N = 64
D_ATTN = 512
HEAD_DIM = 64
V_ROWS = HEAD_DIM + 16
N_Q_HEADS = 8
N_KV_HEADS = 2
Q_PER_KV = 4
D_KV = 128
D_IN = D_SSM + D_ATTN + 2 * D_KV
ROPE_THETA = 10000.0
PEER_HEADS = 8
PEER_KEY_DIM = 256
PEER_HALF = 128
N_KEYS = 128
PEER_TOPK = 16

PEER_BLOCK_ROWS = 8
PEER_STRIP = 256
SSM_CHUNK = 32
SSM_LP = SSM_CHUNK * SSM_P
VMEM_LIMIT = 48 * 1024 * 1024

def _cparams(*sem):
    return pltpu.CompilerParams(dimension_semantics=sem, vmem_limit_bytes=VMEM_LIMIT)


def _rms(x, g):
    return x * lax.rsqrt(jnp.mean(x * x, axis=-1, keepdims=True) + EPS) * g


def _in_proj_kernel(x_ref, g_ref, w_ref, cq_ref, sq_ref, gq_ref, gk_ref, bd_ref,
                    u_ref, ub_ref, qt_ref, k_ref, vt_ref):
    h = _rms(x_ref[...], g_ref[...])
    z = jnp.dot(h.astype(BF16), w_ref[...], preferred_element_type=F32)
    u_ref[...] = z[:, :D_SSM]
    ub_ref[...] = z[:, :D_SSM].astype(BF16)

    def head_norm_rope(t, gain, bd, c, s):
        ss = jnp.dot((t * t).astype(BF16), bd, preferred_element_type=F32)
        t = t * lax.rsqrt(ss * (1.0 / HEAD_DIM) + EPS) * gain
        w = t.shape[1]
        lane = lax.broadcasted_iota(jnp.int32, t.shape, 1)
        partner = jnp.where((lane % 32) < 16, pltpu.roll(t, w - 16, 1), pltpu.roll(t, 16, 1))
        return t * c + partner * s

    cq = cq_ref[...]
    sq = sq_ref[...]
    q = head_norm_rope(z[:, D_SSM:D_SSM + D_ATTN], gq_ref[...], bd_ref[...], cq, sq)
    qt_ref[...] = (q * (HEAD_DIM ** -0.5 * LOG2_E)).T.astype(BF16)
    k = head_norm_rope(z[:, D_SSM + D_ATTN:D_SSM + D_ATTN + D_KV], gk_ref[...],
                       bd_ref[:D_KV, :D_KV], cq[:, :D_KV], sq[:, :D_KV])
    vt = z[:, D_SSM + D_ATTN + D_KV:].T
    row = lax.broadcasted_iota(jnp.int32, (V_ROWS - HEAD_DIM, vt.shape[1]), 0)
    ones_row = jnp.where(row == 0, 1.0, 0.0).astype(F32)
    for j in range(N_KV_HEADS):
        k_ref[j] = k[:, j * HEAD_DIM:(j + 1) * HEAD_DIM].astype(BF16)
        vt_ref[j, 0] = jnp.concatenate([vt[j * HEAD_DIM:(j + 1) * HEAD_DIM], ones_row], axis=0).astype(BF16)


def _in_proj(x, g, w, cq, sq, gq, gk, bd, seq, tm):
    t = x.shape[0]
    nseq = seq // tm
    return pl.pallas_call(
        _in_proj_kernel,
        grid=(t // tm,),
        in_specs=[
            pl.BlockSpec((tm, D_MODEL), lambda i: (i, 0)),
            pl.BlockSpec((1, D_MODEL), lambda i: (0, 0)),
            pl.BlockSpec((D_MODEL, D_IN), lambda i: (0, 0)),
            pl.BlockSpec((tm, D_ATTN), lambda i: (i % nseq, 0)),
            pl.BlockSpec((tm, D_ATTN), lambda i: (i % nseq, 0)),
            pl.BlockSpec((1, D_ATTN), lambda i: (0, 0)),
            pl.BlockSpec((1, D_KV), lambda i: (0, 0)),
            pl.BlockSpec((D_ATTN, D_ATTN), lambda i: (0, 0)),
        ],
        out_specs=[
            pl.BlockSpec((tm, D_SSM), lambda i: (i, 0)),
            pl.BlockSpec((tm, D_SSM), lambda i: (i, 0)),
            pl.BlockSpec((D_ATTN, tm), lambda i: (0, i)),
            pl.BlockSpec((N_KV_HEADS, tm, HEAD_DIM), lambda i: (0, i, 0)),
            pl.BlockSpec((N_KV_HEADS, 1, V_ROWS, tm), lambda i: (0, i, 0, 0)),
        ],
        out_shape=[
            jax.ShapeDtypeStruct((t, D_SSM), F32),
            jax.ShapeDtypeStruct((t, D_SSM), BF16),
            jax.ShapeDtypeStruct((D_ATTN, t), BF16),
            jax.ShapeDtypeStruct((N_KV_HEADS, t, HEAD_DIM), BF16),
            jax.ShapeDtypeStruct((N_KV_HEADS, t // tm, V_ROWS, tm), BF16),
        ],
        compiler_params=_cparams("parallel"),
        name="in_proj",
    )(x, g, w, cq, sq, gq, gk, bd)


def _ssm_params_kernel(lr_ref, li_ref, ldt_ref, bt_re_ref, bt_im_ref, c_re_ref, c_im_ref,
                       mt_ref, et_ref, fm_ref, dec_ref):
    L = SSM_CHUNK
    hi = lax.Precision.HIGHEST
    r_parts, e_parts, f_parts, p_parts, q_parts = [], [], [], [], []
    idx = lax.broadcasted_iota(jnp.int32, (L, SSM_N), 0).astype(F32)
    for d in range(2):
        lr = lr_ref[0, d]
        li = li_ref[0, d]
        dt = jnp.exp(ldt_ref[0, d])

        def apow(tau):
            mag = jnp.exp(tau * (lr * dt))
            ang = tau * (li * dt)
            return mag * jnp.cos(ang), mag * jnp.sin(ang)

        def apow_rows(tau):
            pr, pi = apow(tau)
            return pr[:, None, :], pi[:, None, :]

        ab_re, ab_im = apow(jnp.ones((1, SSM_N), F32))
        den = lr * lr + li * li
        nr = ab_re - 1.0
        f_re = (nr * lr + ab_im * li) / den
        f_im = (ab_im * lr - nr * li) / den
        bt_re = bt_re_ref[0, d]
        bt_im = bt_im_ref[0, d]
        bb_re = f_re * bt_re - f_im * bt_im
        bb_im = f_re * bt_im + f_im * bt_re
        c_re = c_re_ref[0, d]
        c_im = c_im_ref[0, d]

        tau_k = idx if d == 0 else (L - 1.0) - idx
        pr, pi = apow_rows(tau_k)
        lm_re = (c_re[None] * pr - c_im[None] * pi).reshape(L * SSM_P, SSM_N)
        lm_im = (c_re[None] * pi + c_im[None] * pr).reshape(L * SSM_P, SSM_N)
        nt = (((1,), (1,)), ((), ()))
        r_parts.append(lax.dot_general(bb_re, lm_re, nt, precision=hi, preferred_element_type=F32)
                       - lax.dot_general(bb_im, lm_im, nt, precision=hi, preferred_element_type=F32))

        tau_e = (L - 1.0) - idx if d == 0 else idx
        pr, pi = apow_rows(tau_e)
        e_parts.append((pr * bb_re[None] - pi * bb_im[None]).reshape(L * SSM_P, SSM_N))
        e_parts.append((pr * bb_im[None] + pi * bb_re[None]).reshape(L * SSM_P, SSM_N))

        tau_f = idx + 1.0 if d == 0 else L - idx
        pr, pi = apow_rows(tau_f)
        f_parts.append((c_re[None] * pr - c_im[None] * pi).reshape(L * SSM_P, SSM_N))
        f_parts.append(-(c_re[None] * pi + c_im[None] * pr).reshape(L * SSM_P, SSM_N))

        al_re, al_im = apow(jnp.full((1, SSM_N), float(L), F32))
        p_parts += [al_re, al_re]
        q_parts += [-al_im, al_im]

    zeros = jnp.zeros((SSM_P, SSM_LP), F32)
    r_bwd = jnp.concatenate([r_parts[1], zeros], axis=1)
    r_fwd = pltpu.roll(jnp.concatenate([r_parts[0], zeros], axis=1), SSM_LP - SSM_P, 1)
    r_cat = r_bwd + r_fwd
    for s in range(L):
        shift = (2 * SSM_LP - (L - 1 - s) * SSM_P) % (2 * SSM_LP)
        win = r_cat if shift == 0 else pltpu.roll(r_cat, shift, 1)
        mt_ref[0, s * SSM_P:(s + 1) * SSM_P, :] = win[:, :SSM_LP].astype(BF16)
    et_ref[0] = jnp.concatenate(e_parts, axis=1).astype(BF16)
    fm_ref[0] = jnp.concatenate(f_parts, axis=1).astype(BF16)
    dec_ref[0] = jnp.concatenate([jnp.concatenate(p_parts, axis=1),
                                  jnp.concatenate(q_parts, axis=1)], axis=0)


def _ssm_params(lr, li, ldt, bt_re, bt_im, c_re, c_im):
    g = lr.shape[0]
    vec = pl.BlockSpec((1, 2, 1, SSM_N), lambda i: (i, 0, 0, 0))
    mat = pl.BlockSpec((1, 2, SSM_P, SSM_N), lambda i: (i, 0, 0, 0))
    return pl.pallas_call(
        _ssm_params_kernel,
        grid=(g,),
        in_specs=[vec, vec, vec, mat, mat, mat, mat],
        out_specs=[
            pl.BlockSpec((1, SSM_LP, SSM_LP), lambda i: (i, 0, 0)),
            pl.BlockSpec((1, SSM_LP, 4 * SSM_N), lambda i: (i, 0, 0)),
            pl.BlockSpec((1, SSM_LP, 4 * SSM_N), lambda i: (i, 0, 0)),
            pl.BlockSpec((1, 2, 4 * SSM_N), lambda i: (i, 0, 0)),
        ],
        out_shape=[
            jax.ShapeDtypeStruct((g, SSM_LP, SSM_LP), BF16),
            jax.ShapeDtypeStruct((g, SSM_LP, 4 * SSM_N), BF16),
            jax.ShapeDtypeStruct((g, SSM_LP, 4 * SSM_N), BF16),
            jax.ShapeDtypeStruct((g, 2, 4 * SSM_N), F32),
        ],
        compiler_params=_cparams("parallel"),
        name="ssm_params",
    )(lr, li, ldt, bt_re, bt_im, c_re, c_im)


def _ssm_conv_kernel(u_ref, mt_ref, et_ref, fm_ref, dec_ref, y_ref, *, n_batch):
    u = u_ref[0]
    nc = u.shape[0]
    ncb = nc // n_batch
    w = 2 * SSM_N
    s_all = jnp.dot(u, et_ref[0], preferred_element_type=F32)
    dec = dec_ref[0]
    row = lax.broadcasted_iota(jnp.int32, (ncb, w), 0)

    def cmul(x, p, q):
        return p * x + q * pltpu.roll(x, SSM_N, 1)

    h_rows = []
    for b in range(n_batch):
        s_b = s_all[b * ncb:(b + 1) * ncb]
        xf, xb = s_b[:, :w], s_b[:, w:]
        pf, qf = dec[0:1, :w], dec[1:2, :w]
        pb, qb = dec[0:1, w:], dec[1:2, w:]
        k = 1
        while k < ncb:
            xf = xf + jnp.where(row >= k, cmul(pltpu.roll(xf, k, 0), pf, qf), 0.0)
            xb = xb + jnp.where(row < ncb - k, cmul(pltpu.roll(xb, ncb - k, 0), pb, qb), 0.0)
            pf, qf = pf * pf - qf * qf, 2.0 * pf * qf
            pb, qb = pb * pb - qb * qb, 2.0 * pb * qb
            k *= 2
        hf = jnp.where(row >= 1, pltpu.roll(xf, 1, 0), 0.0)
        hb = jnp.where(row < ncb - 1, pltpu.roll(xb, ncb - 1, 0), 0.0)
        h_rows.append(jnp.concatenate([hf, hb], axis=1))
    h = jnp.concatenate(h_rows, axis=0).astype(BF16)
    y = jnp.dot(u, mt_ref[0], preferred_element_type=F32)
    y = y + lax.dot_general(h, fm_ref[0], (((1,), (1,)), ((), ())), preferred_element_type=F32)
    y_ref[0] = y.astype(y_ref.dtype)


def _ssm_conv(ug, mt, et, fm, dec, n_batch):
    g, nc, lp = ug.shape
    return pl.pallas_call(
        functools.partial(_ssm_conv_kernel, n_batch=n_batch),
        grid=(g,),
        in_specs=[
            pl.BlockSpec((1, nc, lp), lambda i: (i, 0, 0)),
            pl.BlockSpec((1, lp, lp), lambda i: (i, 0, 0)),
            pl.BlockSpec((1, lp, 4 * SSM_N), lambda i: (i, 0, 0)),
            pl.BlockSpec((1, lp, 4 * SSM_N), lambda i: (i, 0, 0)),
            pl.BlockSpec((1, 2, 4 * SSM_N), lambda i: (i, 0, 0)),
        ],
        out_specs=pl.BlockSpec((1, nc, lp), lambda i: (i, 0, 0)),
        out_shape=jax.ShapeDtypeStruct((g, nc, lp), BF16),
        compiler_params=_cparams("parallel"),
        name="ssm_conv",
    )(ug, mt, et, fm, dec)


def _attn_kernel(qt_ref, k_ref, vt_ref, o_ref, q_sc, m_ref, acc_ref, sa_ref, sb_ref, *, tq, tk, nk):
    qt = qt_ref[...]
    for j in range(Q_PER_KV):
        q_sc[:, j * tq:(j + 1) * tq] = qt[j * HEAD_DIM:(j + 1) * HEAD_DIM, :]
    m_ref[...] = jnp.full(m_ref.shape, -jnp.inf, F32)
    acc_ref[...] = jnp.zeros(acc_ref.shape, F32)

    def scores(i):
        kb = k_ref[0, pl.ds(pl.multiple_of(i * tk, tk), tk), :]
        return jnp.dot(kb, q_sc[...], preferred_element_type=F32)

    def accumulate(s_ref, i):
        s = s_ref[...]
        m_prev = m_ref[...]
        m_new = jnp.maximum(m_prev, jnp.max(s, axis=0, keepdims=True))
        alpha = jnp.exp2(m_prev - m_new)
        p = jnp.exp2(s - m_new[0:1, :])
        acc_ref[:V_ROWS, :] = (acc_ref[:V_ROWS, :] * alpha[0:1, :]
                               + jnp.dot(vt_ref[0, i], p.astype(BF16), preferred_element_type=F32))
        m_ref[...] = m_new

    sa_ref[...] = scores(0)

    def body(j, carry):
        sb_ref[...] = scores(2 * j + 1)
        accumulate(sa_ref, 2 * j)
        sa_ref[...] = scores(2 * j + 2)
        accumulate(sb_ref, 2 * j + 1)
        return carry

    lax.fori_loop(0, nk // 2 - 1, body, 0)
    sb_ref[...] = scores(nk - 1)
    accumulate(sa_ref, nk - 2)
    accumulate(sb_ref, nk - 1)
    acc = acc_ref[...]
    out = (acc / acc[HEAD_DIM:HEAD_DIM + 1, :]).T
    o_ref[...] = jnp.concatenate([out[j * tq:(j + 1) * tq, :HEAD_DIM] for j in range(Q_PER_KV)], axis=1)


def _attention(qt, k, vt, n_batch, seq, tq):
    t = qt.shape[1]
    tk = vt.shape[3]
    nq = seq // tq
    nk = seq // tk
    assert nk % 2 == 0, "the key loop handles chunks in pairs"
    w = Q_PER_KV * HEAD_DIM
    return pl.pallas_call(
        functools.partial(_attn_kernel, tq=tq, tk=tk, nk=nk),
        grid=(n_batch, N_KV_HEADS, nq),
        in_specs=[
            pl.BlockSpec((w, tq), lambda b, g, i: (g, b * nq + i)),
            pl.BlockSpec((1, seq, HEAD_DIM), lambda b, g, i: (g, b, 0)),
            pl.BlockSpec((1, nk, V_ROWS, tk), lambda b, g, i: (g, b, 0, 0)),
        ],
        out_specs=pl.BlockSpec((tq, w), lambda b, g, i: (b * nq + i, g)),
        out_shape=jax.ShapeDtypeStruct((t, D_ATTN), F32),
        scratch_shapes=[
            pltpu.VMEM((HEAD_DIM, Q_PER_KV * tq), BF16),
            pltpu.VMEM((8, Q_PER_KV * tq), F32),
            pltpu.VMEM((2 * HEAD_DIM, Q_PER_KV * tq), F32),
            pltpu.VMEM((tk, Q_PER_KV * tq), F32),
            pltpu.VMEM((tk, Q_PER_KV * tq), F32),
        ],
        compiler_params=_cparams("parallel", "parallel", "parallel"),
        name="attention",
    )(qt, k, vt)


def _out_proj_kernel(x_ref, ys_ref, u_ref, at_ref, dsk_ref, wglu_ref, bglu_ref, gs_ref, ga_ref,
                     wo_ref, gf_ref, xo_ref, xnt_ref):
    y = ys_ref[...].astype(F32) + dsk_ref[...] * u_ref[...]
    y = jax.nn.gelu(y)
    z = jnp.dot(y.astype(BF16), wglu_ref[...], preferred_element_type=F32) + bglu_ref[...]
    y = y * jax.nn.sigmoid(z)
    ys = _rms(y, gs_ref[...]).astype(BF16)
    ya = _rms(at_ref[...], ga_ref[...]).astype(BF16)
    xo = (x_ref[...]
          + jnp.dot(ys, wo_ref[:D_SSM, :], preferred_element_type=F32)
          + jnp.dot(ya, wo_ref[D_SSM:, :], preferred_element_type=F32))
    xo_ref[...] = xo
    xnt_ref[...] = _rms(xo, gf_ref[...]).T.astype(BF16)


def _out_proj(x, ys, u, at, dsk, wglu, bglu, gs, ga, wo, gf, tm):
    t = x.shape[0]
    row = lambda w: pl.BlockSpec((tm, w), lambda i: (i, 0))
    full = lambda a, b: pl.BlockSpec((a, b), lambda i: (0, 0))
    return pl.pallas_call(
        _out_proj_kernel,
        grid=(t // tm,),
        in_specs=[row(D_MODEL), row(D_SSM), row(D_SSM), row(D_ATTN), full(1, D_SSM),
                  full(D_SSM, D_SSM), full(1, D_SSM), full(1, D_SSM), full(1, D_ATTN),
                  full(D_MODEL, D_MODEL), full(1, D_MODEL)],
        out_specs=[row(D_MODEL), pl.BlockSpec((D_MODEL, tm), lambda i: (0, i))],
        out_shape=[jax.ShapeDtypeStruct((t, D_MODEL), F32), jax.ShapeDtypeStruct((D_MODEL, t), BF16)],
        compiler_params=_cparams("parallel"),
        name="out_proj",
    )(x, ys, u, at, dsk, wglu, bglu, gs, ga, wo, gf)


def _oe_sort_network(n):
    pairs = []
    p = 1
    while p < n:
        k = p
        while k >= 1:
            for j in range(k % p, n - k, 2 * k):
                for i in range(min(k, n - j - k)):
                    if (i + j) // (2 * p) == (i + j + k) // (2 * p):
                        pairs.append((i + j, i + j + k))
            k //= 2
        p *= 2
    return pairs


def _ce(a, b):
    if b is None:
        return a, None
    if a is None:
        return b, None
    return jnp.maximum(a, b), jnp.minimum(a, b)


def _sorted_top16(slabs):
    n = PEER_TOPK
    v = list(slabs) + [None] * (n - len(slabs))
    for i, j in _oe_sort_network(n):
        v[i], v[j] = _ce(v[i], v[j])
    for shift in (4, 2, 1):
        partner = [None if x is None else pltpu.roll(x, shift, 0) for x in v]
        top = []
        for a in range(n):
            x, y = v[a], partner[n - 1 - a]
            top.append(y if x is None else (x if y is None else jnp.maximum(x, y)))
        v = top
        d = n // 2
        while d >= 1:
            for i in range(n):
                if i & d == 0:
                    v[i], v[i + d] = _ce(v[i], v[i + d])
            d //= 2
    return v


_CAND = [(a, b) for a in range(PEER_TOPK) for b in range(PEER_TOPK) if (a + 1) * (b + 1) <= PEER_TOPK]


def _count_prefix(pred, vals):
    pick = lambda m, a, b: jnp.where(m, a, b)
    m8 = pred(vals[7])
    m4 = pred(pick(m8, vals[11], vals[3]))
    m2 = pred(pick(m8, pick(m4, vals[13], vals[9]), pick(m4, vals[5], vals[1])))
    m1 = pred(pick(m8, pick(m4, pick(m2, vals[14], vals[12]), pick(m2, vals[10], vals[8])),
                   pick(m4, pick(m2, vals[6], vals[4]), pick(m2, vals[2], vals[0]))))
    cnt = pick(m8, 8.0, 0.0) + pick(m4, 4.0, 0.0) + pick(m2, 2.0, 0.0) + pick(m1, 1.0, 0.0)
    return cnt + pick(pred(vals[15]), 1.0, 0.0)


def _bf16_pair_words(x):
    b = pltpu.bitcast(x, jnp.uint32)
    b = (b + jnp.uint32(0x7FFF) + ((b >> 16) & jnp.uint32(1))) >> 16
    return b | (b << 16)


def _peer_route_kernel(xnt_ref, wq_ref, k1_ref, k2_ref, rho_ref, e2_ref, r1_ref, c1_ref):
    def scores(h):
        qt = jnp.dot(wq_ref[h * PEER_KEY_DIM:(h + 1) * PEER_KEY_DIM, :], xnt_ref[...],
                     preferred_element_type=F32)
        return (jnp.dot(k1_ref[h], qt[:PEER_HALF].astype(BF16), preferred_element_type=F32),
                jnp.dot(k2_ref[h], qt[PEER_HALF:].astype(BF16), preferred_element_type=F32))

    s = scores(0)
    for h in range(PEER_HEADS):
        s_next = scores(h + 1) if h + 1 < PEER_HEADS else None
        _peer_route_head(h, s[0], s[1], rho_ref, e2_ref, r1_ref, c1_ref)
        s = s_next


def _peer_route_head(h, s1, s2, rho_ref, e2_ref, r1_ref, c1_ref):
    tm = s1.shape[1]
    nslab = N_KEYS // 8
    s1 = [s1[8 * i:8 * i + 8] for i in range(nslab)]
    s2 = [s2[8 * i:8 * i + 8] for i in range(nslab)]
    v1 = _sorted_top16(s1)
    v2 = _sorted_top16(s2)

    sub = lax.broadcasted_iota(jnp.int32, (8, tm), 0)
    cand = [v1[a] + v2[b] for a, b in _CAND]
    packed = []
    for base in range(0, len(cand), 8):
        slab = None
        for s, c in enumerate(cand[base:base + 8]):
            slab = c if slab is None else jnp.where(sub == s, c, slab)
        if len(cand) - base < 8:
            slab = jnp.where(sub < len(cand) - base, slab, -jnp.inf)
        packed.append(slab)
    best = _sorted_top16(packed)
    theta = best[PEER_TOPK - 1]
    z = jnp.ones((8, tm), F32)
    for c in best[1:]:
        z = z + jnp.exp(c - best[0])
    inv_z = 1.0 / z

    r1, c1, rho, e2 = [], [], [], []
    for i in range(nslab):
        r1.append(_count_prefix(lambda v, x=s1[i]: x + v >= theta, v2))
        rho.append(_count_prefix(lambda v, x=s2[i]: v > x, v2))
        c1.append(jnp.exp(s1[i] - v1[0]) * inv_z)
        e2.append(jnp.exp(s2[i] - v2[0]))
    r1 = _bf16_pair_words(jnp.concatenate(r1, axis=0))
    c1 = _bf16_pair_words(jnp.concatenate(c1, axis=0))
    rho = jnp.concatenate(rho, axis=0).astype(BF16)
    e2 = jnp.concatenate(e2, axis=0).astype(BF16)
    for l in range(tm // 128):
        lanes = slice(l * 128, (l + 1) * 128)
        r1_ref[h, l] = r1[:, lanes]
        c1_ref[h, l] = c1[:, lanes]
    for l in range(tm // 256):
        lanes = slice(l * 256, (l + 1) * 256)
        rho_ref[h, l * N_KEYS:(l + 1) * N_KEYS, :] = rho[:, lanes]
        e2_ref[h, l * N_KEYS:(l + 1) * N_KEYS, :] = e2[:, lanes]


def _peer_route(xn_t, wq_t, k1, k2, tm):
    t = xn_t.shape[1]
    sel = pl.BlockSpec((PEER_HEADS, tm // 128, N_KEYS, 128), lambda i: (0, i, 0, 0))
    shp = jax.ShapeDtypeStruct((PEER_HEADS, t // 128, N_KEYS, 128), jnp.uint32)
    tok = pl.BlockSpec((PEER_HEADS, tm // 2, 256), lambda i: (0, i, 0))
    shp16 = jax.ShapeDtypeStruct((PEER_HEADS, t // 2, 256), BF16)
    return pl.pallas_call(
        _peer_route_kernel,
        grid=(t // tm,),
        in_specs=[
            pl.BlockSpec((D_MODEL, tm), lambda i: (0, i)),
            pl.BlockSpec((PEER_HEADS * PEER_KEY_DIM, D_MODEL), lambda i: (0, 0)),
            pl.BlockSpec((PEER_HEADS, N_KEYS, PEER_HALF), lambda i: (0, 0, 0)),
            pl.BlockSpec((PEER_HEADS, N_KEYS, PEER_HALF), lambda i: (0, 0, 0)),
        ],
        out_specs=[tok, tok, sel, sel],
        out_shape=[shp16, shp16, shp, shp],
        compiler_params=_cparams("parallel"),
        name="peer_route",
    )(xn_t, wq_t, k1, k2)


def _gelu_tanh(x):
    k1 = -2.0 * math.sqrt(2.0 / math.pi) * LOG2_E
    t = x * (k1 + (k1 * 0.044715) * (x * x))
    return x / (1.0 + jnp.exp2(t))


def _load_row_repeated(ref, h, l, i, n):
    return ref[h, l, pl.ds(i, n, stride=0), :]


def _peer_dense_kernel(xn_ref, u_ref, vt_ref, rho_ref, e2_ref, r1_ref, c1_ref, o_ref, a_ref, *, rows):
    e = pl.program_id(1)

    @pl.when(e == 0)
    def _():
        o_ref[...] = jnp.zeros(o_ref.shape, F32)

    tt = xn_ref.shape[1]
    sub = 16
    strip = PEER_STRIP
    zero = jnp.zeros((sub, strip), BF16)

    def row_tile(ref, h, s, i):
        return jnp.concatenate(
            [pltpu.bitcast(_load_row_repeated(ref, h, s * (strip // 128) + l, i, sub // 2), BF16)
             for l in range(strip // 128)], axis=1)

    blk = PEER_BLOCK_ROWS * N_KEYS
    nblk = rows // PEER_BLOCK_ROWS

    def first_matmul(k):
        return jnp.dot(u_ref[k * blk:(k + 1) * blk, :].astype(BF16), xn_ref[...], preferred_element_type=F32)

    act = first_matmul(0)
    for k in range(nblk):
        act_next = first_matmul(k + 1) if k + 1 < nblk else None
        a_ref[k * blk:(k + 1) * blk, :] = _gelu_tanh(act).astype(BF16)
        for s in range(tt // strip):
            lanes = slice(s * strip, (s + 1) * strip)
            for ii in range(PEER_BLOCK_ROWS):
                i = k * PEER_BLOCK_ROWS + ii
                rc = [(row_tile(r1_ref, h, s, i), row_tile(c1_ref, h, s, i)) for h in range(PEER_HEADS)]
                for jb in range(N_KEYS // sub):
                    keys = slice(s * N_KEYS + jb * sub, s * N_KEYS + (jb + 1) * sub)
                    w = None
                    for h in range(PEER_HEADS):
                        r, c = rc[h]
                        term = c * jnp.where(rho_ref[h, keys, :] < r, e2_ref[h, keys, :], zero)
                        w = term if w is None else w + term
                    lo = k * blk + ii * N_KEYS + jb * sub
                    a_ref[lo:lo + sub, lanes] = a_ref[lo:lo + sub, lanes] * w
        o_ref[...] += lax.dot_general(vt_ref[k * blk:(k + 1) * blk, :].astype(BF16),
                                      a_ref[k * blk:(k + 1) * blk, :], (((0,), (0,)), ((), ())),
                                      preferred_element_type=F32)
        act = act_next


def _peer_dense(xn_t, eu, ev, layer, rho, e2, r1, c1, tt, te):
    t = xn_t.shape[1]
    ne = eu.shape[1]
    rows = te // N_KEYS
    tok = pl.BlockSpec((PEER_HEADS, tt // 2, 256), lambda j, e: (0, j, 0))
    sel = pl.BlockSpec((PEER_HEADS, tt // 128, rows, 128), lambda j, e: (0, j, e, 0))
    return pl.pallas_call(
        functools.partial(_peer_dense_kernel, rows=rows),
        grid=(t // tt, ne // te),
        in_specs=[
            pl.BlockSpec((D_MODEL, tt), lambda j, e: (0, j)),
            pl.BlockSpec((None, te, D_MODEL), lambda j, e: (layer, e, 0)),
            pl.BlockSpec((None, te, D_MODEL), lambda j, e: (layer, e, 0)),
            tok, tok, sel, sel,
        ],
        out_specs=pl.BlockSpec((D_MODEL, tt), lambda j, e: (0, j)),
        out_shape=jax.ShapeDtypeStruct((D_MODEL, t), F32),
        scratch_shapes=[pltpu.VMEM((te, tt), BF16)],
        compiler_params=_cparams("parallel", "arbitrary"),
        name="peer_dense",
    )(xn_t, eu, ev, rho, e2, r1, c1)


def _ple_kernel(x_ref, pe_ref, p_ref, g_ref, wg_ref, wp_ref, gfin_ref, o_ref, *, final):
    x = x_ref[...] + pe_ref[...].T
    gate = jax.nn.sigmoid(jnp.dot(_rms(x, g_ref[...]).astype(BF16), wg_ref[...], preferred_element_type=F32))
    x = x + gate * jnp.dot(p_ref[...].astype(BF16), wp_ref[...], preferred_element_type=F32)
    if final:
        x = _rms(x, gfin_ref[...])
    o_ref[...] = x


def _ple(x, pe, p, g, wg, wp, gfin, final, tm):
    t = x.shape[0]
    row = lambda w: pl.BlockSpec((tm, w), lambda i: (i, 0))
    full = lambda a, b: pl.BlockSpec((a, b), lambda i: (0, 0))
    return pl.pallas_call(
        functools.partial(_ple_kernel, final=final),
        grid=(t // tm,),
        in_specs=[row(D_MODEL), pl.BlockSpec((D_MODEL, tm), lambda i: (0, i)), row(D_PLE), full(1, D_MODEL),
                  full(D_MODEL, D_MODEL), full(D_PLE, D_MODEL), full(1, D_MODEL)],
        out_specs=row(D_MODEL),
        out_shape=jax.ShapeDtypeStruct((t, D_MODEL), F32),
        compiler_params=_cparams("parallel"),
        name="ple",
    )(x, pe, p, g, wg, wp, gfin)


def _rope_tables(seq):
    n_rows = seq // GRID_W
    row = jnp.repeat(jnp.arange(n_rows, dtype=jnp.int32), GRID_W).astype(F32)
    col = jnp.tile(jnp.arange(GRID_W, dtype=jnp.int32), n_rows).astype(F32)
    quarter = HEAD_DIM // 4
    inv = ROPE_THETA ** (-jnp.arange(quarter, dtype=F32) / quarter)
    ang_r = row[:, None] * inv[None, :]
    ang_c = col[:, None] * inv[None, :]
    cos = jnp.concatenate([jnp.cos(ang_r)] * 2 + [jnp.cos(ang_c)] * 2, axis=1)
    sin = jnp.concatenate([-jnp.sin(ang_r), jnp.sin(ang_r), -jnp.sin(ang_c), jnp.sin(ang_c)], axis=1)
    return jnp.tile(cos, (1, N_Q_HEADS)), jnp.tile(sin, (1, N_Q_HEADS))


def _forward(x, p, norm_mix_g, w_in, lam_re, lam_im, log_dt, b_re, b_im, c_re, c_im, d_skip, w_glu,
             b_glu, q_norm_g, k_norm_g, ssm_out_g, attn_out_g, w_out, norm_ffn_g, w_query, sub_keys1,
             sub_keys2, expert_u, expert_v, norm_ple_g, w_ple_gate, w_ple_proj, final_norm_g):
    n_batch, seq, _ = x.shape
    depth = w_in.shape[0]
    t = n_batch * seq
    tm = min(512, seq)
    tq = min(1024, seq)
    tk = min(512, seq // 2)
    tt = min(512, t)
    te = 2048
    nc = t // SSM_CHUNK

    cq, sq = _rope_tables(seq)
    bd = jnp.kron(jnp.eye(N_Q_HEADS, dtype=F32), jnp.ones((HEAD_DIM, HEAD_DIM), F32)).astype(BF16)
    row = lambda a: a.reshape(1, -1).astype(F32)

    xf = x.reshape(t, D_MODEL)
    for i in range(depth):
        u, ub, qt, k, vt = _in_proj(xf, row(norm_mix_g[i]), w_in[i].astype(BF16), cq, sq,
                                row(jnp.tile(q_norm_g[i], N_Q_HEADS)), row(jnp.tile(k_norm_g[i], N_KV_HEADS)),
                                bd, seq, tk)

        grp = lambda a: jnp.transpose(a, (1, 0, 2))[:, :, None, :]
        ldt = jnp.broadcast_to(log_dt[i][:, :, None], (2, SSM_G, SSM_N))
        mt, et, fm, dec = _ssm_params(
            grp(lam_re[i]), grp(lam_im[i]), grp(ldt),
            jnp.transpose(b_re[i], (1, 0, 3, 2)), jnp.transpose(b_im[i], (1, 0, 3, 2)),
            jnp.transpose(c_re[i], (1, 0, 2, 3)), jnp.transpose(c_im[i], (1, 0, 2, 3)))
        ug = (ub.reshape(nc, SSM_CHUNK, SSM_G, SSM_P)
              .transpose(2, 0, 1, 3).reshape(SSM_G, nc, SSM_LP))
        yg = _ssm_conv(ug, mt, et, fm, dec, n_batch)
        y_ssm = yg.reshape(SSM_G, nc, SSM_CHUNK, SSM_P).transpose(1, 2, 0, 3).reshape(t, D_SSM)

        attn = _attention(qt, k, vt, n_batch, seq, tq)

        xf, xnt = _out_proj(xf, y_ssm, u, attn, row(d_skip[i]), w_glu[i].astype(BF16), row(b_glu[i]),
                            row(ssm_out_g[i]), row(attn_out_g[i]), w_out[i].astype(BF16),
                            row(norm_ffn_g[i]), tm)

        rho, e2, r1, c1 = _peer_route(xnt, w_query[i].T.astype(BF16), sub_keys1[i].astype(BF16),
                                      sub_keys2[i].astype(BF16), min(256, t))
        peer_t = _peer_dense(xnt, expert_u, expert_v, i,
                             rho, e2, r1, c1, tt, te)

        xf = _ple(xf, peer_t, p[i].reshape(t, D_PLE), row(norm_ple_g[i]), w_ple_gate[i].astype(BF16),
                  w_ple_proj[i].astype(BF16), row(final_norm_g), i == depth - 1, tm)
    return xf.reshape(n_batch, seq, D_MODEL)


def kernel(x, p, norm_mix_g, w_in, lam_re, lam_im, log_dt, b_re, b_im, c_re, c_im, d_skip, w_glu, b_glu, q_norm_g, k_norm_g, ssm_out_g, attn_out_g, w_out, norm_ffn_g, w_query, sub_keys1, sub_keys2, expert_u, expert_v, norm_ple_g, w_ple_gate, w_ple_proj, final_norm_g):
    return _forward(x, p, norm_mix_g, w_in, lam_re, lam_im, log_dt, b_re, b_im, c_re, c_im, d_skip, w_glu,
                    b_glu, q_norm_g, k_norm_g, ssm_out_g, attn_out_g, w_out, norm_ffn_g, w_query,
                    sub_keys1, sub_keys2, expert_u, expert_v, norm_ple_g, w_ple_gate, w_ple_proj,
                    final_norm_g)
```

```python
import functools
import math
from typing import NamedTuple

import jax
import jax.numpy as jnp
from jax import lax
from jax.experimental import pallas as pl
from jax.experimental.pallas import tpu as pltpu

F32 = jnp.float32
BF16 = jnp.bfloat16

D_MODEL = 1024
D_PLE = 256
GRID_W = 64
EPS = 1e-6
LOG2_E = 1.4426950408889634
D_SSM = 512
SSM_P = 16
SSM_G = D_SSM // SSM_P
SSM_N = 64
D_ATTN = 512
HEAD_DIM = 64
V_ROWS = HEAD_DIM + 16
N_Q_HEADS = 8
N_KV_HEADS = 2
Q_PER_KV = 4
D_KV = 128
D_IN = D_SSM + D_ATTN + 2 * D_KV
ROPE_THETA = 10000.0
PEER_HEADS = 8
PEER_KEY_DIM = 256
PEER_HALF = 128
N_KEYS = 128
PEER_TOPK = 16

LANE = 128
SUBLANE = 8
BF16_ROWS = 2 * SUBLANE
PEER_BLOCK_ROWS = 8
PEER_STRIP = 2 * LANE
SSM_CHUNK = 32
SSM_LP = SSM_CHUNK * SSM_P
VMEM_LIMIT = 48 * 1024 * 1024

def _cparams(*sem):
    return pltpu.CompilerParams(dimension_semantics=sem, vmem_limit_bytes=VMEM_LIMIT)


def _rms(x, g):
    return x * lax.rsqrt(jnp.mean(x * x, axis=-1, keepdims=True) + EPS) * g


def _in_proj_kernel(x_ref, g_ref, w_ref, cq_ref, sq_ref, gq_ref, gk_ref, bd_ref,
                    u_ref, ub_ref, qt_ref, k_ref, vt_ref):
    h = _rms(x_ref[...], g_ref[...])
    z = jnp.dot(h.astype(BF16), w_ref[...], preferred_element_type=F32)
    u_ref[...] = z[:, :D_SSM]
    ub_ref[...] = z[:, :D_SSM].astype(BF16)

    def head_norm_rope(t, gain, bd, c, s):
        ss = jnp.dot((t * t).astype(BF16), bd, preferred_element_type=F32)
        t = t * lax.rsqrt(ss * (1.0 / HEAD_DIM) + EPS) * gain
        w = t.shape[1]
        lane = lax.broadcasted_iota(jnp.int32, t.shape, 1)
        partner = jnp.where((lane % 32) < 16, pltpu.roll(t, w - 16, 1), pltpu.roll(t, 16, 1))
        return t * c + partner * s

    cq = cq_ref[...]
    sq = sq_ref[...]
    q = head_norm_rope(z[:, D_SSM:D_SSM + D_ATTN], gq_ref[...], bd_ref[...], cq, sq)
    qt_ref[...] = (q * (HEAD_DIM ** -0.5 * LOG2_E)).T.astype(BF16)
    k = head_norm_rope(z[:, D_SSM + D_ATTN:D_SSM + D_ATTN + D_KV], gk_ref[...],
                       bd_ref[:D_KV, :D_KV], cq[:, :D_KV], sq[:, :D_KV])
    vt = z[:, D_SSM + D_ATTN + D_KV:].T
    row = lax.broadcasted_iota(jnp.int32, (V_ROWS - HEAD_DIM, vt.shape[1]), 0)
    ones_row = jnp.where(row == 0, 1.0, 0.0).astype(F32)
    for j in range(N_KV_HEADS):
        k_ref[j] = k[:, j * HEAD_DIM:(j + 1) * HEAD_DIM].astype(BF16)
        vt_ref[j, 0] = jnp.concatenate([vt[j * HEAD_DIM:(j + 1) * HEAD_DIM], ones_row], axis=0).astype(BF16)


def _in_proj(x, g, w, cq, sq, gq, gk, bd, seq, tm):
    t = x.shape[0]
    nseq = seq // tm
    return pl.pallas_call(
        _in_proj_kernel,
        grid=(t // tm,),
        in_specs=[
            pl.BlockSpec((tm, D_MODEL), lambda i: (i, 0)),
            pl.BlockSpec((1, D_MODEL), lambda i: (0, 0)),
            pl.BlockSpec((D_MODEL, D_IN), lambda i: (0, 0)),
            pl.BlockSpec((tm, D_ATTN), lambda i: (i % nseq, 0)),
            pl.BlockSpec((tm, D_ATTN), lambda i: (i % nseq, 0)),
            pl.BlockSpec((1, D_ATTN), lambda i: (0, 0)),
            pl.BlockSpec((1, D_KV), lambda i: (0, 0)),
            pl.BlockSpec((D_ATTN, D_ATTN), lambda i: (0, 0)),
        ],
        out_specs=[
            pl.BlockSpec((tm, D_SSM), lambda i: (i, 0)),
            pl.BlockSpec((tm, D_SSM), lambda i: (i, 0)),
            pl.BlockSpec((D_ATTN, tm), lambda i: (0, i)),
            pl.BlockSpec((N_KV_HEADS, tm, HEAD_DIM), lambda i: (0, i, 0)),
            pl.BlockSpec((N_KV_HEADS, 1, V_ROWS, tm), lambda i: (0, i, 0, 0)),
        ],
        out_shape=[
            jax.ShapeDtypeStruct((t, D_SSM), F32),
            jax.ShapeDtypeStruct((t, D_SSM), BF16),
            jax.ShapeDtypeStruct((D_ATTN, t), BF16),
            jax.ShapeDtypeStruct((N_KV_HEADS, t, HEAD_DIM), BF16),
            jax.ShapeDtypeStruct((N_KV_HEADS, t // tm, V_ROWS, tm), BF16),
        ],
        compiler_params=_cparams("parallel"),
        name="in_proj",
    )(x, g, w, cq, sq, gq, gk, bd)


def _ssm_params_kernel(lr_ref, li_ref, ldt_ref, bt_re_ref, bt_im_ref, c_re_ref, c_im_ref,
                       mt_ref, et_ref, fm_ref, dec_ref):
    L = SSM_CHUNK
    hi = lax.Precision.HIGHEST
    r_parts, e_parts, f_parts, p_parts, q_parts = [], [], [], [], []
    idx = lax.broadcasted_iota(jnp.int32, (L, SSM_N), 0).astype(F32)
    for d in range(2):
        lr = lr_ref[0, d]
        li = li_ref[0, d]
        dt = jnp.exp(ldt_ref[0, d])

        def apow(tau):
            mag = jnp.exp(tau * (lr * dt))
            ang = tau * (li * dt)
            return mag * jnp.cos(ang), mag * jnp.sin(ang)

        def apow_rows(tau):
            pr, pi = apow(tau)
            return pr[:, None, :], pi[:, None, :]

        ab_re, ab_im = apow(jnp.ones((1, SSM_N), F32))
        den = lr * lr + li * li
        nr = ab_re - 1.0
        f_re = (nr * lr + ab_im * li) / den
        f_im = (ab_im * lr - nr * li) / den
        bt_re = bt_re_ref[0, d]
        bt_im = bt_im_ref[0, d]
        bb_re = f_re * bt_re - f_im * bt_im
        bb_im = f_re * bt_im + f_im * bt_re
        c_re = c_re_ref[0, d]
        c_im = c_im_ref[0, d]

        tau_k = idx if d == 0 else (L - 1.0) - idx
        pr, pi = apow_rows(tau_k)
        lm_re = (c_re[None] * pr - c_im[None] * pi).reshape(L * SSM_P, SSM_N)
        lm_im = (c_re[None] * pi + c_im[None] * pr).reshape(L * SSM_P, SSM_N)
        nt = (((1,), (1,)), ((), ()))
        r_parts.append(lax.dot_general(bb_re, lm_re, nt, precision=hi, preferred_element_type=F32)
                       - lax.dot_general(bb_im, lm_im, nt, precision=hi, preferred_element_type=F32))

        tau_e = (L - 1.0) - idx if d == 0 else idx
        pr, pi = apow_rows(tau_e)
        e_parts.append((pr * bb_re[None] - pi * bb_im[None]).reshape(L * SSM_P, SSM_N))
        e_parts.append((pr * bb_im[None] + pi * bb_re[None]).reshape(L * SSM_P, SSM_N))

        tau_f = idx + 1.0 if d == 0 else L - idx
        pr, pi = apow_rows(tau_f)
        f_parts.append((c_re[None] * pr - c_im[None] * pi).reshape(L * SSM_P, SSM_N))
        f_parts.append(-(c_re[None] * pi + c_im[None] * pr).reshape(L * SSM_P, SSM_N))

        al_re, al_im = apow(jnp.full((1, SSM_N), float(L), F32))
        p_parts += [al_re, al_re]
        q_parts += [-al_im, al_im]

    zeros = jnp.zeros((SSM_P, SSM_LP), F32)
    r_bwd = jnp.concatenate([r_parts[1], zeros], axis=1)
    r_fwd = pltpu.roll(jnp.concatenate([r_parts[0], zeros], axis=1), SSM_LP - SSM_P, 1)
    r_cat = r_bwd + r_fwd
    for s in range(L):
        shift = (2 * SSM_LP - (L - 1 - s) * SSM_P) % (2 * SSM_LP)
        win = r_cat if shift == 0 else pltpu.roll(r_cat, shift, 1)
        mt_ref[0, s * SSM_P:(s + 1) * SSM_P, :] = win[:, :SSM_LP].astype(BF16)
    et_ref[0] = jnp.concatenate(e_parts, axis=1).astype(BF16)
    fm_ref[0] = jnp.concatenate(f_parts, axis=1).astype(BF16)
    dec_ref[0] = jnp.concatenate([jnp.concatenate(p_parts, axis=1),
                                  jnp.concatenate(q_parts, axis=1)], axis=0)


def _ssm_params(lr, li, ldt, bt_re, bt_im, c_re, c_im):
    g = lr.shape[0]
    vec = pl.BlockSpec((1, 2, 1, SSM_N), lambda i: (i, 0, 0, 0))
    mat = pl.BlockSpec((1, 2, SSM_P, SSM_N), lambda i: (i, 0, 0, 0))
    return pl.pallas_call(
        _ssm_params_kernel,
        grid=(g,),
        in_specs=[vec, vec, vec, mat, mat, mat, mat],
        out_specs=[
            pl.BlockSpec((1, SSM_LP, SSM_LP), lambda i: (i, 0, 0)),
            pl.BlockSpec((1, SSM_LP, 4 * SSM_N), lambda i: (i, 0, 0)),
            pl.BlockSpec((1, SSM_LP, 4 * SSM_N), lambda i: (i, 0, 0)),
            pl.BlockSpec((1, 2, 4 * SSM_N), lambda i: (i, 0, 0)),
        ],
        out_shape=[
            jax.ShapeDtypeStruct((g, SSM_LP, SSM_LP), BF16),
            jax.ShapeDtypeStruct((g, SSM_LP, 4 * SSM_N), BF16),
            jax.ShapeDtypeStruct((g, SSM_LP, 4 * SSM_N), BF16),
            jax.ShapeDtypeStruct((g, 2, 4 * SSM_N), F32),
        ],
        compiler_params=_cparams("parallel"),
        name="ssm_params",
    )(lr, li, ldt, bt_re, bt_im, c_re, c_im)


def _ssm_conv_kernel(u_ref, mt_ref, et_ref, fm_ref, dec_ref, y_ref, *, n_batch):
    u = u_ref[0]
    nc = u.shape[0]
    ncb = nc // n_batch
    w = 2 * SSM_N
    s_all = jnp.dot(u, et_ref[0], preferred_element_type=F32)
    dec = dec_ref[0]
    row = lax.broadcasted_iota(jnp.int32, (ncb, w), 0)

    def cmul(x, p, q):
        return p * x + q * pltpu.roll(x, SSM_N, 1)

    h_rows = []
    for b in range(n_batch):
        s_b = s_all[b * ncb:(b + 1) * ncb]
        xf, xb = s_b[:, :w], s_b[:, w:]
        pf, qf = dec[0:1, :w], dec[1:2, :w]
        pb, qb = dec[0:1, w:], dec[1:2, w:]
        k = 1
        while k < ncb:
            xf = xf + jnp.where(row >= k, cmul(pltpu.roll(xf, k, 0), pf, qf), 0.0)
            xb = xb + jnp.where(row < ncb - k, cmul(pltpu.roll(xb, ncb - k, 0), pb, qb), 0.0)
            pf, qf = pf * pf - qf * qf, 2.0 * pf * qf
            pb, qb = pb * pb - qb * qb, 2.0 * pb * qb
            k *= 2
        hf = jnp.where(row >= 1, pltpu.roll(xf, 1, 0), 0.0)
        hb = jnp.where(row < ncb - 1, pltpu.roll(xb, ncb - 1, 0), 0.0)
        h_rows.append(jnp.concatenate([hf, hb], axis=1))
    h = jnp.concatenate(h_rows, axis=0).astype(BF16)
    y = jnp.dot(u, mt_ref[0], preferred_element_type=F32)
    y = y + lax.dot_general(h, fm_ref[0], (((1,), (1,)), ((), ())), preferred_element_type=F32)
    y_ref[0] = y.astype(y_ref.dtype)


def _ssm_conv(ug, mt, et, fm, dec, n_batch):
    g, nc, lp = ug.shape
    return pl.pallas_call(
        functools.partial(_ssm_conv_kernel, n_batch=n_batch),
        grid=(g,),
        in_specs=[
            pl.BlockSpec((1, nc, lp), lambda i: (i, 0, 0)),
            pl.BlockSpec((1, lp, lp), lambda i: (i, 0, 0)),
            pl.BlockSpec((1, lp, 4 * SSM_N), lambda i: (i, 0, 0)),
            pl.BlockSpec((1, lp, 4 * SSM_N), lambda i: (i, 0, 0)),
            pl.BlockSpec((1, 2, 4 * SSM_N), lambda i: (i, 0, 0)),
        ],
        out_specs=pl.BlockSpec((1, nc, lp), lambda i: (i, 0, 0)),
        out_shape=jax.ShapeDtypeStruct((g, nc, lp), BF16),
        compiler_params=_cparams("parallel"),
        name="ssm_conv",
    )(ug, mt, et, fm, dec)


def _attn_kernel(qt_ref, k_ref, vt_ref, o_ref, q_sc, m_ref, acc_ref, sa_ref, sb_ref, *, tq, tk, nk):
    qt = qt_ref[...]
    for j in range(Q_PER_KV):
        q_sc[:, j * tq:(j + 1) * tq] = qt[j * HEAD_DIM:(j + 1) * HEAD_DIM, :]
    m_ref[...] = jnp.full(m_ref.shape, -jnp.inf, F32)
    acc_ref[...] = jnp.zeros(acc_ref.shape, F32)

    def scores(i):
        kb = k_ref[0, pl.ds(pl.multiple_of(i * tk, tk), tk), :]
        return jnp.dot(kb, q_sc[...], preferred_element_type=F32)

    def accumulate(s_ref, i):
        s = s_ref[...]
        m_prev = m_ref[...]
        m_new = jnp.maximum(m_prev, jnp.max(s, axis=0, keepdims=True))
        alpha = jnp.exp2(m_prev - m_new)
        p = jnp.exp2(s - m_new[0:1, :])
        acc_ref[:V_ROWS, :] = (acc_ref[:V_ROWS, :] * alpha[0:1, :]
                               + jnp.dot(vt_ref[0, i], p.astype(BF16), preferred_element_type=F32))
        m_ref[...] = m_new

    sa_ref[...] = scores(0)

    def body(j, carry):
        sb_ref[...] = scores(2 * j + 1)
        accumulate(sa_ref, 2 * j)
        sa_ref[...] = scores(2 * j + 2)
        accumulate(sb_ref, 2 * j + 1)
        return carry

    lax.fori_loop(0, nk // 2 - 1, body, 0)
    sb_ref[...] = scores(nk - 1)
    accumulate(sa_ref, nk - 2)
    accumulate(sb_ref, nk - 1)
    acc = acc_ref[...]
    out = (acc / acc[HEAD_DIM:HEAD_DIM + 1, :]).T
    o_ref[...] = jnp.concatenate([out[j * tq:(j + 1) * tq, :HEAD_DIM] for j in range(Q_PER_KV)], axis=1)


def _attention(qt, k, vt, n_batch, seq, tq):
    t = qt.shape[1]
    tk = vt.shape[3]
    nq = seq // tq
    nk = seq // tk
    assert nk % 2 == 0, "the key loop handles chunks in pairs"
    w = Q_PER_KV * HEAD_DIM
    return pl.pallas_call(
        functools.partial(_attn_kernel, tq=tq, tk=tk, nk=nk),
        grid=(n_batch, N_KV_HEADS, nq),
        in_specs=[
            pl.BlockSpec((w, tq), lambda b, g, i: (g, b * nq + i)),
            pl.BlockSpec((1, seq, HEAD_DIM), lambda b, g, i: (g, b, 0)),
            pl.BlockSpec((1, nk, V_ROWS, tk), lambda b, g, i: (g, b, 0, 0)),
        ],
        out_specs=pl.BlockSpec((tq, w), lambda b, g, i: (b * nq + i, g)),
        out_shape=jax.ShapeDtypeStruct((t, D_ATTN), F32),
        scratch_shapes=[
            pltpu.VMEM((HEAD_DIM, Q_PER_KV * tq), BF16),
            pltpu.VMEM((8, Q_PER_KV * tq), F32),
            pltpu.VMEM((2 * HEAD_DIM, Q_PER_KV * tq), F32),
            pltpu.VMEM((tk, Q_PER_KV * tq), F32),
            pltpu.VMEM((tk, Q_PER_KV * tq), F32),
        ],
        compiler_params=_cparams("parallel", "parallel", "parallel"),
        name="attention",
    )(qt, k, vt)


def _out_proj_kernel(x_ref, ys_ref, u_ref, at_ref, dsk_ref, wglu_ref, bglu_ref, gs_ref, ga_ref,
                     wo_ref, gf_ref, xo_ref, xnt_ref):
    y = ys_ref[...].astype(F32) + dsk_ref[...] * u_ref[...]
    y = jax.nn.gelu(y)
    z = jnp.dot(y.astype(BF16), wglu_ref[...], preferred_element_type=F32) + bglu_ref[...]
    y = y * jax.nn.sigmoid(z)
    ys = _rms(y, gs_ref[...]).astype(BF16)
    ya = _rms(at_ref[...], ga_ref[...]).astype(BF16)
    xo = (x_ref[...]
          + jnp.dot(ys, wo_ref[:D_SSM, :], preferred_element_type=F32)
          + jnp.dot(ya, wo_ref[D_SSM:, :], preferred_element_type=F32))
    xo_ref[...] = xo
    xnt_ref[...] = _rms(xo, gf_ref[...]).T.astype(BF16)


def _out_proj(x, ys, u, at, dsk, wglu, bglu, gs, ga, wo, gf, tm):
    t = x.shape[0]
    row = lambda w: pl.BlockSpec((tm, w), lambda i: (i, 0))
    full = lambda a, b: pl.BlockSpec((a, b), lambda i: (0, 0))
    return pl.pallas_call(
        _out_proj_kernel,
        grid=(t // tm,),
        in_specs=[row(D_MODEL), row(D_SSM), row(D_SSM), row(D_ATTN), full(1, D_SSM),
                  full(D_SSM, D_SSM), full(1, D_SSM), full(1, D_SSM), full(1, D_ATTN),
                  full(D_MODEL, D_MODEL), full(1, D_MODEL)],
        out_specs=[row(D_MODEL), pl.BlockSpec((D_MODEL, tm), lambda i: (0, i))],
        out_shape=[jax.ShapeDtypeStruct((t, D_MODEL), F32), jax.ShapeDtypeStruct((D_MODEL, t), BF16)],
        compiler_params=_cparams("parallel"),
        name="out_proj",
    )(x, ys, u, at, dsk, wglu, bglu, gs, ga, wo, gf)


def _oe_sort_network(n):
    pairs = []
    p = 1
    while p < n:
        k = p
        while k >= 1:
            for j in range(k % p, n - k, 2 * k):
                for i in range(min(k, n - j - k)):
                    if (i + j) // (2 * p) == (i + j + k) // (2 * p):
                        pairs.append((i + j, i + j + k))
            k //= 2
        p *= 2
    return pairs


def _ce(a, b):
    if b is None:
        return a, None
    if a is None:
        return b, None
    return jnp.maximum(a, b), jnp.minimum(a, b)


def _sorted_top16(slabs):
    n = PEER_TOPK
    v = list(slabs) + [None] * (n - len(slabs))
    for i, j in _oe_sort_network(n):
        v[i], v[j] = _ce(v[i], v[j])
    for shift in (4, 2, 1):
        partner = [None if x is None else pltpu.roll(x, shift, 0) for x in v]
        top = []
        for a in range(n):
            x, y = v[a], partner[n - 1 - a]
            top.append(y if x is None else (x if y is None else jnp.maximum(x, y)))
        v = top
        d = n // 2
        while d >= 1:
            for i in range(n):
                if i & d == 0:
                    v[i], v[i + d] = _ce(v[i], v[i + d])
            d //= 2
    return v


_CAND = [(a, b) for a in range(PEER_TOPK) for b in range(PEER_TOPK) if (a + 1) * (b + 1) <= PEER_TOPK]


def _count_prefix(pred, vals):
    pick = lambda m, a, b: jnp.where(m, a, b)
    m8 = pred(vals[7])
    m4 = pred(pick(m8, vals[11], vals[3]))
    m2 = pred(pick(m8, pick(m4, vals[13], vals[9]), pick(m4, vals[5], vals[1])))
    m1 = pred(pick(m8, pick(m4, pick(m2, vals[14], vals[12]), pick(m2, vals[10], vals[8])),
                   pick(m4, pick(m2, vals[6], vals[4]), pick(m2, vals[2], vals[0]))))
    cnt = pick(m8, 8.0, 0.0) + pick(m4, 4.0, 0.0) + pick(m2, 2.0, 0.0) + pick(m1, 1.0, 0.0)
    return cnt + pick(pred(vals[15]), 1.0, 0.0)


def _bf16_pair_words(x):
    b = pltpu.bitcast(x, jnp.uint32)
    b = (b + jnp.uint32(0x7FFF) + ((b >> 16) & jnp.uint32(1))) >> 16
    return b | (b << 16)


def _peer_route_kernel(xnt_ref, wq_ref, k1_ref, k2_ref, rho_ref, e2_ref, r1_ref, c1_ref):
    def scores(h):
        qt = jnp.dot(wq_ref[h * PEER_KEY_DIM:(h + 1) * PEER_KEY_DIM, :], xnt_ref[...],
                     preferred_element_type=F32)
        return (jnp.dot(k1_ref[h], qt[:PEER_HALF].astype(BF16), preferred_element_type=F32),
                jnp.dot(k2_ref[h], qt[PEER_HALF:].astype(BF16), preferred_element_type=F32))

    s = scores(0)
    for h in range(PEER_HEADS):
        s_next = scores(h + 1) if h + 1 < PEER_HEADS else None
        _peer_route_head(h, s[0], s[1], rho_ref, e2_ref, r1_ref, c1_ref)
        s = s_next


def _peer_route_head(h, s1, s2, rho_ref, e2_ref, r1_ref, c1_ref):
    tm = s1.shape[1]
    nslab = N_KEYS // SUBLANE
    s1 = [s1[SUBLANE * i:SUBLANE * (i + 1)] for i in range(nslab)]
    s2 = [s2[SUBLANE * i:SUBLANE * (i + 1)] for i in range(nslab)]
    v1 = _sorted_top16(s1)
    v2 = _sorted_top16(s2)

    sub = lax.broadcasted_iota(jnp.int32, (SUBLANE, tm), 0)
    cand = [v1[a] + v2[b] for a, b in _CAND]
    packed = []
    for base in range(0, len(cand), 8):
        slab = None
        for s, c in enumerate(cand[base:base + 8]):
            slab = c if slab is None else jnp.where(sub == s, c, slab)
        if len(cand) - base < 8:
            slab = jnp.where(sub < len(cand) - base, slab, -jnp.inf)
        packed.append(slab)
    best = _sorted_top16(packed)
    theta = best[PEER_TOPK - 1]
    z = jnp.ones((SUBLANE, tm), F32)
    for c in best[1:]:
        z = z + jnp.exp(c - best[0])
    inv_z = 1.0 / z

    r1, c1, rho, e2 = [], [], [], []
    for i in range(nslab):
        r1.append(_count_prefix(lambda v, x=s1[i]: x + v >= theta, v2))
        rho.append(_count_prefix(lambda v, x=s2[i]: v > x, v2))
        c1.append(jnp.exp(s1[i] - v1[0]) * inv_z)
        e2.append(jnp.exp(s2[i] - v2[0]))
    r1 = _bf16_pair_words(jnp.concatenate(r1, axis=0))
    c1 = _bf16_pair_words(jnp.concatenate(c1, axis=0))
    rho = jnp.concatenate(rho, axis=0).astype(BF16)
    e2 = jnp.concatenate(e2, axis=0).astype(BF16)
    for l in range(tm // LANE):
        lanes = slice(l * LANE, (l + 1) * LANE)
        r1_ref[h, l] = r1[:, lanes]
        c1_ref[h, l] = c1[:, lanes]
    for l in range(tm // PEER_STRIP):
        lanes = slice(l * PEER_STRIP, (l + 1) * PEER_STRIP)
        rho_ref[h, l * N_KEYS:(l + 1) * N_KEYS, :] = rho[:, lanes]
        e2_ref[h, l * N_KEYS:(l + 1) * N_KEYS, :] = e2[:, lanes]


def _peer_route(xn_t, wq_t, k1, k2, tm):
    t = xn_t.shape[1]
    sel = pl.BlockSpec((PEER_HEADS, tm // LANE, N_KEYS, LANE), lambda i: (0, i, 0, 0))
    shp = jax.ShapeDtypeStruct((PEER_HEADS, t // LANE, N_KEYS, LANE), jnp.uint32)
    tok = pl.BlockSpec((PEER_HEADS, tm // 2, PEER_STRIP), lambda i: (0, i, 0))
    shp16 = jax.ShapeDtypeStruct((PEER_HEADS, t // 2, PEER_STRIP), BF16)
    return pl.pallas_call(
        _peer_route_kernel,
        grid=(t // tm,),
        in_specs=[
            pl.BlockSpec((D_MODEL, tm), lambda i: (0, i)),
            pl.BlockSpec((PEER_HEADS * PEER_KEY_DIM, D_MODEL), lambda i: (0, 0)),
            pl.BlockSpec((PEER_HEADS, N_KEYS, PEER_HALF), lambda i: (0, 0, 0)),
            pl.BlockSpec((PEER_HEADS, N_KEYS, PEER_HALF), lambda i: (0, 0, 0)),
        ],
        out_specs=[tok, tok, sel, sel],
        out_shape=[shp16, shp16, shp, shp],
        compiler_params=_cparams("parallel"),
        name="peer_route",
    )(xn_t, wq_t, k1, k2)


def _gelu_tanh(x):
    k1 = -2.0 * math.sqrt(2.0 / math.pi) * LOG2_E
    t = x * (k1 + (k1 * 0.044715) * (x * x))
    return x / (1.0 + jnp.exp2(t))


def _load_row_repeated(ref, h, l, i, n):
    return ref[h, l, pl.ds(i, n, stride=0), :]


def _peer_dense_kernel(xn_ref, u_ref, vt_ref, rho_ref, e2_ref, r1_ref, c1_ref, o_ref, a_ref, *, rows):
    e = pl.program_id(1)

    @pl.when(e == 0)
    def _():
        o_ref[...] = jnp.zeros(o_ref.shape, F32)

    tt = xn_ref.shape[1]
    sub = BF16_ROWS
    strip = PEER_STRIP
    zero = jnp.zeros((sub, strip), BF16)

    def row_tile(ref, h, s, i):
        words = jnp.concatenate([_load_row_repeated(ref, h, s * (strip // LANE) + l, i, sub // 2)
                                 for l in range(strip // LANE)], axis=1)
        return pltpu.bitcast(words, BF16)

    blk = PEER_BLOCK_ROWS * N_KEYS
    nblk = rows // PEER_BLOCK_ROWS

    def first_matmul(k):
        return jnp.dot(u_ref[k * blk:(k + 1) * blk, :].astype(BF16), xn_ref[...], preferred_element_type=F32)

    a_ref[0:blk, :] = _gelu_tanh(first_matmul(0)).astype(BF16)
    for k in range(nblk):
        if k + 1 < nblk:
            a_ref[(k + 1) * blk:(k + 2) * blk, :] = _gelu_tanh(first_matmul(k + 1)).astype(BF16)
        for s in range(tt // strip):
            lanes = slice(s * strip, (s + 1) * strip)
            for ii in range(PEER_BLOCK_ROWS):
                i = k * PEER_BLOCK_ROWS + ii
                w = [None] * (N_KEYS // sub)
                for h in range(PEER_HEADS):
                    r = row_tile(r1_ref, h, s, i)
                    c = row_tile(c1_ref, h, s, i)
                    for jb in range(N_KEYS // sub):
                        keys = slice(s * N_KEYS + jb * sub, s * N_KEYS + (jb + 1) * sub)
                        term = c * jnp.where(rho_ref[h, keys, :] < r, e2_ref[h, keys, :], zero)
                        w[jb] = term if w[jb] is None else w[jb] + term
                for jb in range(N_KEYS // sub):
                    lo = k * blk + ii * N_KEYS + jb * sub
                    a_ref[lo:lo + sub, lanes] = a_ref[lo:lo + sub, lanes] * w[jb]
        o_ref[...] += lax.dot_general(vt_ref[k * blk:(k + 1) * blk, :].astype(BF16),
                                      a_ref[k * blk:(k + 1) * blk, :], (((0,), (0,)), ((), ())),
                                      preferred_element_type=F32)


def _peer_dense(xn_t, eu, ev, layer, rho, e2, r1, c1, tt, te):
    t = xn_t.shape[1]
    ne = eu.shape[1]
    rows = te // N_KEYS
    tok = pl.BlockSpec((PEER_HEADS, tt // 2, PEER_STRIP), lambda j, e: (0, j, 0))
    sel = pl.BlockSpec((PEER_HEADS, tt // LANE, rows, LANE), lambda j, e: (0, j, e, 0))
    return pl.pallas_call(
        functools.partial(_peer_dense_kernel, rows=rows),
        grid=(t // tt, ne // te),
        in_specs=[
            pl.BlockSpec((D_MODEL, tt), lambda j, e: (0, j)),
            pl.BlockSpec((None, te, D_MODEL), lambda j, e: (layer, e, 0)),
            pl.BlockSpec((None, te, D_MODEL), lambda j, e: (layer, e, 0)),
            tok, tok, sel, sel,
        ],
        out_specs=pl.BlockSpec((D_MODEL, tt), lambda j, e: (0, j)),
        out_shape=jax.ShapeDtypeStruct((D_MODEL, t), F32),
        scratch_shapes=[pltpu.VMEM((te, tt), BF16)],
        compiler_params=_cparams("parallel", "arbitrary"),
        name="peer_dense",
    )(xn_t, eu, ev, rho, e2, r1, c1)


def _ple_kernel(x_ref, pe_ref, p_ref, g_ref, wg_ref, wp_ref, gfin_ref, o_ref, *, final):
    x = x_ref[...] + pe_ref[...].T
    gate = jax.nn.sigmoid(jnp.dot(_rms(x, g_ref[...]).astype(BF16), wg_ref[...], preferred_element_type=F32))
    x = x + gate * jnp.dot(p_ref[...].astype(BF16), wp_ref[...], preferred_element_type=F32)
    if final:
        x = _rms(x, gfin_ref[...])
    o_ref[...] = x


def _ple(x, pe, p, g, wg, wp, gfin, final, tm):
    t = x.shape[0]
    row = lambda w: pl.BlockSpec((tm, w), lambda i: (i, 0))
    full = lambda a, b: pl.BlockSpec((a, b), lambda i: (0, 0))
    return pl.pallas_call(
        functools.partial(_ple_kernel, final=final),
        grid=(t // tm,),
        in_specs=[row(D_MODEL), pl.BlockSpec((D_MODEL, tm), lambda i: (0, i)), row(D_PLE), full(1, D_MODEL),
                  full(D_MODEL, D_MODEL), full(D_PLE, D_MODEL), full(1, D_MODEL)],
        out_specs=row(D_MODEL),
        out_shape=jax.ShapeDtypeStruct((t, D_MODEL), F32),
        compiler_params=_cparams("parallel"),
        name="ple",
    )(x, pe, p, g, wg, wp, gfin)


def _rope_tables(seq):
    n_rows = seq // GRID_W
    row = jnp.repeat(jnp.arange(n_rows, dtype=jnp.int32), GRID_W).astype(F32)
    col = jnp.tile(jnp.arange(GRID_W, dtype=jnp.int32), n_rows).astype(F32)
    quarter = HEAD_DIM // 4
    inv = ROPE_THETA ** (-jnp.arange(quarter, dtype=F32) / quarter)
    ang_r = row[:, None] * inv[None, :]
    ang_c = col[:, None] * inv[None, :]
    cos = jnp.concatenate([jnp.cos(ang_r)] * 2 + [jnp.cos(ang_c)] * 2, axis=1)
    sin = jnp.concatenate([-jnp.sin(ang_r), jnp.sin(ang_r), -jnp.sin(ang_c), jnp.sin(ang_c)], axis=1)
    return jnp.tile(cos, (1, N_Q_HEADS)), jnp.tile(sin, (1, N_Q_HEADS))


class _Tiles(NamedTuple):
    tokens: int
    queries: int
    keys: int
    route_tokens: int
    peer_tokens: int
    peer_experts: int


def _tiles(n_batch, seq):
    t = n_batch * seq
    return _Tiles(tokens=min(512, seq), queries=min(1024, seq), keys=min(512, seq // 2),
                  route_tokens=min(2 * LANE, t), peer_tokens=min(512, t), peer_experts=2048)


def _forward(x, p, norm_mix_g, w_in, lam_re, lam_im, log_dt, b_re, b_im, c_re, c_im, d_skip, w_glu,
             b_glu, q_norm_g, k_norm_g, ssm_out_g, attn_out_g, w_out, norm_ffn_g, w_query, sub_keys1,
             sub_keys2, expert_u, expert_v, norm_ple_g, w_ple_gate, w_ple_proj, final_norm_g):
    n_batch, seq, _ = x.shape
    depth = w_in.shape[0]
    t = n_batch * seq
    tl = _tiles(n_batch, seq)
    tm, tq, tk, tt, te = tl.tokens, tl.queries, tl.keys, tl.peer_tokens, tl.peer_experts
    nc = t // SSM_CHUNK

    cq, sq = _rope_tables(seq)
    bd = jnp.kron(jnp.eye(N_Q_HEADS, dtype=F32), jnp.ones((HEAD_DIM, HEAD_DIM), F32)).astype(BF16)
    row = lambda a: a.reshape(1, -1).astype(F32)

    xf = x.reshape(t, D_MODEL)
    for i in range(depth):
        u, ub, qt, k, vt = _in_proj(xf, row(norm_mix_g[i]), w_in[i].astype(BF16), cq, sq,
                                row(jnp.tile(q_norm_g[i], N_Q_HEADS)), row(jnp.tile(k_norm_g[i], N_KV_HEADS)),
                                bd, seq, tk)

        grp = lambda a: jnp.transpose(a, (1, 0, 2))[:, :, None, :]
        ldt = jnp.broadcast_to(log_dt[i][:, :, None], (2, SSM_G, SSM_N))
        mt, et, fm, dec = _ssm_params(
            grp(lam_re[i]), grp(lam_im[i]), grp(ldt),
            jnp.transpose(b_re[i], (1, 0, 3, 2)), jnp.transpose(b_im[i], (1, 0, 3, 2)),
            jnp.transpose(c_re[i], (1, 0, 2, 3)), jnp.transpose(c_im[i], (1, 0, 2, 3)))
        ug = (ub.reshape(nc, SSM_CHUNK, SSM_G, SSM_P)
              .transpose(2, 0, 1, 3).reshape(SSM_G, nc, SSM_LP))
        yg = _ssm_conv(ug, mt, et, fm, dec, n_batch)
        y_ssm = yg.reshape(SSM_G, nc, SSM_CHUNK, SSM_P).transpose(1, 2, 0, 3).reshape(t, D_SSM)

        attn = _attention(qt, k, vt, n_batch, seq, tq)

        xf, xnt = _out_proj(xf, y_ssm, u, attn, row(d_skip[i]), w_glu[i].astype(BF16), row(b_glu[i]),
                            row(ssm_out_g[i]), row(attn_out_g[i]), w_out[i].astype(BF16),
                            row(norm_ffn_g[i]), tm)

        rho, e2, r1, c1 = _peer_route(xnt, w_query[i].T.astype(BF16), sub_keys1[i].astype(BF16),
                                      sub_keys2[i].astype(BF16), tl.route_tokens)
        peer_t = _peer_dense(xnt, expert_u, expert_v, i,
                             rho, e2, r1, c1, tt, te)

        xf = _ple(xf, peer_t, p[i].reshape(t, D_PLE), row(norm_ple_g[i]), w_ple_gate[i].astype(BF16),
                  w_ple_proj[i].astype(BF16), row(final_norm_g), i == depth - 1, tm)
    return xf.reshape(n_batch, seq, D_MODEL)


def kernel(x, p, norm_mix_g, w_in, lam_re, lam_im, log_dt, b_re, b_im, c_re, c_im, d_skip, w_glu, b_glu, q_norm_g, k_norm_g, ssm_out_g, attn_out_g, w_out, norm_ffn_g, w_query, sub_keys1, sub_keys2, expert_u, expert_v, norm_ple_g, w_ple_gate, w_ple_proj, final_norm_g):
    return _forward(x, p, norm_mix_g, w_in, lam_re, lam_im, log_dt, b_re, b_im, c_re, c_im, d_skip, w_glu,
                    b_glu, q_norm_g, k_norm_g, ssm_out_g, attn_out_g, w_out, norm_ffn_g, w_query,
                    sub_keys1, sub_keys2, expert_u, expert_v, norm_ple_g, w_ple_gate, w_ple_proj,
                    final_norm_g)
```

```python
import functools
import math
from typing import NamedTuple

import jax
import jax.numpy as jnp
from jax import lax
from jax.experimental import pallas as pl
from jax.experimental.pallas import tpu as pltpu

F32 = jnp.float32
BF16 = jnp.bfloat16

D_MODEL = 1024
D_PLE = 256
GRID_W = 64
EPS = 1e-6
LOG2_E = 1.4426950408889634
D_SSM = 512
SSM_P = 16
SSM_G = D_SSM // SSM_P
SSM_N = 64
D_ATTN = 512
HEAD_DIM = 64
V_ROWS = HEAD_DIM + 16
N_Q_HEADS = 8
N_KV_HEADS = 2
Q_PER_KV = 4
D_KV = 128
D_IN = D_SSM + D_ATTN + 2 * D_KV
ROPE_THETA = 10000.0
PEER_HEADS = 8
PEER_KEY_DIM = 256
PEER_HALF = 128
N_KEYS = 128
PEER_TOPK = 16

LANE = 128
SUBLANE = 8
BF16_ROWS = 2 * SUBLANE
PEER_BLOCK_ROWS = 8
PEER_STRIP = 2 * LANE
SSM_CHUNK = 32
SSM_LP = SSM_CHUNK * SSM_P
VMEM_LIMIT = 48 * 1024 * 1024

def _cparams(*sem):
    return pltpu.CompilerParams(dimension_semantics=sem, vmem_limit_bytes=VMEM_LIMIT)


def _rms(x, g):
    return x * lax.rsqrt(jnp.mean(x * x, axis=-1, keepdims=True) + EPS) * g


def _in_proj_kernel(x_ref, g_ref, w_ref, cq_ref, sq_ref, gq_ref, gk_ref, bd_ref,
                    u_ref, ub_ref, qt_ref, k_ref, vt_ref):
    h = _rms(x_ref[...], g_ref[...])
    z = jnp.dot(h.astype(BF16), w_ref[...], preferred_element_type=F32)
    u_ref[...] = z[:, :D_SSM]
    ub_ref[...] = z[:, :D_SSM].astype(BF16)

    def head_norm_rope(t, gain, bd, c, s):
        ss = jnp.dot((t * t).astype(BF16), bd, preferred_element_type=F32)
        t = t * lax.rsqrt(ss * (1.0 / HEAD_DIM) + EPS) * gain
        w = t.shape[1]
        lane = lax.broadcasted_iota(jnp.int32, t.shape, 1)
        partner = jnp.where((lane % 32) < 16, pltpu.roll(t, w - 16, 1), pltpu.roll(t, 16, 1))
        return t * c + partner * s

    cq = cq_ref[...]
    sq = sq_ref[...]
    q = head_norm_rope(z[:, D_SSM:D_SSM + D_ATTN], gq_ref[...], bd_ref[...], cq, sq)
    qt_ref[...] = (q * (HEAD_DIM ** -0.5 * LOG2_E)).T.astype(BF16)
    k = head_norm_rope(z[:, D_SSM + D_ATTN:D_SSM + D_ATTN + D_KV], gk_ref[...],
                       bd_ref[:D_KV, :D_KV], cq[:, :D_KV], sq[:, :D_KV])
    vt = z[:, D_SSM + D_ATTN + D_KV:].T
    row = lax.broadcasted_iota(jnp.int32, (V_ROWS - HEAD_DIM, vt.shape[1]), 0)
    ones_row = jnp.where(row == 0, 1.0, 0.0).astype(F32)
    for j in range(N_KV_HEADS):
        k_ref[j] = k[:, j * HEAD_DIM:(j + 1) * HEAD_DIM].astype(BF16)
        vt_ref[j, 0] = jnp.concatenate([vt[j * HEAD_DIM:(j + 1) * HEAD_DIM], ones_row], axis=0).astype(BF16)


def _in_proj(x, g, w, cq, sq, gq, gk, bd, seq, tm):
    t = x.shape[0]
    nseq = seq // tm
    return pl.pallas_call(
        _in_proj_kernel,
        grid=(t // tm,),
        in_specs=[
            pl.BlockSpec((tm, D_MODEL), lambda i: (i, 0)),
            pl.BlockSpec((1, D_MODEL), lambda i: (0, 0)),
            pl.BlockSpec((D_MODEL, D_IN), lambda i: (0, 0)),
            pl.BlockSpec((tm, D_ATTN), lambda i: (i % nseq, 0)),
            pl.BlockSpec((tm, D_ATTN), lambda i: (i % nseq, 0)),
            pl.BlockSpec((1, D_ATTN), lambda i: (0, 0)),
            pl.BlockSpec((1, D_KV), lambda i: (0, 0)),
            pl.BlockSpec((D_ATTN, D_ATTN), lambda i: (0, 0)),
        ],
        out_specs=[
            pl.BlockSpec((tm, D_SSM), lambda i: (i, 0)),
            pl.BlockSpec((tm, D_SSM), lambda i: (i, 0)),
            pl.BlockSpec((D_ATTN, tm), lambda i: (0, i)),
            pl.BlockSpec((N_KV_HEADS, tm, HEAD_DIM), lambda i: (0, i, 0)),
            pl.BlockSpec((N_KV_HEADS, 1, V_ROWS, tm), lambda i: (0, i, 0, 0)),
        ],
        out_shape=[
            jax.ShapeDtypeStruct((t, D_SSM), F32),
            jax.ShapeDtypeStruct((t, D_SSM), BF16),
            jax.ShapeDtypeStruct((D_ATTN, t), BF16),
            jax.ShapeDtypeStruct((N_KV_HEADS, t, HEAD_DIM), BF16),
            jax.ShapeDtypeStruct((N_KV_HEADS, t // tm, V_ROWS, tm), BF16),
        ],
        compiler_params=_cparams("parallel"),
        name="in_proj",
    )(x, g, w, cq, sq, gq, gk, bd)


def _ssm_params_kernel(lr_ref, li_ref, ldt_ref, bt_re_ref, bt_im_ref, c_re_ref, c_im_ref,
                       mt_ref, et_ref, fm_ref, dec_ref):
    L = SSM_CHUNK
    hi = lax.Precision.HIGHEST
    r_parts, e_parts, f_parts, p_parts, q_parts = [], [], [], [], []
    idx = lax.broadcasted_iota(jnp.int32, (L, SSM_N), 0).astype(F32)
    for d in range(2):
        lr = lr_ref[0, d]
        li = li_ref[0, d]
        dt = jnp.exp(ldt_ref[0, d])

        def apow(tau):
            mag = jnp.exp(tau * (lr * dt))
            ang = tau * (li * dt)
            return mag * jnp.cos(ang), mag * jnp.sin(ang)

        def apow_rows(tau):
            pr, pi = apow(tau)
            return pr[:, None, :], pi[:, None, :]

        ab_re, ab_im = apow(jnp.ones((1, SSM_N), F32))
        den = lr * lr + li * li
        nr = ab_re - 1.0
        f_re = (nr * lr + ab_im * li) / den
        f_im = (ab_im * lr - nr * li) / den
        bt_re = bt_re_ref[0, d]
        bt_im = bt_im_ref[0, d]
        bb_re = f_re * bt_re - f_im * bt_im
        bb_im = f_re * bt_im + f_im * bt_re
        c_re = c_re_ref[0, d]
        c_im = c_im_ref[0, d]

        tau_k = idx if d == 0 else (L - 1.0) - idx
        pr, pi = apow_rows(tau_k)
        lm_re = (c_re[None] * pr - c_im[None] * pi).reshape(L * SSM_P, SSM_N)
        lm_im = (c_re[None] * pi + c_im[None] * pr).reshape(L * SSM_P, SSM_N)
        nt = (((1,), (1,)), ((), ()))
        r_parts.append(lax.dot_general(bb_re, lm_re, nt, precision=hi, preferred_element_type=F32)
                       - lax.dot_general(bb_im, lm_im, nt, precision=hi, preferred_element_type=F32))

        tau_e = (L - 1.0) - idx if d == 0 else idx
        pr, pi = apow_rows(tau_e)
        e_parts.append((pr * bb_re[None] - pi * bb_im[None]).reshape(L * SSM_P, SSM_N))
        e_parts.append((pr * bb_im[None] + pi * bb_re[None]).reshape(L * SSM_P, SSM_N))

        tau_f = idx + 1.0 if d == 0 else L - idx
        pr, pi = apow_rows(tau_f)
        f_parts.append((c_re[None] * pr - c_im[None] * pi).reshape(L * SSM_P, SSM_N))
        f_parts.append(-(c_re[None] * pi + c_im[None] * pr).reshape(L * SSM_P, SSM_N))

        al_re, al_im = apow(jnp.full((1, SSM_N), float(L), F32))
        p_parts += [al_re, al_re]
        q_parts += [-al_im, al_im]

    zeros = jnp.zeros((SSM_P, SSM_LP), F32)
    r_bwd = jnp.concatenate([r_parts[1], zeros], axis=1)
    r_fwd = pltpu.roll(jnp.concatenate([r_parts[0], zeros], axis=1), SSM_LP - SSM_P, 1)
    r_cat = r_bwd + r_fwd
    for s in range(L):
        shift = (2 * SSM_LP - (L - 1 - s) * SSM_P) % (2 * SSM_LP)
        win = r_cat if shift == 0 else pltpu.roll(r_cat, shift, 1)
        mt_ref[0, s * SSM_P:(s + 1) * SSM_P, :] = win[:, :SSM_LP].astype(BF16)
    et_ref[0] = jnp.concatenate(e_parts, axis=1).astype(BF16)
    fm_ref[0] = jnp.concatenate(f_parts, axis=1).astype(BF16)
    dec_ref[0] = jnp.concatenate([jnp.concatenate(p_parts, axis=1),
                                  jnp.concatenate(q_parts, axis=1)], axis=0)


def _ssm_params(lr, li, ldt, bt_re, bt_im, c_re, c_im):
    g = lr.shape[0]
    vec = pl.BlockSpec((1, 2, 1, SSM_N), lambda i: (i, 0, 0, 0))
    mat = pl.BlockSpec((1, 2, SSM_P, SSM_N), lambda i: (i, 0, 0, 0))
    return pl.pallas_call(
        _ssm_params_kernel,
        grid=(g,),
        in_specs=[vec, vec, vec, mat, mat, mat, mat],
        out_specs=[
            pl.BlockSpec((1, SSM_LP, SSM_LP), lambda i: (i, 0, 0)),
            pl.BlockSpec((1, SSM_LP, 4 * SSM_N), lambda i: (i, 0, 0)),
            pl.BlockSpec((1, SSM_LP, 4 * SSM_N), lambda i: (i, 0, 0)),
            pl.BlockSpec((1, 2, 4 * SSM_N), lambda i: (i, 0, 0)),
        ],
        out_shape=[
            jax.ShapeDtypeStruct((g, SSM_LP, SSM_LP), BF16),
            jax.ShapeDtypeStruct((g, SSM_LP, 4 * SSM_N), BF16),
            jax.ShapeDtypeStruct((g, SSM_LP, 4 * SSM_N), BF16),
            jax.ShapeDtypeStruct((g, 2, 4 * SSM_N), F32),
        ],
        compiler_params=_cparams("parallel"),
        name="ssm_params",
    )(lr, li, ldt, bt_re, bt_im, c_re, c_im)


def _ssm_conv_kernel(u_ref, mt_ref, et_ref, fm_ref, dec_ref, y_ref, *, n_batch):
    u = u_ref[0]
    nc = u.shape[0]
    ncb = nc // n_batch
    w = 2 * SSM_N
    s_all = jnp.dot(u, et_ref[0], preferred_element_type=F32)
    dec = dec_ref[0]
    row = lax.broadcasted_iota(jnp.int32, (ncb, w), 0)

    def cmul(x, p, q):
        return p * x + q * pltpu.roll(x, SSM_N, 1)

    h_rows = []
    for b in range(n_batch):
        s_b = s_all[b * ncb:(b + 1) * ncb]
        xf, xb = s_b[:, :w], s_b[:, w:]
        pf, qf = dec[0:1, :w], dec[1:2, :w]
        pb, qb = dec[0:1, w:], dec[1:2, w:]
        k = 1
        while k < ncb:
            xf = xf + jnp.where(row >= k, cmul(pltpu.roll(xf, k, 0), pf, qf), 0.0)
            xb = xb + jnp.where(row < ncb - k, cmul(pltpu.roll(xb, ncb - k, 0), pb, qb), 0.0)
            pf, qf = pf * pf - qf * qf, 2.0 * pf * qf
            pb, qb = pb * pb - qb * qb, 2.0 * pb * qb
            k *= 2
        hf = jnp.where(row >= 1, pltpu.roll(xf, 1, 0), 0.0)
        hb = jnp.where(row < ncb - 1, pltpu.roll(xb, ncb - 1, 0), 0.0)
        h_rows.append(jnp.concatenate([hf, hb], axis=1))
    h = jnp.concatenate(h_rows, axis=0).astype(BF16)
    y = jnp.dot(u, mt_ref[0], preferred_element_type=F32)
    y = y + lax.dot_general(h, fm_ref[0], (((1,), (1,)), ((), ())), preferred_element_type=F32)
    y_ref[0] = y.astype(y_ref.dtype)


def _ssm_conv(ug, mt, et, fm, dec, n_batch):
    g, nc, lp = ug.shape
    return pl.pallas_call(
        functools.partial(_ssm_conv_kernel, n_batch=n_batch),
        grid=(g,),
        in_specs=[
            pl.BlockSpec((1, nc, lp), lambda i: (i, 0, 0)),
            pl.BlockSpec((1, lp, lp), lambda i: (i, 0, 0)),
            pl.BlockSpec((1, lp, 4 * SSM_N), lambda i: (i, 0, 0)),
            pl.BlockSpec((1, lp, 4 * SSM_N), lambda i: (i, 0, 0)),
            pl.BlockSpec((1, 2, 4 * SSM_N), lambda i: (i, 0, 0)),
        ],
        out_specs=pl.BlockSpec((1, nc, lp), lambda i: (i, 0, 0)),
        out_shape=jax.ShapeDtypeStruct((g, nc, lp), BF16),
        compiler_params=_cparams("parallel"),
        name="ssm_conv",
    )(ug, mt, et, fm, dec)


def _attn_kernel(qt_ref, k_ref, vt_ref, o_ref, q_sc, m_ref, acc_ref, sa_ref, sb_ref, *, tq, tk, nk):
    qt = qt_ref[...]
    for j in range(Q_PER_KV):
        q_sc[:, j * tq:(j + 1) * tq] = qt[j * HEAD_DIM:(j + 1) * HEAD_DIM, :]
    m_ref[...] = jnp.full(m_ref.shape, -jnp.inf, F32)
    acc_ref[...] = jnp.zeros(acc_ref.shape, F32)

    def scores(i):
        kb = k_ref[0, pl.ds(pl.multiple_of(i * tk, tk), tk), :]
        return jnp.dot(kb, q_sc[...], preferred_element_type=F32)

    def accumulate(s_ref, i):
        s = s_ref[...]
        m_prev = m_ref[...]
        m_new = jnp.maximum(m_prev, jnp.max(s, axis=0, keepdims=True))
        alpha = jnp.exp2(m_prev - m_new)
        p = jnp.exp2(s - m_new[0:1, :])
        acc_ref[:V_ROWS, :] = (acc_ref[:V_ROWS, :] * alpha[0:1, :]
                               + jnp.dot(vt_ref[0, i], p.astype(BF16), preferred_element_type=F32))
        m_ref[...] = m_new

    sa_ref[...] = scores(0)

    def body(j, carry):
        sb_ref[...] = scores(2 * j + 1)
        accumulate(sa_ref, 2 * j)
        sa_ref[...] = scores(2 * j + 2)
        accumulate(sb_ref, 2 * j + 1)
        return carry

    lax.fori_loop(0, nk // 2 - 1, body, 0)
    sb_ref[...] = scores(nk - 1)
    accumulate(sa_ref, nk - 2)
    accumulate(sb_ref, nk - 1)
    acc = acc_ref[...]
    out = (acc / acc[HEAD_DIM:HEAD_DIM + 1, :]).T
    o_ref[...] = jnp.concatenate([out[j * tq:(j + 1) * tq, :HEAD_DIM] for j in range(Q_PER_KV)], axis=1)


def _attention(qt, k, vt, n_batch, seq, tq):
    t = qt.shape[1]
    tk = vt.shape[3]
    nq = seq // tq
    nk = seq // tk
    assert nk % 2 == 0, "the key loop handles chunks in pairs"
    w = Q_PER_KV * HEAD_DIM
    return pl.pallas_call(
        functools.partial(_attn_kernel, tq=tq, tk=tk, nk=nk),
        grid=(n_batch, N_KV_HEADS, nq),
        in_specs=[
            pl.BlockSpec((w, tq), lambda b, g, i: (g, b * nq + i)),
            pl.BlockSpec((1, seq, HEAD_DIM), lambda b, g, i: (g, b, 0)),
            pl.BlockSpec((1, nk, V_ROWS, tk), lambda b, g, i: (g, b, 0, 0)),
        ],
        out_specs=pl.BlockSpec((tq, w), lambda b, g, i: (b * nq + i, g)),
        out_shape=jax.ShapeDtypeStruct((t, D_ATTN), F32),
        scratch_shapes=[
            pltpu.VMEM((HEAD_DIM, Q_PER_KV * tq), BF16),
            pltpu.VMEM((8, Q_PER_KV * tq), F32),
            pltpu.VMEM((2 * HEAD_DIM, Q_PER_KV * tq), F32),
            pltpu.VMEM((tk, Q_PER_KV * tq), F32),
            pltpu.VMEM((tk, Q_PER_KV * tq), F32),
        ],
        compiler_params=_cparams("parallel", "parallel", "parallel"),
        name="attention",
    )(qt, k, vt)


def _out_proj_kernel(x_ref, ys_ref, u_ref, at_ref, dsk_ref, wglu_ref, bglu_ref, gs_ref, ga_ref,
                     wo_ref, gf_ref, xo_ref, xnt_ref):
    y = ys_ref[...].astype(F32) + dsk_ref[...] * u_ref[...]
    y = jax.nn.gelu(y)
    z = jnp.dot(y.astype(BF16), wglu_ref[...], preferred_element_type=F32) + bglu_ref[...]
    y = y * jax.nn.sigmoid(z)
    ys = _rms(y, gs_ref[...]).astype(BF16)
    ya = _rms(at_ref[...], ga_ref[...]).astype(BF16)
    xo = (x_ref[...]
          + jnp.dot(ys, wo_ref[:D_SSM, :], preferred_element_type=F32)
          + jnp.dot(ya, wo_ref[D_SSM:, :], preferred_element_type=F32))
    xo_ref[...] = xo
    xnt_ref[...] = _rms(xo, gf_ref[...]).T.astype(BF16)


def _out_proj(x, ys, u, at, dsk, wglu, bglu, gs, ga, wo, gf, tm):
    t = x.shape[0]
    row = lambda w: pl.BlockSpec((tm, w), lambda i: (i, 0))
    full = lambda a, b: pl.BlockSpec((a, b), lambda i: (0, 0))
    return pl.pallas_call(
        _out_proj_kernel,
        grid=(t // tm,),
        in_specs=[row(D_MODEL), row(D_SSM), row(D_SSM), row(D_ATTN), full(1, D_SSM),
                  full(D_SSM, D_SSM), full(1, D_SSM), full(1, D_SSM), full(1, D_ATTN),
                  full(D_MODEL, D_MODEL), full(1, D_MODEL)],
        out_specs=[row(D_MODEL), pl.BlockSpec((D_MODEL, tm), lambda i: (0, i))],
        out_shape=[jax.ShapeDtypeStruct((t, D_MODEL), F32), jax.ShapeDtypeStruct((D_MODEL, t), BF16)],
        compiler_params=_cparams("parallel"),
        name="out_proj",
    )(x, ys, u, at, dsk, wglu, bglu, gs, ga, wo, gf)


def _oe_sort_network(n):
    pairs = []
    p = 1
    while p < n:
        k = p
        while k >= 1:
            for j in range(k % p, n - k, 2 * k):
                for i in range(min(k, n - j - k)):
                    if (i + j) // (2 * p) == (i + j + k) // (2 * p):
                        pairs.append((i + j, i + j + k))
            k //= 2
        p *= 2
    return pairs


def _ce(a, b):
    if b is None:
        return a, None
    if a is None:
        return b, None
    return jnp.maximum(a, b), jnp.minimum(a, b)


def _sorted_top16(slabs):
    n = PEER_TOPK
    v = list(slabs) + [None] * (n - len(slabs))
    for i, j in _oe_sort_network(n):
        v[i], v[j] = _ce(v[i], v[j])
    for shift in (4, 2, 1):
        partner = [None if x is None else pltpu.roll(x, shift, 0) for x in v]
        top = []
        for a in range(n):
            x, y = v[a], partner[n - 1 - a]
            top.append(y if x is None else (x if y is None else jnp.maximum(x, y)))
        v = top
        d = n // 2
        while d >= 1:
            for i in range(n):
                if i & d == 0:
                    v[i], v[i + d] = _ce(v[i], v[i + d])
            d //= 2
    return v


_CAND = [(a, b) for a in range(PEER_TOPK) for b in range(PEER_TOPK) if (a + 1) * (b + 1) <= PEER_TOPK]


def _count_prefix(pred, vals):
    pick = lambda m, a, b: jnp.where(m, a, b)
    m8 = pred(vals[7])
    m4 = pred(pick(m8, vals[11], vals[3]))
    m2 = pred(pick(m8, pick(m4, vals[13], vals[9]), pick(m4, vals[5], vals[1])))
    m1 = pred(pick(m8, pick(m4, pick(m2, vals[14], vals[12]), pick(m2, vals[10], vals[8])),
                   pick(m4, pick(m2, vals[6], vals[4]), pick(m2, vals[2], vals[0]))))
    cnt = pick(m8, 8.0, 0.0) + pick(m4, 4.0, 0.0) + pick(m2, 2.0, 0.0) + pick(m1, 1.0, 0.0)
    return cnt + pick(pred(vals[15]), 1.0, 0.0)


def _bf16_pair_words(x):
    b = pltpu.bitcast(x, jnp.uint32)
    b = (b + jnp.uint32(0x7FFF) + ((b >> 16) & jnp.uint32(1))) >> 16
    return b | (b << 16)


def _peer_route_kernel(xnt_ref, wq_ref, k1_ref, k2_ref, rho_ref, e2_ref, r1_ref, c1_ref):
    def scores(h):
        qt = jnp.dot(wq_ref[h * PEER_KEY_DIM:(h + 1) * PEER_KEY_DIM, :], xnt_ref[...],
                     preferred_element_type=F32)
        return (jnp.dot(k1_ref[h], qt[:PEER_HALF].astype(BF16), preferred_element_type=F32),
                jnp.dot(k2_ref[h], qt[PEER_HALF:].astype(BF16), preferred_element_type=F32))

    s = scores(0)
    for h in range(PEER_HEADS):
        s_next = scores(h + 1) if h + 1 < PEER_HEADS else None
        _peer_route_head(h, s[0], s[1], rho_ref, e2_ref, r1_ref, c1_ref)
        s = s_next


def _peer_route_head(h, s1, s2, rho_ref, e2_ref, r1_ref, c1_ref):
    tm = s1.shape[1]
    nslab = N_KEYS // SUBLANE
    s1 = [s1[SUBLANE * i:SUBLANE * (i + 1)] for i in range(nslab)]
    s2 = [s2[SUBLANE * i:SUBLANE * (i + 1)] for i in range(nslab)]
    v1 = _sorted_top16(s1)
    v2 = _sorted_top16(s2)

    sub = lax.broadcasted_iota(jnp.int32, (SUBLANE, tm), 0)
    cand = [v1[a] + v2[b] for a, b in _CAND]
    packed = []
    for base in range(0, len(cand), 8):
        slab = None
        for s, c in enumerate(cand[base:base + 8]):
            slab = c if slab is None else jnp.where(sub == s, c, slab)
        if len(cand) - base < 8:
            slab = jnp.where(sub < len(cand) - base, slab, -jnp.inf)
        packed.append(slab)
    best = _sorted_top16(packed)
    theta = best[PEER_TOPK - 1]
    z = jnp.ones((SUBLANE, tm), F32)
    for c in best[1:]:
        z = z + jnp.exp(c - best[0])
    inv_z = 1.0 / z

    r1, c1, rho, e2 = [], [], [], []
    for i in range(nslab):
        r1.append(_count_prefix(lambda v, x=s1[i]: x + v >= theta, v2))
        rho.append(_count_prefix(lambda v, x=s2[i]: v > x, v2))
        c1.append(jnp.exp(s1[i] - v1[0]) * inv_z)
        e2.append(jnp.exp(s2[i] - v2[0]))
    r1 = _bf16_pair_words(jnp.concatenate(r1, axis=0))
    c1 = _bf16_pair_words(jnp.concatenate(c1, axis=0))
    rho = jnp.concatenate(rho, axis=0).astype(BF16)
    e2 = jnp.concatenate(e2, axis=0).astype(BF16)
    for l in range(tm // LANE):
        lanes = slice(l * LANE, (l + 1) * LANE)
        r1_ref[h, l] = r1[:, lanes]
        c1_ref[h, l] = c1[:, lanes]
    for l in range(tm // PEER_STRIP):
        lanes = slice(l * PEER_STRIP, (l + 1) * PEER_STRIP)
        rho_ref[h, l * N_KEYS:(l + 1) * N_KEYS, :] = rho[:, lanes]
        e2_ref[h, l * N_KEYS:(l + 1) * N_KEYS, :] = e2[:, lanes]


def _peer_route(xn_t, wq_t, k1, k2, tm):
    t = xn_t.shape[1]
    sel = pl.BlockSpec((PEER_HEADS, tm // LANE, N_KEYS, LANE), lambda i: (0, i, 0, 0))
    shp = jax.ShapeDtypeStruct((PEER_HEADS, t // LANE, N_KEYS, LANE), jnp.uint32)
    tok = pl.BlockSpec((PEER_HEADS, tm // 2, PEER_STRIP), lambda i: (0, i, 0))
    shp16 = jax.ShapeDtypeStruct((PEER_HEADS, t // 2, PEER_STRIP), BF16)
    return pl.pallas_call(
        _peer_route_kernel,
        grid=(t // tm,),
        in_specs=[
            pl.BlockSpec((D_MODEL, tm), lambda i: (0, i)),
            pl.BlockSpec((PEER_HEADS * PEER_KEY_DIM, D_MODEL), lambda i: (0, 0)),
            pl.BlockSpec((PEER_HEADS, N_KEYS, PEER_HALF), lambda i: (0, 0, 0)),
            pl.BlockSpec((PEER_HEADS, N_KEYS, PEER_HALF), lambda i: (0, 0, 0)),
        ],
        out_specs=[tok, tok, sel, sel],
        out_shape=[shp16, shp16, shp, shp],
        compiler_params=_cparams("parallel"),
        name="peer_route",
    )(xn_t, wq_t, k1, k2)


def _gelu_tanh(x):
    k1 = -2.0 * math.sqrt(2.0 / math.pi) * LOG2_E
    t = x * (k1 + (k1 * 0.044715) * (x * x))
    return x / (1.0 + jnp.exp2(t))


def _load_row_repeated(ref, h, l, i, n):
    return ref[h, l, pl.ds(i, n, stride=0), :]


def _peer_dense_kernel(xn_ref, u_ref, vt_ref, rho_ref, e2_ref, r1_ref, c1_ref, o_ref, a_ref, *, rows):
    e = pl.program_id(1)

    @pl.when(e == 0)
    def _():
        o_ref[...] = jnp.zeros(o_ref.shape, F32)

    tt = xn_ref.shape[1]
    sub = BF16_ROWS
    strip = PEER_STRIP
    zero = jnp.zeros((sub, strip), BF16)

    def row_tile(ref, h, s, i):
        words = jnp.concatenate([_load_row_repeated(ref, h, s * (strip // LANE) + l, i, sub // 2)
                                 for l in range(strip // LANE)], axis=1)
        return pltpu.bitcast(words, BF16)

    blk = PEER_BLOCK_ROWS * N_KEYS
    nblk = rows // PEER_BLOCK_ROWS

    def first_matmul(k):
        return jnp.dot(u_ref[k * blk:(k + 1) * blk, :].astype(BF16), xn_ref[...], preferred_element_type=F32)

    a_ref[0:blk, :] = _gelu_tanh(first_matmul(0).astype(BF16))
    for k in range(nblk):
        if k + 1 < nblk:
            a_ref[(k + 1) * blk:(k + 2) * blk, :] = _gelu_tanh(first_matmul(k + 1).astype(BF16))
        for s in range(tt // strip):
            lanes = slice(s * strip, (s + 1) * strip)
            for ii in range(PEER_BLOCK_ROWS):
                i = k * PEER_BLOCK_ROWS + ii
                w = [None] * (N_KEYS // sub)
                for h in range(PEER_HEADS):
                    r = row_tile(r1_ref, h, s, i)
                    c = row_tile(c1_ref, h, s, i)
                    for jb in range(N_KEYS // sub):
                        keys = slice(s * N_KEYS + jb * sub, s * N_KEYS + (jb + 1) * sub)
                        term = c * jnp.where(rho_ref[h, keys, :] < r, e2_ref[h, keys, :], zero)
                        w[jb] = term if w[jb] is None else w[jb] + term
                for jb in range(N_KEYS // sub):
                    lo = k * blk + ii * N_KEYS + jb * sub
                    a_ref[lo:lo + sub, lanes] = a_ref[lo:lo + sub, lanes] * w[jb]
        o_ref[...] += lax.dot_general(vt_ref[k * blk:(k + 1) * blk, :].astype(BF16),
                                      a_ref[k * blk:(k + 1) * blk, :], (((0,), (0,)), ((), ())),
                                      preferred_element_type=F32)


def _peer_dense(xn_t, eu, ev, layer, rho, e2, r1, c1, tt, te):
    t = xn_t.shape[1]
    ne = eu.shape[1]
    rows = te // N_KEYS
    tok = pl.BlockSpec((PEER_HEADS, tt // 2, PEER_STRIP), lambda j, e: (0, j, 0))
    sel = pl.BlockSpec((PEER_HEADS, tt // LANE, rows, LANE), lambda j, e: (0, j, e, 0))
    return pl.pallas_call(
        functools.partial(_peer_dense_kernel, rows=rows),
        grid=(t // tt, ne // te),
        in_specs=[
            pl.BlockSpec((D_MODEL, tt), lambda j, e: (0, j)),
            pl.BlockSpec((None, te, D_MODEL), lambda j, e: (layer, e, 0)),
            pl.BlockSpec((None, te, D_MODEL), lambda j, e: (layer, e, 0)),
            tok, tok, sel, sel,
        ],
        out_specs=pl.BlockSpec((D_MODEL, tt), lambda j, e: (0, j)),
        out_shape=jax.ShapeDtypeStruct((D_MODEL, t), F32),
        scratch_shapes=[pltpu.VMEM((te, tt), BF16)],
        compiler_params=_cparams("parallel", "arbitrary"),
        name="peer_dense",
    )(xn_t, eu, ev, rho, e2, r1, c1)


def _ple_kernel(x_ref, pe_ref, p_ref, g_ref, wg_ref, wp_ref, gfin_ref, o_ref, *, final):
    x = x_ref[...] + pe_ref[...].T
    gate = jax.nn.sigmoid(jnp.dot(_rms(x, g_ref[...]).astype(BF16), wg_ref[...], preferred_element_type=F32))
    x = x + gate * jnp.dot(p_ref[...].astype(BF16), wp_ref[...], preferred_element_type=F32)
    if final:
        x = _rms(x, gfin_ref[...])
    o_ref[...] = x


def _ple(x, pe, p, g, wg, wp, gfin, final, tm):
    t = x.shape[0]
    row = lambda w: pl.BlockSpec((tm, w), lambda i: (i, 0))
    full = lambda a, b: pl.BlockSpec((a, b), lambda i: (0, 0))
    return pl.pallas_call(
        functools.partial(_ple_kernel, final=final),
        grid=(t // tm,),
        in_specs=[row(D_MODEL), pl.BlockSpec((D_MODEL, tm), lambda i: (0, i)), row(D_PLE), full(1, D_MODEL),
                  full(D_MODEL, D_MODEL), full(D_PLE, D_MODEL), full(1, D_MODEL)],
        out_specs=row(D_MODEL),
        out_shape=jax.ShapeDtypeStruct((t, D_MODEL), F32),
        compiler_params=_cparams("parallel"),
        name="ple",
    )(x, pe, p, g, wg, wp, gfin)


def _rope_tables(seq):
    n_rows = seq // GRID_W
    row = jnp.repeat(jnp.arange(n_rows, dtype=jnp.int32), GRID_W).astype(F32)
    col = jnp.tile(jnp.arange(GRID_W, dtype=jnp.int32), n_rows).astype(F32)
    quarter = HEAD_DIM // 4
    inv = ROPE_THETA ** (-jnp.arange(quarter, dtype=F32) / quarter)
    ang_r = row[:, None] * inv[None, :]
    ang_c = col[:, None] * inv[None, :]
    cos = jnp.concatenate([jnp.cos(ang_r)] * 2 + [jnp.cos(ang_c)] * 2, axis=1)
    sin = jnp.concatenate([-jnp.sin(ang_r), jnp.sin(ang_r), -jnp.sin(ang_c), jnp.sin(ang_c)], axis=1)
    return jnp.tile(cos, (1, N_Q_HEADS)), jnp.tile(sin, (1, N_Q_HEADS))


class _Tiles(NamedTuple):
    tokens: int
    queries: int
    keys: int
    route_tokens: int
    peer_tokens: int
    peer_experts: int


def _tiles(n_batch, seq):
    t = n_batch * seq
    return _Tiles(tokens=min(512, seq), queries=min(1024, seq), keys=min(512, seq // 2),
                  route_tokens=min(2 * LANE, t), peer_tokens=min(512, t), peer_experts=2048)


def _forward(x, p, norm_mix_g, w_in, lam_re, lam_im, log_dt, b_re, b_im, c_re, c_im, d_skip, w_glu,
             b_glu, q_norm_g, k_norm_g, ssm_out_g, attn_out_g, w_out, norm_ffn_g, w_query, sub_keys1,
             sub_keys2, expert_u, expert_v, norm_ple_g, w_ple_gate, w_ple_proj, final_norm_g):
    n_batch, seq, _ = x.shape
    depth = w_in.shape[0]
    t = n_batch * seq
    tl = _tiles(n_batch, seq)
    tm, tq, tk, tt, te = tl.tokens, tl.queries, tl.keys, tl.peer_tokens, tl.peer_experts
    nc = t // SSM_CHUNK

    cq, sq = _rope_tables(seq)
    bd = jnp.kron(jnp.eye(N_Q_HEADS, dtype=F32), jnp.ones((HEAD_DIM, HEAD_DIM), F32)).astype(BF16)
    row = lambda a: a.reshape(1, -1).astype(F32)

    xf = x.reshape(t, D_MODEL)
    for i in range(depth):
        u, ub, qt, k, vt = _in_proj(xf, row(norm_mix_g[i]), w_in[i].astype(BF16), cq, sq,
                                row(jnp.tile(q_norm_g[i], N_Q_HEADS)), row(jnp.tile(k_norm_g[i], N_KV_HEADS)),
                                bd, seq, tk)

        grp = lambda a: jnp.transpose(a, (1, 0, 2))[:, :, None, :]
        ldt = jnp.broadcast_to(log_dt[i][:, :, None], (2, SSM_G, SSM_N))
        mt, et, fm, dec = _ssm_params(
            grp(lam_re[i]), grp(lam_im[i]), grp(ldt),
            jnp.transpose(b_re[i], (1, 0, 3, 2)), jnp.transpose(b_im[i], (1, 0, 3, 2)),
            jnp.transpose(c_re[i], (1, 0, 2, 3)), jnp.transpose(c_im[i], (1, 0, 2, 3)))
        ug = (ub.reshape(nc, SSM_CHUNK, SSM_G, SSM_P)
              .transpose(2, 0, 1, 3).reshape(SSM_G, nc, SSM_LP))
        yg = _ssm_conv(ug, mt, et, fm, dec, n_batch)
        y_ssm = yg.reshape(SSM_G, nc, SSM_CHUNK, SSM_P).transpose(1, 2, 0, 3).reshape(t, D_SSM)

        attn = _attention(qt, k, vt, n_batch, seq, tq)

        xf, xnt = _out_proj(xf, y_ssm, u, attn, row(d_skip[i]), w_glu[i].astype(BF16), row(b_glu[i]),
                            row(ssm_out_g[i]), row(attn_out_g[i]), w_out[i].astype(BF16),
                            row(norm_ffn_g[i]), tm)

        rho, e2, r1, c1 = _peer_route(xnt, w_query[i].T.astype(BF16), sub_keys1[i].astype(BF16),
                                      sub_keys2[i].astype(BF16), tl.route_tokens)
        peer_t = _peer_dense(xnt, expert_u, expert_v, i,
                             rho, e2, r1, c1, tt, te)

        xf = _ple(xf, peer_t, p[i].reshape(t, D_PLE), row(norm_ple_g[i]), w_ple_gate[i].astype(BF16),
                  w_ple_proj[i].astype(BF16), row(final_norm_g), i == depth - 1, tm)
    return xf.reshape(n_batch, seq, D_MODEL)


def kernel(x, p, norm_mix_g, w_in, lam_re, lam_im, log_dt, b_re, b_im, c_re, c_im, d_skip, w_glu, b_glu, q_norm_g, k_norm_g, ssm_out_g, attn_out_g, w_out, norm_ffn_g, w_query, sub_keys1, sub_keys2, expert_u, expert_v, norm_ple_g, w_ple_gate, w_ple_proj, final_norm_g):
    return _forward(x, p, norm_mix_g, w_in, lam_re, lam_im, log_dt, b_re, b_im, c_re, c_im, d_skip, w_glu,
                    b_glu, q_norm_g, k_norm_g, ssm_out_g, attn_out_g, w_out, norm_ffn_g, w_query,
                    sub_keys1, sub_keys2, expert_u, expert_v, norm_ple_g, w_ple_gate, w_ple_proj,
                    final_norm_g)
```

```python
import functools
import math
from typing import NamedTuple

import jax
import jax.numpy as jnp
from jax import lax
from jax.experimental import pallas as pl
from jax.experimental.pallas import tpu as pltpu

F32 = jnp.float32
BF16 = jnp.bfloat16

D_MODEL = 1024
D_PLE = 256
GRID_W = 64
EPS = 1e-6
LOG2_E = 1.4426950408889634
D_SSM = 512
SSM_P = 16
SSM_G = D_SSM // SSM_P
SSM_N = 64
D_ATTN = 512
HEAD_DIM = 64
V_ROWS = HEAD_DIM + 16
N_Q_HEADS = 8
N_KV_HEADS = 2
Q_PER_KV = 4
D_KV = 128
D_IN = D_SSM + D_ATTN + 2 * D_KV
ROPE_THETA = 10000.0
PEER_HEADS = 8
PEER_KEY_DIM = 256
PEER_HALF = 128
N_KEYS = 128
PEER_TOPK = 16

LANE = 128
SUBLANE = 8
BF16_ROWS = 2 * SUBLANE
PEER_BLOCK_ROWS = 8
PEER_STRIP = 2 * LANE
SSM_CHUNK = 32
SSM_LP = SSM_CHUNK * SSM_P
VMEM_LIMIT = 48 * 1024 * 1024

def _cparams(*sem):
    return pltpu.CompilerParams(dimension_semantics=sem, vmem_limit_bytes=VMEM_LIMIT)


def _rms(x, g):
    return x * lax.rsqrt(jnp.mean(x * x, axis=-1, keepdims=True) + EPS) * g


def _in_proj_kernel(x_ref, g_ref, w_ref, cq_ref, sq_ref, gq_ref, gk_ref, bd_ref,
                    u_ref, ub_ref, qt_ref, k_ref, vt_ref):
    h = _rms(x_ref[...], g_ref[...])
    z = jnp.dot(h.astype(BF16), w_ref[...], preferred_element_type=F32)
    u_ref[...] = z[:, :D_SSM]
    ub_ref[...] = z[:, :D_SSM].astype(BF16)

    def head_norm_rope(t, gain, bd, c, s):
        ss = jnp.dot((t * t).astype(BF16), bd, preferred_element_type=F32)
        t = t * lax.rsqrt(ss * (1.0 / HEAD_DIM) + EPS) * gain
        w = t.shape[1]
        lane = lax.broadcasted_iota(jnp.int32, t.shape, 1)
        partner = jnp.where((lane % 32) < 16, pltpu.roll(t, w - 16, 1), pltpu.roll(t, 16, 1))
        return t * c + partner * s

    cq = cq_ref[...]
    sq = sq_ref[...]
    q = head_norm_rope(z[:, D_SSM:D_SSM + D_ATTN], gq_ref[...], bd_ref[...], cq, sq)
    qt_ref[...] = (q * (HEAD_DIM ** -0.5 * LOG2_E)).T.astype(BF16)
    k = head_norm_rope(z[:, D_SSM + D_ATTN:D_SSM + D_ATTN + D_KV], gk_ref[...],
                       bd_ref[:D_KV, :D_KV], cq[:, :D_KV], sq[:, :D_KV])
    vt = z[:, D_SSM + D_ATTN + D_KV:].T
    row = lax.broadcasted_iota(jnp.int32, (V_ROWS - HEAD_DIM, vt.shape[1]), 0)
    ones_row = jnp.where(row == 0, 1.0, 0.0).astype(F32)
    for j in range(N_KV_HEADS):
        k_ref[j] = k[:, j * HEAD_DIM:(j + 1) * HEAD_DIM].astype(BF16)
        vt_ref[j, 0] = jnp.concatenate([vt[j * HEAD_DIM:(j + 1) * HEAD_DIM], ones_row], axis=0).astype(BF16)


def _in_proj(x, g, w, cq, sq, gq, gk, bd, seq, tm):
    t = x.shape[0]
    nseq = seq // tm
    return pl.pallas_call(
        _in_proj_kernel,
        grid=(t // tm,),
        in_specs=[
            pl.BlockSpec((tm, D_MODEL), lambda i: (i, 0)),
            pl.BlockSpec((1, D_MODEL), lambda i: (0, 0)),
            pl.BlockSpec((D_MODEL, D_IN), lambda i: (0, 0)),
            pl.BlockSpec((tm, D_ATTN), lambda i: (i % nseq, 0)),
            pl.BlockSpec((tm, D_ATTN), lambda i: (i % nseq, 0)),
            pl.BlockSpec((1, D_ATTN), lambda i: (0, 0)),
            pl.BlockSpec((1, D_KV), lambda i: (0, 0)),
            pl.BlockSpec((D_ATTN, D_ATTN), lambda i: (0, 0)),
        ],
        out_specs=[
            pl.BlockSpec((tm, D_SSM), lambda i: (i, 0)),
            pl.BlockSpec((tm, D_SSM), lambda i: (i, 0)),
            pl.BlockSpec((D_ATTN, tm), lambda i: (0, i)),
            pl.BlockSpec((N_KV_HEADS, tm, HEAD_DIM), lambda i: (0, i, 0)),
            pl.BlockSpec((N_KV_HEADS, 1, V_ROWS, tm), lambda i: (0, i, 0, 0)),
        ],
        out_shape=[
            jax.ShapeDtypeStruct((t, D_SSM), F32),
            jax.ShapeDtypeStruct((t, D_SSM), BF16),
            jax.ShapeDtypeStruct((D_ATTN, t), BF16),
            jax.ShapeDtypeStruct((N_KV_HEADS, t, HEAD_DIM), BF16),
            jax.ShapeDtypeStruct((N_KV_HEADS, t // tm, V_ROWS, tm), BF16),
        ],
        compiler_params=_cparams("parallel"),
        name="in_proj",
    )(x, g, w, cq, sq, gq, gk, bd)


def _ssm_params_kernel(lr_ref, li_ref, ldt_ref, bt_re_ref, bt_im_ref, c_re_ref, c_im_ref,
                       mt_ref, et_ref, fm_ref, dec_ref):
    L = SSM_CHUNK
    hi = lax.Precision.HIGHEST
    r_parts, e_parts, f_parts, p_parts, q_parts = [], [], [], [], []
    idx = lax.broadcasted_iota(jnp.int32, (L, SSM_N), 0).astype(F32)
    for d in range(2):
        lr = lr_ref[0, d]
        li = li_ref[0, d]
        dt = jnp.exp(ldt_ref[0, d])

        def apow(tau):
            mag = jnp.exp(tau * (lr * dt))
            ang = tau * (li * dt)
            return mag * jnp.cos(ang), mag * jnp.sin(ang)

        def apow_rows(tau):
            pr, pi = apow(tau)
            return pr[:, None, :], pi[:, None, :]

        ab_re, ab_im = apow(jnp.ones((1, SSM_N), F32))
        den = lr * lr + li * li
        nr = ab_re - 1.0
        f_re = (nr * lr + ab_im * li) / den
        f_im = (ab_im * lr - nr * li) / den
        bt_re = bt_re_ref[0, d]
        bt_im = bt_im_ref[0, d]
        bb_re = f_re * bt_re - f_im * bt_im
        bb_im = f_re * bt_im + f_im * bt_re
        c_re = c_re_ref[0, d]
        c_im = c_im_ref[0, d]

        tau_k = idx if d == 0 else (L - 1.0) - idx
        pr, pi = apow_rows(tau_k)
        lm_re = (c_re[None] * pr - c_im[None] * pi).reshape(L * SSM_P, SSM_N)
        lm_im = (c_re[None] * pi + c_im[None] * pr).reshape(L * SSM_P, SSM_N)
        nt = (((1,), (1,)), ((), ()))
        r_parts.append(lax.dot_general(bb_re, lm_re, nt, precision=hi, preferred_element_type=F32)
                       - lax.dot_general(bb_im, lm_im, nt, precision=hi, preferred_element_type=F32))

        tau_e = (L - 1.0) - idx if d == 0 else idx
        pr, pi = apow_rows(tau_e)
        e_parts.append((pr * bb_re[None] - pi * bb_im[None]).reshape(L * SSM_P, SSM_N))
        e_parts.append((pr * bb_im[None] + pi * bb_re[None]).reshape(L * SSM_P, SSM_N))

        tau_f = idx + 1.0 if d == 0 else L - idx
        pr, pi = apow_rows(tau_f)
        f_parts.append((c_re[None] * pr - c_im[None] * pi).reshape(L * SSM_P, SSM_N))
        f_parts.append(-(c_re[None] * pi + c_im[None] * pr).reshape(L * SSM_P, SSM_N))

        al_re, al_im = apow(jnp.full((1, SSM_N), float(L), F32))
        p_parts += [al_re, al_re]
        q_parts += [-al_im, al_im]

    zeros = jnp.zeros((SSM_P, SSM_LP), F32)
    r_bwd = jnp.concatenate([r_parts[1], zeros], axis=1)
    r_fwd = pltpu.roll(jnp.concatenate([r_parts[0], zeros], axis=1), SSM_LP - SSM_P, 1)
    r_cat = r_bwd + r_fwd
    for s in range(L):
        shift = (2 * SSM_LP - (L - 1 - s) * SSM_P) % (2 * SSM_LP)
        win = r_cat if shift == 0 else pltpu.roll(r_cat, shift, 1)
        mt_ref[0, s * SSM_P:(s + 1) * SSM_P, :] = win[:, :SSM_LP].astype(BF16)
    et_ref[0] = jnp.concatenate(e_parts, axis=1).astype(BF16)
    fm_ref[0] = jnp.concatenate(f_parts, axis=1).astype(BF16)
    dec_ref[0] = jnp.concatenate([jnp.concatenate(p_parts, axis=1),
                                  jnp.concatenate(q_parts, axis=1)], axis=0)


def _ssm_params(lr, li, ldt, bt_re, bt_im, c_re, c_im):
    g = lr.shape[0]
    vec = pl.BlockSpec((1, 2, 1, SSM_N), lambda i: (i, 0, 0, 0))
    mat = pl.BlockSpec((1, 2, SSM_P, SSM_N), lambda i: (i, 0, 0, 0))
    return pl.pallas_call(
        _ssm_params_kernel,
        grid=(g,),
        in_specs=[vec, vec, vec, mat, mat, mat, mat],
        out_specs=[
            pl.BlockSpec((1, SSM_LP, SSM_LP), lambda i: (i, 0, 0)),
            pl.BlockSpec((1, SSM_LP, 4 * SSM_N), lambda i: (i, 0, 0)),
            pl.BlockSpec((1, SSM_LP, 4 * SSM_N), lambda i: (i, 0, 0)),
            pl.BlockSpec((1, 2, 4 * SSM_N), lambda i: (i, 0, 0)),
        ],
        out_shape=[
            jax.ShapeDtypeStruct((g, SSM_LP, SSM_LP), BF16),
            jax.ShapeDtypeStruct((g, SSM_LP, 4 * SSM_N), BF16),
            jax.ShapeDtypeStruct((g, SSM_LP, 4 * SSM_N), BF16),
            jax.ShapeDtypeStruct((g, 2, 4 * SSM_N), F32),
        ],
        compiler_params=_cparams("parallel"),
        name="ssm_params",
    )(lr, li, ldt, bt_re, bt_im, c_re, c_im)


def _ssm_conv_kernel(u_ref, mt_ref, et_ref, fm_ref, dec_ref, y_ref, *, n_batch):
    u = u_ref[0]
    nc = u.shape[0]
    ncb = nc // n_batch
    w = 2 * SSM_N
    s_all = jnp.dot(u, et_ref[0], preferred_element_type=F32)
    dec = dec_ref[0]
    row = lax.broadcasted_iota(jnp.int32, (ncb, w), 0)

    def cmul(x, p, q):
        return p * x + q * pltpu.roll(x, SSM_N, 1)

    h_rows = []
    for b in range(n_batch):
        s_b = s_all[b * ncb:(b + 1) * ncb]
        xf, xb = s_b[:, :w], s_b[:, w:]
        pf, qf = dec[0:1, :w], dec[1:2, :w]
        pb, qb = dec[0:1, w:], dec[1:2, w:]
        k = 1
        while k < ncb:
            xf = xf + jnp.where(row >= k, cmul(pltpu.roll(xf, k, 0), pf, qf), 0.0)
            xb = xb + jnp.where(row < ncb - k, cmul(pltpu.roll(xb, ncb - k, 0), pb, qb), 0.0)
            pf, qf = pf * pf - qf * qf, 2.0 * pf * qf
            pb, qb = pb * pb - qb * qb, 2.0 * pb * qb
            k *= 2
        hf = jnp.where(row >= 1, pltpu.roll(xf, 1, 0), 0.0)
        hb = jnp.where(row < ncb - 1, pltpu.roll(xb, ncb - 1, 0), 0.0)
        h_rows.append(jnp.concatenate([hf, hb], axis=1))
    h = jnp.concatenate(h_rows, axis=0).astype(BF16)
    y = jnp.dot(u, mt_ref[0], preferred_element_type=F32)
    y = y + lax.dot_general(h, fm_ref[0], (((1,), (1,)), ((), ())), preferred_element_type=F32)
    y_ref[0] = y.astype(y_ref.dtype)


def _ssm_conv(ug, mt, et, fm, dec, n_batch):
    g, nc, lp = ug.shape
    return pl.pallas_call(
        functools.partial(_ssm_conv_kernel, n_batch=n_batch),
        grid=(g,),
        in_specs=[
            pl.BlockSpec((1, nc, lp), lambda i: (i, 0, 0)),
            pl.BlockSpec((1, lp, lp), lambda i: (i, 0, 0)),
            pl.BlockSpec((1, lp, 4 * SSM_N), lambda i: (i, 0, 0)),
            pl.BlockSpec((1, lp, 4 * SSM_N), lambda i: (i, 0, 0)),
            pl.BlockSpec((1, 2, 4 * SSM_N), lambda i: (i, 0, 0)),
        ],
        out_specs=pl.BlockSpec((1, nc, lp), lambda i: (i, 0, 0)),
        out_shape=jax.ShapeDtypeStruct((g, nc, lp), BF16),
        compiler_params=_cparams("parallel"),
        name="ssm_conv",
    )(ug, mt, et, fm, dec)


def _attn_kernel(qt_ref, k_ref, vt_ref, o_ref, q_sc, m_ref, acc_ref, sa_ref, sb_ref, *, tq, tk, nk):
    qt = qt_ref[...]
    for j in range(Q_PER_KV):
        q_sc[:, j * tq:(j + 1) * tq] = qt[j * HEAD_DIM:(j + 1) * HEAD_DIM, :]
    m_ref[...] = jnp.full(m_ref.shape, -jnp.inf, F32)
    acc_ref[...] = jnp.zeros(acc_ref.shape, F32)

    def scores(i):
        kb = k_ref[0, pl.ds(pl.multiple_of(i * tk, tk), tk), :]
        return jnp.dot(kb, q_sc[...], preferred_element_type=F32)

    def accumulate(s_ref, i):
        s = s_ref[...]
        m_prev = m_ref[...]
        m_new = jnp.maximum(m_prev, jnp.max(s, axis=0, keepdims=True))
        alpha = jnp.exp2(m_prev - m_new)
        p = jnp.exp2(s - m_new[0:1, :])
        acc_ref[:V_ROWS, :] = (acc_ref[:V_ROWS, :] * alpha[0:1, :]
                               + jnp.dot(vt_ref[0, i], p.astype(BF16), preferred_element_type=F32))
        m_ref[...] = m_new

    sa_ref[...] = scores(0)

    def body(j, carry):
        sb_ref[...] = scores(2 * j + 1)
        accumulate(sa_ref, 2 * j)
        sa_ref[...] = scores(2 * j + 2)
        accumulate(sb_ref, 2 * j + 1)
        return carry

    lax.fori_loop(0, nk // 2 - 1, body, 0)
    sb_ref[...] = scores(nk - 1)
    accumulate(sa_ref, nk - 2)
    accumulate(sb_ref, nk - 1)
    acc = acc_ref[...]
    out = (acc / acc[HEAD_DIM:HEAD_DIM + 1, :]).T
    o_ref[...] = jnp.concatenate([out[j * tq:(j + 1) * tq, :HEAD_DIM] for j in range(Q_PER_KV)], axis=1)


def _attention(qt, k, vt, n_batch, seq, tq):
    t = qt.shape[1]
    tk = vt.shape[3]
    nq = seq // tq
    nk = seq // tk
    assert nk % 2 == 0, "the key loop handles chunks in pairs"
    w = Q_PER_KV * HEAD_DIM
    return pl.pallas_call(
        functools.partial(_attn_kernel, tq=tq, tk=tk, nk=nk),
        grid=(n_batch, N_KV_HEADS, nq),
        in_specs=[
            pl.BlockSpec((w, tq), lambda b, g, i: (g, b * nq + i)),
            pl.BlockSpec((1, seq, HEAD_DIM), lambda b, g, i: (g, b, 0)),
            pl.BlockSpec((1, nk, V_ROWS, tk), lambda b, g, i: (g, b, 0, 0)),
        ],
        out_specs=pl.BlockSpec((tq, w), lambda b, g, i: (b * nq + i, g)),
        out_shape=jax.ShapeDtypeStruct((t, D_ATTN), F32),
        scratch_shapes=[
            pltpu.VMEM((HEAD_DIM, Q_PER_KV * tq), BF16),
            pltpu.VMEM((8, Q_PER_KV * tq), F32),
            pltpu.VMEM((2 * HEAD_DIM, Q_PER_KV * tq), F32),
            pltpu.VMEM((tk, Q_PER_KV * tq), F32),
            pltpu.VMEM((tk, Q_PER_KV * tq), F32),
        ],
        compiler_params=_cparams("parallel", "parallel", "parallel"),
        name="attention",
    )(qt, k, vt)


def _out_proj_kernel(x_ref, ys_ref, u_ref, at_ref, dsk_ref, wglu_ref, bglu_ref, gs_ref, ga_ref,
                     wo_ref, gf_ref, xo_ref, xnt_ref):
    y = ys_ref[...].astype(F32) + dsk_ref[...] * u_ref[...]
    y = jax.nn.gelu(y)
    z = jnp.dot(y.astype(BF16), wglu_ref[...], preferred_element_type=F32) + bglu_ref[...]
    y = y * jax.nn.sigmoid(z)
    ys = _rms(y, gs_ref[...]).astype(BF16)
    ya = _rms(at_ref[...], ga_ref[...]).astype(BF16)
    xo = (x_ref[...]
          + jnp.dot(ys, wo_ref[:D_SSM, :], preferred_element_type=F32)
          + jnp.dot(ya, wo_ref[D_SSM:, :], preferred_element_type=F32))
    xo_ref[...] = xo
    xnt_ref[...] = _rms(xo, gf_ref[...]).T.astype(BF16)


def _out_proj(x, ys, u, at, dsk, wglu, bglu, gs, ga, wo, gf, tm):
    t = x.shape[0]
    row = lambda w: pl.BlockSpec((tm, w), lambda i: (i, 0))
    full = lambda a, b: pl.BlockSpec((a, b), lambda i: (0, 0))
    return pl.pallas_call(
        _out_proj_kernel,
        grid=(t // tm,),
        in_specs=[row(D_MODEL), row(D_SSM), row(D_SSM), row(D_ATTN), full(1, D_SSM),
                  full(D_SSM, D_SSM), full(1, D_SSM), full(1, D_SSM), full(1, D_ATTN),
                  full(D_MODEL, D_MODEL), full(1, D_MODEL)],
        out_specs=[row(D_MODEL), pl.BlockSpec((D_MODEL, tm), lambda i: (0, i))],
        out_shape=[jax.ShapeDtypeStruct((t, D_MODEL), F32), jax.ShapeDtypeStruct((D_MODEL, t), BF16)],
        compiler_params=_cparams("parallel"),
        name="out_proj",
    )(x, ys, u, at, dsk, wglu, bglu, gs, ga, wo, gf)


def _oe_sort_network(n):
    pairs = []
    p = 1
    while p < n:
        k = p
        while k >= 1:
            for j in range(k % p, n - k, 2 * k):
                for i in range(min(k, n - j - k)):
                    if (i + j) // (2 * p) == (i + j + k) // (2 * p):
                        pairs.append((i + j, i + j + k))
            k //= 2
        p *= 2
    return pairs


def _ce(a, b):
    if b is None:
        return a, None
    if a is None:
        return b, None
    return jnp.maximum(a, b), jnp.minimum(a, b)


def _sorted_top16(slabs):
    n = PEER_TOPK
    v = list(slabs) + [None] * (n - len(slabs))
    for i, j in _oe_sort_network(n):
        v[i], v[j] = _ce(v[i], v[j])
    for shift in (4, 2, 1):
        partner = [None if x is None else pltpu.roll(x, shift, 0) for x in v]
        top = []
        for a in range(n):
            x, y = v[a], partner[n - 1 - a]
            top.append(y if x is None else (x if y is None else jnp.maximum(x, y)))
        v = top
        d = n // 2
        while d >= 1:
            for i in range(n):
                if i & d == 0:
                    v[i], v[i + d] = _ce(v[i], v[i + d])
            d //= 2
    return v


_CAND = [(a, b) for a in range(PEER_TOPK) for b in range(PEER_TOPK) if (a + 1) * (b + 1) <= PEER_TOPK]


def _count_prefix(pred, vals):
    pick = lambda m, a, b: jnp.where(m, a, b)
    m8 = pred(vals[7])
    m4 = pred(pick(m8, vals[11], vals[3]))
    m2 = pred(pick(m8, pick(m4, vals[13], vals[9]), pick(m4, vals[5], vals[1])))
    m1 = pred(pick(m8, pick(m4, pick(m2, vals[14], vals[12]), pick(m2, vals[10], vals[8])),
                   pick(m4, pick(m2, vals[6], vals[4]), pick(m2, vals[2], vals[0]))))
    cnt = pick(m8, 8.0, 0.0) + pick(m4, 4.0, 0.0) + pick(m2, 2.0, 0.0) + pick(m1, 1.0, 0.0)
    return cnt + pick(pred(vals[15]), 1.0, 0.0)


def _bf16_pair_words(x):
    b = pltpu.bitcast(x, jnp.uint32)
    b = (b + jnp.uint32(0x7FFF) + ((b >> 16) & jnp.uint32(1))) >> 16
    return b | (b << 16)


def _peer_route_kernel(xnt_ref, wq_ref, k1_ref, k2_ref, rho_ref, e2_ref, r1_ref, c1_ref):
    def scores(h):
        qt = jnp.dot(wq_ref[h * PEER_KEY_DIM:(h + 1) * PEER_KEY_DIM, :], xnt_ref[...],
                     preferred_element_type=F32)
        return (jnp.dot(k1_ref[h], qt[:PEER_HALF].astype(BF16), preferred_element_type=F32),
                jnp.dot(k2_ref[h], qt[PEER_HALF:].astype(BF16), preferred_element_type=F32))

    s = scores(0)
    for h in range(PEER_HEADS):
        s_next = scores(h + 1) if h + 1 < PEER_HEADS else None
        _peer_route_head(h, s[0], s[1], rho_ref, e2_ref, r1_ref, c1_ref)
        s = s_next


def _peer_route_head(h, s1, s2, rho_ref, e2_ref, r1_ref, c1_ref):
    tm = s1.shape[1]
    nslab = N_KEYS // SUBLANE
    s1 = [s1[SUBLANE * i:SUBLANE * (i + 1)] for i in range(nslab)]
    s2 = [s2[SUBLANE * i:SUBLANE * (i + 1)] for i in range(nslab)]
    v1 = _sorted_top16(s1)
    v2 = _sorted_top16(s2)

    sub = lax.broadcasted_iota(jnp.int32, (SUBLANE, tm), 0)
    cand = [v1[a] + v2[b] for a, b in _CAND]
    packed = []
    for base in range(0, len(cand), 8):
        slab = None
        for s, c in enumerate(cand[base:base + 8]):
            slab = c if slab is None else jnp.where(sub == s, c, slab)
        if len(cand) - base < 8:
            slab = jnp.where(sub < len(cand) - base, slab, -jnp.inf)
        packed.append(slab)
    best = _sorted_top16(packed)
    theta = best[PEER_TOPK - 1]
    z = jnp.ones((SUBLANE, tm), F32)
    for c in best[1:]:
        z = z + jnp.exp(c - best[0])
    inv_z = 1.0 / z

    r1, c1, rho, e2 = [], [], [], []
    for i in range(nslab):
        r1.append(_count_prefix(lambda v, x=s1[i]: x + v >= theta, v2))
        rho.append(_count_prefix(lambda v, x=s2[i]: v > x, v2))
        c1.append(jnp.exp(s1[i] - v1[0]) * inv_z)
        e2.append(jnp.exp(s2[i] - v2[0]))
    r1 = _bf16_pair_words(jnp.concatenate(r1, axis=0))
    c1 = _bf16_pair_words(jnp.concatenate(c1, axis=0))
    rho = jnp.concatenate(rho, axis=0).astype(BF16)
    e2 = jnp.concatenate(e2, axis=0).astype(BF16)
    for l in range(tm // LANE):
        lanes = slice(l * LANE, (l + 1) * LANE)
        r1_ref[h, l] = r1[:, lanes]
        c1_ref[h, l] = c1[:, lanes]
    for l in range(tm // PEER_STRIP):
        lanes = slice(l * PEER_STRIP, (l + 1) * PEER_STRIP)
        rho_ref[h, l * N_KEYS:(l + 1) * N_KEYS, :] = rho[:, lanes]
        e2_ref[h, l * N_KEYS:(l + 1) * N_KEYS, :] = e2[:, lanes]


def _peer_route(xn_t, wq_t, k1, k2, tm):
    t = xn_t.shape[1]
    sel = pl.BlockSpec((PEER_HEADS, tm // LANE, N_KEYS, LANE), lambda i: (0, i, 0, 0))
    shp = jax.ShapeDtypeStruct((PEER_HEADS, t // LANE, N_KEYS, LANE), jnp.uint32)
    tok = pl.BlockSpec((PEER_HEADS, tm // 2, PEER_STRIP), lambda i: (0, i, 0))
    shp16 = jax.ShapeDtypeStruct((PEER_HEADS, t // 2, PEER_STRIP), BF16)
    return pl.pallas_call(
        _peer_route_kernel,
        grid=(t // tm,),
        in_specs=[
            pl.BlockSpec((D_MODEL, tm), lambda i: (0, i)),
            pl.BlockSpec((PEER_HEADS * PEER_KEY_DIM, D_MODEL), lambda i: (0, 0)),
            pl.BlockSpec((PEER_HEADS, N_KEYS, PEER_HALF), lambda i: (0, 0, 0)),
            pl.BlockSpec((PEER_HEADS, N_KEYS, PEER_HALF), lambda i: (0, 0, 0)),
        ],
        out_specs=[tok, tok, sel, sel],
        out_shape=[shp16, shp16, shp, shp],
        compiler_params=_cparams("parallel"),
        name="peer_route",
    )(xn_t, wq_t, k1, k2)


def _gelu_tanh(x):
    k1 = -2.0 * math.sqrt(2.0 / math.pi) * LOG2_E
    t = x * (k1 + (k1 * 0.044715) * (x * x))
    return x / (1.0 + jnp.exp2(t))


def _load_row_repeated(ref, h, l, i, n):
    return ref[h, l, pl.ds(i, n, stride=0), :]


def _peer_dense_kernel(xn_ref, u_ref, vt_ref, rho_ref, e2_ref, r1_ref, c1_ref, o_ref, a_ref, *, rows):
    e = pl.program_id(1)

    @pl.when(e == 0)
    def _():
        o_ref[...] = jnp.zeros(o_ref.shape, F32)

    tt = xn_ref.shape[1]
    sub = BF16_ROWS
    strip = PEER_STRIP
    zero = jnp.zeros((sub, strip), BF16)

    def row_tile(ref, h, s, i):
        words = jnp.concatenate([_load_row_repeated(ref, h, s * (strip // LANE) + l, i, sub // 2)
                                 for l in range(strip // LANE)], axis=1)
        return pltpu.bitcast(words, BF16)

    blk = PEER_BLOCK_ROWS * N_KEYS
    nblk = rows // PEER_BLOCK_ROWS

    def first_matmul(k):
        return jnp.dot(u_ref[k * blk:(k + 1) * blk, :].astype(BF16), xn_ref[...], preferred_element_type=F32)

    a_ref[0:blk, :] = _gelu_tanh(first_matmul(0).astype(BF16))
    for k in range(nblk):
        if k + 1 < nblk:
            a_ref[(k + 1) * blk:(k + 2) * blk, :] = _gelu_tanh(first_matmul(k + 1).astype(BF16))
        for s in range(tt // strip):
            lanes = slice(s * strip, (s + 1) * strip)
            for ii in range(PEER_BLOCK_ROWS):
                i = k * PEER_BLOCK_ROWS + ii
                w = [None] * (N_KEYS // sub)
                for h in range(PEER_HEADS):
                    r = row_tile(r1_ref, h, s, i)
                    c = row_tile(c1_ref, h, s, i)
                    for jb in range(N_KEYS // sub):
                        keys = slice(s * N_KEYS + jb * sub, s * N_KEYS + (jb + 1) * sub)
                        term = c * jnp.where(rho_ref[h, keys, :] < r, e2_ref[h, keys, :], zero)
                        w[jb] = term if w[jb] is None else w[jb] + term
                for jb in range(N_KEYS // sub):
                    lo = k * blk + ii * N_KEYS + jb * sub
                    a_ref[lo:lo + sub, lanes] = a_ref[lo:lo + sub, lanes] * w[jb]
        o_ref[...] += lax.dot_general(vt_ref[k * blk:(k + 1) * blk, :].astype(BF16),
                                      a_ref[k * blk:(k + 1) * blk, :], (((0,), (0,)), ((), ())),
                                      preferred_element_type=F32)


def _peer_dense(xn_t, eu, ev, layer, rho, e2, r1, c1, tt, te):
    t = xn_t.shape[1]
    ne = eu.shape[1]
    rows = te // N_KEYS
    tok = pl.BlockSpec((PEER_HEADS, tt // 2, PEER_STRIP), lambda j, e: (0, j, 0))
    sel = pl.BlockSpec((PEER_HEADS, tt // LANE, rows, LANE), lambda j, e: (0, j, e, 0))
    return pl.pallas_call(
        functools.partial(_peer_dense_kernel, rows=rows),
        grid=(t // tt, ne // te),
        in_specs=[
            pl.BlockSpec((D_MODEL, tt), lambda j, e: (0, j)),
            pl.BlockSpec((None, te, D_MODEL), lambda j, e: (layer, e, 0)),
            pl.BlockSpec((None, te, D_MODEL), lambda j, e: (layer, e, 0)),
            tok, tok, sel, sel,
        ],
        out_specs=pl.BlockSpec((D_MODEL, tt), lambda j, e: (0, j)),
        out_shape=jax.ShapeDtypeStruct((D_MODEL, t), F32),
        scratch_shapes=[pltpu.VMEM((te, tt), BF16)],
        compiler_params=_cparams("parallel", "arbitrary"),
        name="peer_dense",
    )(xn_t, eu, ev, rho, e2, r1, c1)


def _ple_kernel(x_ref, pe_ref, p_ref, g_ref, wg_ref, wp_ref, gfin_ref, o_ref, *, final):
    x = x_ref[...] + pe_ref[...].T
    gate = jax.nn.sigmoid(jnp.dot(_rms(x, g_ref[...]).astype(BF16), wg_ref[...], preferred_element_type=F32))
    x = x + gate * jnp.dot(p_ref[...].astype(BF16), wp_ref[...], preferred_element_type=F32)
    if final:
        x = _rms(x, gfin_ref[...])
    o_ref[...] = x


def _ple(x, pe, p, g, wg, wp, gfin, final, tm):
    t = x.shape[0]
    row = lambda w: pl.BlockSpec((tm, w), lambda i: (i, 0))
    full = lambda a, b: pl.BlockSpec((a, b), lambda i: (0, 0))
    return pl.pallas_call(
        functools.partial(_ple_kernel, final=final),
        grid=(t // tm,),
        in_specs=[row(D_MODEL), pl.BlockSpec((D_MODEL, tm), lambda i: (0, i)), row(D_PLE), full(1, D_MODEL),
                  full(D_MODEL, D_MODEL), full(D_PLE, D_MODEL), full(1, D_MODEL)],
        out_specs=row(D_MODEL),
        out_shape=jax.ShapeDtypeStruct((t, D_MODEL), F32),
        compiler_params=_cparams("parallel"),
        name="ple",
    )(x, pe, p, g, wg, wp, gfin)


def _rope_tables(seq):
    n_rows = seq // GRID_W
    row = jnp.repeat(jnp.arange(n_rows, dtype=jnp.int32), GRID_W).astype(F32)
    col = jnp.tile(jnp.arange(GRID_W, dtype=jnp.int32), n_rows).astype(F32)
    quarter = HEAD_DIM // 4
    inv = ROPE_THETA ** (-jnp.arange(quarter, dtype=F32) / quarter)
    ang_r = row[:, None] * inv[None, :]
    ang_c = col[:, None] * inv[None, :]
    cos = jnp.concatenate([jnp.cos(ang_r)] * 2 + [jnp.cos(ang_c)] * 2, axis=1)
    sin = jnp.concatenate([-jnp.sin(ang_r), jnp.sin(ang_r), -jnp.sin(ang_c), jnp.sin(ang_c)], axis=1)
    return jnp.tile(cos, (1, N_Q_HEADS)), jnp.tile(sin, (1, N_Q_HEADS))


class _Tiles(NamedTuple):
    tokens: int
    queries: int
    keys: int
    route_tokens: int
    peer_tokens: int
    peer_experts: int


def _tiles(n_batch, seq):
    t = n_batch * seq
    return _Tiles(tokens=min(512, seq), queries=min(1024, seq), keys=min(512, seq // 2),
                  route_tokens=min(2 * LANE, t), peer_tokens=min(1024, t), peer_experts=1024)


def _forward(x, p, norm_mix_g, w_in, lam_re, lam_im, log_dt, b_re, b_im, c_re, c_im, d_skip, w_glu,
             b_glu, q_norm_g, k_norm_g, ssm_out_g, attn_out_g, w_out, norm_ffn_g, w_query, sub_keys1,
             sub_keys2, expert_u, expert_v, norm_ple_g, w_ple_gate, w_ple_proj, final_norm_g):
    n_batch, seq, _ = x.shape
    depth = w_in.shape[0]
    t = n_batch * seq
    tl = _tiles(n_batch, seq)
    tm, tq, tk, tt, te = tl.tokens, tl.queries, tl.keys, tl.peer_tokens, tl.peer_experts
    nc = t // SSM_CHUNK

    cq, sq = _rope_tables(seq)
    bd = jnp.kron(jnp.eye(N_Q_HEADS, dtype=F32), jnp.ones((HEAD_DIM, HEAD_DIM), F32)).astype(BF16)
    row = lambda a: a.reshape(1, -1).astype(F32)

    xf = x.reshape(t, D_MODEL)
    for i in range(depth):
        u, ub, qt, k, vt = _in_proj(xf, row(norm_mix_g[i]), w_in[i].astype(BF16), cq, sq,
                                row(jnp.tile(q_norm_g[i], N_Q_HEADS)), row(jnp.tile(k_norm_g[i], N_KV_HEADS)),
                                bd, seq, tk)

        grp = lambda a: jnp.transpose(a, (1, 0, 2))[:, :, None, :]
        ldt = jnp.broadcast_to(log_dt[i][:, :, None], (2, SSM_G, SSM_N))
        mt, et, fm, dec = _ssm_params(
            grp(lam_re[i]), grp(lam_im[i]), grp(ldt),
            jnp.transpose(b_re[i], (1, 0, 3, 2)), jnp.transpose(b_im[i], (1, 0, 3, 2)),
            jnp.transpose(c_re[i], (1, 0, 2, 3)), jnp.transpose(c_im[i], (1, 0, 2, 3)))
        ug = (ub.reshape(nc, SSM_CHUNK, SSM_G, SSM_P)
              .transpose(2, 0, 1, 3).reshape(SSM_G, nc, SSM_LP))
        yg = _ssm_conv(ug, mt, et, fm, dec, n_batch)
        y_ssm = yg.reshape(SSM_G, nc, SSM_CHUNK, SSM_P).transpose(1, 2, 0, 3).reshape(t, D_SSM)

        attn = _attention(qt, k, vt, n_batch, seq, tq)

        xf, xnt = _out_proj(xf, y_ssm, u, attn, row(d_skip[i]), w_glu[i].astype(BF16), row(b_glu[i]),
                            row(ssm_out_g[i]), row(attn_out_g[i]), w_out[i].astype(BF16),
                            row(norm_ffn_g[i]), tm)

        rho, e2, r1, c1 = _peer_route(xnt, w_query[i].T.astype(BF16), sub_keys1[i].astype(BF16),
                                      sub_keys2[i].astype(BF16), tl.route_tokens)
        peer_t = _peer_dense(xnt, expert_u, expert_v, i,
                             rho, e2, r1, c1, tt, te)

        xf = _ple(xf, peer_t, p[i].reshape(t, D_PLE), row(norm_ple_g[i]), w_ple_gate[i].astype(BF16),
                  w_ple_proj[i].astype(BF16), row(final_norm_g), i == depth - 1, tm)
    return xf.reshape(n_batch, seq, D_MODEL)


def kernel(x, p, norm_mix_g, w_in, lam_re, lam_im, log_dt, b_re, b_im, c_re, c_im, d_skip, w_glu, b_glu, q_norm_g, k_norm_g, ssm_out_g, attn_out_g, w_out, norm_ffn_g, w_query, sub_keys1, sub_keys2, expert_u, expert_v, norm_ple_g, w_ple_gate, w_ple_proj, final_norm_g):
    return _forward(x, p, norm_mix_g, w_in, lam_re, lam_im, log_dt, b_re, b_im, c_re, c_im, d_skip, w_glu,
                    b_glu, q_norm_g, k_norm_g, ssm_out_g, attn_out_g, w_out, norm_ffn_g, w_query,
                    sub_keys1, sub_keys2, expert_u, expert_v, norm_ple_g, w_ple_gate, w_ple_proj,
                    final_norm_g)
```
